```python
import jax, jax.numpy as jnp
from jax import lax
import numpy as np


D_MODEL = 2048
BATCH = 4
SEQ = 8192
DEPTH = 1

D_FF = 5632
GLA_HEADS = 4
GLA_DK = 128
GLA_DV = 256
GLA_RANK = 16
GLA_TAU = 16.0
GLA_CHUNK = 64
MOBA_HEADS = 8
MOBA_HD = 128
MOBA_BLOCK = 256
MOBA_TOPK = 3
MOBA_QCHUNK = 16
NORM_EPS = 1e-6

GLA_QK_W = GLA_HEADS * GLA_DK
GLA_V_W = GLA_HEADS * GLA_DV
MOBA_W = MOBA_HEADS * MOBA_HD
IN_SPLITS = (GLA_QK_W, GLA_QK_W, GLA_V_W, GLA_V_W, GLA_RANK,
             MOBA_W, MOBA_W, MOBA_W, D_MODEL, D_MODEL)
D_IN = sum(IN_SPLITS)

kernel_name = 'hybrid_gla_moba_macaron_block'


def rms_norm(x, g):
    xf = x.astype(jnp.float32)
    y = xf * lax.rsqrt(jnp.mean(xf * xf, axis=-1, keepdims=True) + NORM_EPS)
    return (y * g.astype(jnp.float32)).astype(x.dtype)


def swiglu(h, w_gate, w_up, w_down):
    return (jax.nn.silu(h @ w_gate) * (h @ w_up)) @ w_down


def alibi_slopes(n):
    return jnp.exp2(-8.0 * jnp.arange(1, n + 1, dtype=jnp.float32) / n)


def gla_attention(q, k, v, log_a):
    B, S, H, dk = q.shape
    dv = v.shape[-1]
    C = GLA_CHUNK
    nc = S // C

    def to_chunks(t):
        return t.reshape(B, nc, C, H, t.shape[-1]).transpose(1, 0, 3, 2, 4)

    causal = jnp.tril(jnp.ones((C, C), dtype=bool))[None, None, :, :, None]

    def step(state, xs):
        qc, kc, vc, ac = xs
        b = jnp.cumsum(ac, axis=2)
        o_inter = jnp.einsum('bhck,bhkv->bhcv', qc * jnp.exp(b), state)
        diff = b[:, :, :, None, :] - b[:, :, None, :, :]
        decay = jnp.exp(jnp.where(causal, diff, -jnp.inf))
        attn = jnp.einsum('bhik,bhjk,bhijk->bhij', qc, kc, decay)
        o_intra = jnp.einsum('bhij,bhjv->bhiv', attn, vc)
        b_last = b[:, :, -1:, :]
        new_state = (jnp.exp(b_last[:, :, 0, :])[..., None] * state
                     + jnp.einsum('bhck,bhcv->bhkv', kc * jnp.exp(b_last - b), vc))
        return new_state, o_inter + o_intra

    state0 = jnp.zeros((B, H, dk, dv), jnp.float32)
    _, out = lax.scan(step, state0, (to_chunks(q), to_chunks(k), to_chunks(v), to_chunks(log_a)))
    return out.transpose(1, 0, 3, 2, 4).reshape(B, S, H, dv)


def moba_attention(q, k, v):
    B, S, H, hd = q.shape
    BLK, QC = MOBA_BLOCK, MOBA_QCHUNK
    nb = -(-S // BLK)
    s_pad = nb * BLK
    pad = ((0, 0), (0, s_pad - S), (0, 0), (0, 0))
    q, k, v = [jnp.pad(t, pad).transpose(0, 2, 1, 3) for t in (q, k, v)]
    kb = k.reshape(B, H, nb, BLK, hd)
    vb = v.reshape(B, H, nb, BLK, hd)
    k_mean = jnp.mean(kb.astype(jnp.float32), axis=3)

    pos = jnp.arange(s_pad, dtype=jnp.int32)
    q_blk = pos // BLK
    gate = jnp.einsum('bhsd,bhnd->bhsn', q.astype(jnp.float32), k_mean)
    past = jnp.arange(nb, dtype=jnp.int32)[None, :] < q_blk[:, None]
    gate = jnp.where(past, gate, -jnp.inf)
    k_sel = min(MOBA_TOPK, nb)
    _, sel_idx = lax.top_k(gate, k_sel)
    sel_valid = jnp.arange(k_sel, dtype=jnp.int32)[None, :] < q_blk[:, None]

    slopes = alibi_slopes(H)[None, :, None, None]
    scale = hd ** -0.5
    bi = jnp.arange(B)[:, None, None, None]
    hi = jnp.arange(H)[None, :, None, None]
    offs = jnp.arange(BLK, dtype=jnp.int32)

    def chunk(c):
        t0 = c * QC
        qc = lax.dynamic_slice_in_dim(q, t0, QC, axis=2)
        idx = lax.dynamic_slice_in_dim(sel_idx, t0, QC, axis=2)
        valid = lax.dynamic_slice_in_dim(sel_valid, t0, QC, axis=0)
        tq = t0 + jnp.arange(QC, dtype=jnp.int32)
        k_g = kb[bi, hi, idx]
        v_g = vb[bi, hi, idx]
        s_sel = jnp.einsum('bhqd,bhqrkd->bhqrk', qc, k_g,
                           preferred_element_type=jnp.float32) * scale
        dist_sel = (tq[:, None, None] - (idx[..., None] * BLK + offs)).astype(jnp.float32)
        s_sel = s_sel - slopes[..., None] * dist_sel
        s_sel = jnp.where(valid[None, None, :, :, None], s_sel, -jnp.inf)
        blk = t0 // BLK
        k_own = lax.dynamic_index_in_dim(kb, blk, axis=2, keepdims=False)
        v_own = lax.dynamic_index_in_dim(vb, blk, axis=2, keepdims=False)
        pos_own = blk * BLK + offs
        dist_own = (tq[:, None] - pos_own[None, :]).astype(jnp.float32)
        s_own = jnp.einsum('bhqd,bhkd->bhqk', qc, k_own,
                           preferred_element_type=jnp.float32) * scale - slopes * dist_own
        s_own = jnp.where(pos_own[None, :] <= tq[:, None], s_own, -jnp.inf)
        s = jnp.concatenate([s_sel.reshape(B, H, QC, k_sel * BLK), s_own], axis=-1)
        p = jax.nn.softmax(s, axis=-1).astype(v.dtype)
        p_sel = p[..., :k_sel * BLK].reshape(B, H, QC, k_sel, BLK)
        p_own = p[..., k_sel * BLK:]
        return (jnp.einsum('bhqrk,bhqrkd->bhqd', p_sel, v_g)
                + jnp.einsum('bhqk,bhkd->bhqd', p_own, v_own))

    out = lax.map(chunk, jnp.arange(s_pad // QC, dtype=jnp.int32))
    out = out.transpose(1, 0, 3, 2, 4).reshape(B, s_pad, H * hd)
    return out[:, :S]


def setup_inputs(seed: int = 0) -> dict:
    key = jax.random.key(seed)
    ks = jax.random.split(key, 20)

    def normal(k, shape, fan_in):
        return jax.random.normal(k, shape, jnp.float32) * fan_in ** -0.5

    def gain(k, shape):
        return 1.0 + 0.02 * jax.random.normal(k, shape, jnp.float32)

    L = DEPTH
    return {
        'x': jax.random.normal(ks[0], (BATCH, SEQ, D_MODEL), jnp.float32),
        'ffn1_norm': gain(ks[1], (L, D_MODEL)),
        'ffn1_w_gate': normal(ks[2], (L, D_MODEL, D_FF), D_MODEL),
        'ffn1_w_up': normal(ks[3], (L, D_MODEL, D_FF), D_MODEL),
        'ffn1_w_down': normal(ks[4], (L, D_FF, D_MODEL), D_FF),
        'mix_norm': gain(ks[5], (L, D_MODEL)),
        'w_in': normal(ks[6], (L, D_MODEL, D_IN), D_MODEL),
        'gla_w_alpha': normal(ks[7], (L, GLA_RANK, GLA_QK_W), GLA_RANK),
        'gla_b_alpha': 0.1 * jax.random.normal(ks[8], (L, GLA_QK_W), jnp.float32),
        'gla_out_norm': gain(ks[9], (L, GLA_DV)),
        'w_branch_gla': normal(ks[10], (L, GLA_V_W, D_MODEL), GLA_V_W),
        'w_branch_moba': normal(ks[11], (L, MOBA_W, D_MODEL), MOBA_W),
        'w_out': normal(ks[12], (L, D_MODEL, D_MODEL), D_MODEL),
        'ffn2_norm': gain(ks[13], (L, D_MODEL)),
        'ffn2_w_gate': normal(ks[14], (L, D_MODEL, D_FF), D_MODEL),
        'ffn2_w_up': normal(ks[15], (L, D_MODEL, D_FF), D_MODEL),
        'ffn2_w_down': normal(ks[16], (L, D_FF, D_MODEL), D_FF),
        'final_norm': gain(ks[17], (D_MODEL,)),
    }


def reference(x, ffn1_norm, ffn1_w_gate, ffn1_w_up, ffn1_w_down, mix_norm, w_in,
              gla_w_alpha, gla_b_alpha, gla_out_norm, w_branch_gla, w_branch_moba,
              w_out, ffn2_norm, ffn2_w_gate, ffn2_w_up, ffn2_w_down, final_norm):
    B, S, _ = x.shape
    split_at = np.cumsum(IN_SPLITS)[:-1].tolist()
    for l in range(DEPTH):
        x = x + 0.5 * swiglu(rms_norm(x, ffn1_norm[l]), ffn1_w_gate[l], ffn1_w_up[l], ffn1_w_down[l])

        h = rms_norm(x, mix_norm[l])
        proj = h @ w_in[l]
        g_q, g_k, g_v, g_r, g_lr, m_q, m_k, m_v, gate_a, gate_b = jnp.split(proj, split_at, axis=-1)

        log_a = jax.nn.log_sigmoid((g_lr @ gla_w_alpha[l] + gla_b_alpha[l]).astype(jnp.float32)) / GLA_TAU
        qa = g_q.astype(jnp.float32).reshape(B, S, GLA_HEADS, GLA_DK) * GLA_DK ** -0.5
        ka = g_k.astype(jnp.float32).reshape(B, S, GLA_HEADS, GLA_DK)
        va = g_v.astype(jnp.float32).reshape(B, S, GLA_HEADS, GLA_DV)
        oa = gla_attention(qa, ka, va, log_a.reshape(B, S, GLA_HEADS, GLA_DK))
        oa = rms_norm(oa, gla_out_norm[l]).astype(x.dtype)
        oa = oa.reshape(B, S, GLA_V_W) * jax.nn.silu(g_r)
        y_a = oa @ w_branch_gla[l]

        ob = moba_attention(m_q.reshape(B, S, MOBA_HEADS, MOBA_HD),
                            m_k.reshape(B, S, MOBA_HEADS, MOBA_HD),
                            m_v.reshape(B, S, MOBA_HEADS, MOBA_HD))
        y_b = ob @ w_branch_moba[l]

        y = (jax.nn.sigmoid(gate_a) * y_a + jax.nn.sigmoid(gate_b) * y_b) @ w_out[l]
        x = x + y

        x = x + 0.5 * swiglu(rms_norm(x, ffn2_norm[l]), ffn2_w_gate[l], ffn2_w_up[l], ffn2_w_down[l])
    return rms_norm(x, final_norm)
```

```python
import functools

import jax
import jax.numpy as jnp
from jax import lax
from jax.experimental import pallas as pl
from jax.experimental.pallas import tpu as pltpu

F32 = jnp.float32
BF16 = jnp.bfloat16

D_MODEL = 2048
D_FF = 5632
GLA_HEADS = 4
GLA_DK = 128
GLA_DV = 256
GLA_RANK = 16
GLA_TAU = 16.0
MOBA_HEADS = 8
MOBA_HD = 128
MOBA_BLOCK = 256
MOBA_TOPK = 3
NORM_EPS = 1e-6

GLA_QK_W = GLA_HEADS * GLA_DK
GLA_V_W = GLA_HEADS * GLA_DV
MOBA_W = MOBA_HEADS * MOBA_HD

COL_GQ = 0
COL_GK = COL_GQ + GLA_QK_W
COL_GV = COL_GK + GLA_QK_W
COL_GR = COL_GV + GLA_V_W
COL_MQ = COL_GR + GLA_V_W
COL_MK = COL_MQ + MOBA_W
COL_MV = COL_MK + MOBA_W
COL_GA = COL_MV + MOBA_W
COL_GB = COL_GA + D_MODEL
PROJ_W = COL_GB + D_MODEL
LR_START = GLA_QK_W * 2 + GLA_V_W * 2

LANES_V7X = 128
SUBLANES_V7X = 8
VMEM_BYTES_V7X = 64 << 20
VMEM_CAP = VMEM_BYTES_V7X - (6 << 20)

FFN_TM = 1024
FFN_TF = 512
PROJ_TM = 1024
PROJ_TN = 1024
GLA_CHUNK = 256
GLA_BASE = SUBLANES_V7X
MERGE_TM = 512
NORM_ROWS = 256

MASK_VALUE = -1e30


def _vmem_limit(nbytes):
    return int(min(VMEM_CAP, nbytes))


def _dot(a, b):
    return jnp.dot(a, b, preferred_element_type=F32)


def _dot_nt(a, b, precision=None):
    return lax.dot_general(a, b, (((1,), (1,)), ((), ())),
                           preferred_element_type=F32, precision=precision)


def _dot_tn(a, b):
    return lax.dot_general(a, b, (((0,), (0,)), ((), ())), preferred_element_type=F32)


def _rms_norm_rows(x_ref, g_ref, out_ref, rows):
    def body(r, carry):
        sl = pl.ds(pl.multiple_of(r * NORM_ROWS, NORM_ROWS), NORM_ROWS)
        x = x_ref[sl, :].astype(F32)
        ms = jnp.mean(x * x, axis=-1, keepdims=True)
        out_ref[sl, :] = (x * lax.rsqrt(ms + NORM_EPS) * g_ref[...]).astype(out_ref.dtype)
        return carry
    lax.fori_loop(0, rows // NORM_ROWS, body, 0)


def _ffn_kernel(x_ref, g_ref, wg_ref, wu_ref, wd_ref, fg_ref, o_ref, xn_ref, *, final_norm):
    j = pl.program_id(1)

    @pl.when(j == 0)
    def _():
        _rms_norm_rows(x_ref, g_ref, xn_ref, FFN_TM)
        o_ref[...] = x_ref[...]

    xn = xn_ref[...]
    hg = _dot(xn, wg_ref[...])
    hu = _dot(xn, wu_ref[...])
    a = (0.5 * hg * jax.nn.sigmoid(hg)) * hu
    o_ref[...] += _dot(a.astype(BF16), wd_ref[...])

    if final_norm:
        @pl.when(j == pl.num_programs(1) - 1)
        def _():
            _rms_norm_rows(o_ref, fg_ref, o_ref, FFN_TM)


def _ffn(x, norm_g, wg, wu, wd, final_g, final_norm):
    t = x.shape[0]
    vmem = (2 * 2 * FFN_TM * D_MODEL * 4
            + FFN_TM * D_MODEL * 2
            + 2 * 3 * D_MODEL * FFN_TF * 2
            + 4 * FFN_TM * FFN_TF * 4
            + (4 << 20))
    return pl.pallas_call(
        functools.partial(_ffn_kernel, final_norm=final_norm),
        grid=(t // FFN_TM, D_FF // FFN_TF),
        in_specs=[
            pl.BlockSpec((FFN_TM, D_MODEL), lambda i, j: (i, 0)),
            pl.BlockSpec((1, D_MODEL), lambda i, j: (0, 0)),
            pl.BlockSpec((D_MODEL, FFN_TF), lambda i, j: (0, j)),
            pl.BlockSpec((D_MODEL, FFN_TF), lambda i, j: (0, j)),
            pl.BlockSpec((FFN_TF, D_MODEL), lambda i, j: (j, 0)),
            pl.BlockSpec((1, D_MODEL), lambda i, j: (0, 0)),
        ],
        out_specs=pl.BlockSpec((FFN_TM, D_MODEL), lambda i, j: (i, 0)),
        out_shape=jax.ShapeDtypeStruct((t, D_MODEL), F32),
        scratch_shapes=[pltpu.VMEM((FFN_TM, D_MODEL), BF16)],
        compiler_params=pltpu.CompilerParams(
            dimension_semantics=("parallel", "arbitrary"),
            vmem_limit_bytes=_vmem_limit(vmem)),
        name="ffn_final" if final_norm else "ffn",
    )(x, norm_g, wg, wu, wd, final_g)


def _log_sigmoid(z):
    return jnp.minimum(z, 0.0) - jnp.log1p(jnp.exp(-jnp.abs(z)))


def _in_proj_kernel(x_ref, g_ref, w_ref, cs_ref, wlr_ref, wa_ref, ba_ref,
                    proj_ref, la_ref, xn_ref):
    j = pl.program_id(1)

    @pl.when(j == 0)
    def _():
        _rms_norm_rows(x_ref, g_ref, xn_ref, PROJ_TM)
        lr = _dot(xn_ref[...], wlr_ref[...])
        z = jnp.dot(lr, wa_ref[...], preferred_element_type=F32,
                    precision=lax.Precision.HIGHEST) + ba_ref[...]
        la_ref[...] = _log_sigmoid(z) * (1.0 / GLA_TAU)

    proj_ref[...] = (_dot(xn_ref[...], w_ref[...]) * cs_ref[...]).astype(proj_ref.dtype)


def _in_proj(x, norm_g, w_main, col_scale, w_lr, w_alpha, b_alpha):
    t = x.shape[0]
    vmem = (2 * PROJ_TM * D_MODEL * 4 + PROJ_TM * D_MODEL * 2
            + 2 * D_MODEL * PROJ_TN * 2 + 2 * PROJ_TM * PROJ_TN * 2
            + 2 * PROJ_TM * GLA_QK_W * 4 + 2 * PROJ_TM * PROJ_TN * 4
            + 2 * D_MODEL * LANES_V7X * 2 + (4 << 20))
    return pl.pallas_call(
        _in_proj_kernel,
        grid=(t // PROJ_TM, PROJ_W // PROJ_TN),
        in_specs=[
            pl.BlockSpec((PROJ_TM, D_MODEL), lambda i, j: (i, 0)),
            pl.BlockSpec((1, D_MODEL), lambda i, j: (0, 0)),
            pl.BlockSpec((D_MODEL, PROJ_TN), lambda i, j: (0, j)),
            pl.BlockSpec((1, PROJ_TN), lambda i, j: (0, j)),
            pl.BlockSpec((D_MODEL, LANES_V7X), lambda i, j: (0, 0)),
            pl.BlockSpec((LANES_V7X, GLA_QK_W), lambda i, j: (0, 0)),
            pl.BlockSpec((1, GLA_QK_W), lambda i, j: (0, 0)),
        ],
        out_specs=[
            pl.BlockSpec((PROJ_TM, PROJ_TN), lambda i, j: (i, j)),
            pl.BlockSpec((PROJ_TM, GLA_QK_W), lambda i, j: (i, 0)),
        ],
        out_shape=[
            jax.ShapeDtypeStruct((t, PROJ_W), BF16),
            jax.ShapeDtypeStruct((t, GLA_QK_W), F32),
        ],
        scratch_shapes=[pltpu.VMEM((PROJ_TM, D_MODEL), BF16)],
        compiler_params=pltpu.CompilerParams(
            dimension_semantics=("parallel", "arbitrary"),
            vmem_limit_bytes=_vmem_limit(vmem)),
        name="in_proj",
    )(x, norm_g, w_main, col_scale, w_lr, w_alpha, b_alpha)


def _bcast_rows(v, s, c):
    return jnp.broadcast_to(v, (c // s, s, GLA_DK)).reshape(c, GLA_DK)


def _gla_head(q, k, v, la, gr, gn, st_ref):
    c = q.shape[0]
    row = lax.broadcasted_iota(jnp.int32, (c, 1), 0)
    ri = lax.broadcasted_iota(jnp.int32, (c, c), 0)
    ci = lax.broadcasted_iota(jnp.int32, (c, c), 1)
    rxc = ri ^ ci

    cs = la.reshape(c // GLA_BASE, GLA_BASE, GLA_DK)
    sub = lax.broadcasted_iota(jnp.int32, cs.shape, 1)
    sh = 1
    while sh < GLA_BASE:
        cs = cs + jnp.where(sub >= sh, pltpu.roll(cs, sh, axis=1), 0.0)
        sh *= 2
    cs = cs.reshape(c, GLA_DK)

    p = _dot_nt((q * jnp.exp(cs)).astype(BF16), (k * jnp.exp(-cs)).astype(BF16))
    attn = jnp.where(rxc < GLA_BASE, jnp.where(ci <= ri, p, 0.0), 0.0)

    s = GLA_BASE
    while s < c:
        ends = cs.reshape(c // s, s, GLA_DK)[:, s - 1:s, :]
        prev = jnp.concatenate([jnp.zeros_like(ends[:1]), ends[:-1]], axis=0)
        odd = ((row // s) & 1) == 1
        qe = jnp.where(odd, jnp.exp(cs), 0.0)
        ke = jnp.where(odd, 0.0, jnp.exp(_bcast_rows(ends, s, c) - cs))
        p = _dot_nt((q * qe).astype(BF16), (k * ke).astype(BF16))
        attn = attn + jnp.where(rxc < 2 * s, p, 0.0)
        cs = cs + jnp.where(odd, _bcast_rows(prev, s, c), 0.0)
        s *= 2

    b_last = cs[c - 1:c, :]
    st = st_ref[...]
    o = _dot_nt((q * jnp.exp(cs)).astype(BF16), st.astype(BF16))
    o = o + _dot(attn.astype(BF16), v)
    kd = (k * jnp.exp(b_last - cs)).astype(BF16)
    st_ref[...] = st * jnp.exp(b_last) + _dot_tn(v, kd)

    ms = jnp.mean(o * o, axis=-1, keepdims=True)
    y = o * lax.rsqrt(ms + NORM_EPS) * gn
    g = gr.astype(F32)
    return y * (g * jax.nn.sigmoid(g))


def _gla_kernel(q_ref, k_ref, v_ref, gr_ref, la_ref, gn_ref, o_ref, st_ref):
    @pl.when(pl.program_id(1) == 0)
    def _():
        st_ref[...] = jnp.zeros_like(st_ref)

    gn = gn_ref[...]
    for h in range(GLA_HEADS):
        ks = slice(h * GLA_DK, (h + 1) * GLA_DK)
        vs = slice(h * GLA_DV, (h + 1) * GLA_DV)
        out = _gla_head(q_ref[:, ks].astype(F32), k_ref[:, ks].astype(F32), v_ref[:, vs],
                        la_ref[:, ks], gr_ref[:, vs], gn, st_ref.at[h])
        o_ref[:, vs] = out.astype(o_ref.dtype)


def _gla(proj3, la3, gn):
    b, s, _ = proj3.shape
    c = GLA_CHUNK
    return pl.pallas_call(
        _gla_kernel,
        grid=(b, s // c),
        in_specs=[
            pl.BlockSpec((None, c, GLA_QK_W), lambda i, j: (i, j, COL_GQ // GLA_QK_W)),
            pl.BlockSpec((None, c, GLA_QK_W), lambda i, j: (i, j, COL_GK // GLA_QK_W)),
            pl.BlockSpec((None, c, GLA_V_W), lambda i, j: (i, j, COL_GV // GLA_V_W)),
            pl.BlockSpec((None, c, GLA_V_W), lambda i, j: (i, j, COL_GR // GLA_V_W)),
            pl.BlockSpec((None, c, GLA_QK_W), lambda i, j: (i, j, 0)),
            pl.BlockSpec((1, GLA_DV), lambda i, j: (0, 0)),
        ],
        out_specs=pl.BlockSpec((None, c, GLA_V_W), lambda i, j: (i, j, 0)),
        out_shape=jax.ShapeDtypeStruct((b, s, GLA_V_W), BF16),
        scratch_shapes=[pltpu.VMEM((GLA_HEADS, GLA_DV, GLA_DK), F32)],
        compiler_params=pltpu.CompilerParams(
            dimension_semantics=("parallel", "arbitrary"),
            vmem_limit_bytes=_vmem_limit(32 << 20)),
        name="gla",
    )(proj3, proj3, proj3, proj3, la3, gn)


def _moba_kernel(slope_ref, q_ref, k_ref, v_ref, o_ref, vt_ref, kmean_ref, selb_ref):
    h = pl.program_id(1)
    i = pl.program_id(2)
    blk = MOBA_BLOCK
    nb = k_ref.shape[0] // blk
    slope = slope_ref[h]

    @pl.when(i == 0)
    def _():
        def prep(j, carry):
            sl = pl.ds(pl.multiple_of(j * blk, blk), blk)
            vt_ref[:, sl] = v_ref[sl, :].T
            kmean_ref[pl.ds(j, 1), :] = jnp.mean(k_ref[sl, :].astype(F32), axis=0, keepdims=True)
            return carry
        lax.fori_loop(0, nb, prep, 0)

    q = q_ref[...]

    gate = _dot_nt(kmean_ref[...], q.astype(F32), precision=lax.Precision.HIGHEST)
    bidx = lax.broadcasted_iota(jnp.int32, gate.shape, 0)
    gate = jnp.where(bidx < i, gate, -jnp.inf)
    selb = jnp.full(gate.shape, MASK_VALUE, F32)
    for _ in range(MOBA_TOPK):
        m = jnp.max(gate, axis=0, keepdims=True)
        first = jnp.min(jnp.where(gate == m, bidx, nb), axis=0, keepdims=True)
        pick = bidx == jnp.where(m > -jnp.inf, first, -1)
        selb = jnp.where(pick, 0.0, selb)
        gate = jnp.where(pick, -jnp.inf, gate)
    selb_ref[...] = selb

    kpos = lax.broadcasted_iota(jnp.int32, (blk, blk), 0)
    qpos = lax.broadcasted_iota(jnp.int32, (blk, blk), 1)
    bias0 = kpos.astype(F32) * slope

    def tile(j, row_bias, m, l, acc, causal):
        sl = pl.ds(pl.multiple_of(j * blk, blk), blk)
        s = _dot_nt(k_ref[sl, :], q) + bias0 + row_bias
        if causal:
            s = jnp.where(kpos <= qpos, s, MASK_VALUE)
        m_new = jnp.maximum(m, jnp.max(s, axis=0, keepdims=True))
        alpha = jnp.exp(m - m_new)
        p = jnp.exp(s - m_new)
        l = alpha * l + jnp.sum(p, axis=0, keepdims=True)
        acc = alpha * acc + _dot(vt_ref[:, sl], p.astype(BF16))
        return m_new, l, acc

    m0 = jnp.full((1, blk), MASK_VALUE, F32)
    l0 = jnp.zeros((1, blk), F32)
    acc0 = jnp.zeros((MOBA_HD, blk), F32)
    carry = tile(i, jnp.zeros((1, blk), F32), m0, l0, acc0, True)

    def body(j, carry):
        row_bias = selb_ref[pl.ds(j, 1), :] + slope * ((j - i) * blk).astype(F32)
        return tile(j, row_bias, *carry, False)

    m, l, acc = lax.fori_loop(0, i, body, carry)
    o_ref[...] = (acc / l).T.astype(o_ref.dtype)


def _moba(proj3, slopes):
    b, s, _ = proj3.shape
    blk = MOBA_BLOCK
    nb = s // blk
    hd = MOBA_HD
    return pl.pallas_call(
        _moba_kernel,
        grid=(b, MOBA_HEADS, nb),
        in_specs=[
            pl.BlockSpec(memory_space=pltpu.SMEM),
            pl.BlockSpec((None, blk, hd), lambda bi, h, i: (bi, i, COL_MQ // hd + h)),
            pl.BlockSpec((None, s, hd), lambda bi, h, i: (bi, 0, COL_MK // hd + h)),
            pl.BlockSpec((None, s, hd), lambda bi, h, i: (bi, 0, COL_MV // hd + h)),
        ],
        out_specs=pl.BlockSpec((None, blk, hd), lambda bi, h, i: (bi, i, h)),
        out_shape=jax.ShapeDtypeStruct((b, s, MOBA_W), BF16),
        scratch_shapes=[
            pltpu.VMEM((hd, s), BF16),
            pltpu.VMEM((nb, hd), F32),
            pltpu.VMEM((nb, blk), F32),
        ],
        compiler_params=pltpu.CompilerParams(
            dimension_semantics=("parallel", "parallel", "arbitrary"),
            vmem_limit_bytes=_vmem_limit(32 << 20)),
        name="moba",
    )(slopes, proj3, proj3, proj3)


def _merge_kernel(x_ref, oa_ref, ob_ref, ga_ref, gb_ref, wa_ref, wb_ref, wo_ref, o_ref):
    ya = _dot(oa_ref[...], wa_ref[...])
    yb = _dot(ob_ref[...], wb_ref[...])
    mix = (jax.nn.sigmoid(ga_ref[...].astype(F32)) * ya
           + jax.nn.sigmoid(gb_ref[...].astype(F32)) * yb)
    o_ref[...] = x_ref[...] + _dot(mix.astype(BF16), wo_ref[...])


def _merge(x, oa, ob, proj, wa, wb, wo):
    t = x.shape[0]
    tm = MERGE_TM
    const = dict(pipeline_mode=pl.Buffered(1))
    vmem = (2 * 2 * tm * D_MODEL * 4 + 2 * 2 * tm * GLA_V_W * 2 + 2 * 2 * tm * D_MODEL * 2
            + (2 * GLA_V_W + D_MODEL) * D_MODEL * 2 + 4 * tm * D_MODEL * 4 + (4 << 20))
    return pl.pallas_call(
        _merge_kernel,
        grid=(t // tm,),
        in_specs=[
            pl.BlockSpec((tm, D_MODEL), lambda i: (i, 0)),
            pl.BlockSpec((tm, GLA_V_W), lambda i: (i, 0)),
            pl.BlockSpec((tm, MOBA_W), lambda i: (i, 0)),
            pl.BlockSpec((tm, D_MODEL), lambda i: (i, COL_GA // D_MODEL)),
            pl.BlockSpec((tm, D_MODEL), lambda i: (i, COL_GB // D_MODEL)),
            pl.BlockSpec((GLA_V_W, D_MODEL), lambda i: (0, 0), **const),
            pl.BlockSpec((MOBA_W, D_MODEL), lambda i: (0, 0), **const),
            pl.BlockSpec((D_MODEL, D_MODEL), lambda i: (0, 0), **const),
        ],
        out_specs=pl.BlockSpec((tm, D_MODEL), lambda i: (i, 0)),
        out_shape=jax.ShapeDtypeStruct((t, D_MODEL), F32),
        compiler_params=pltpu.CompilerParams(
            dimension_semantics=("parallel",),
            vmem_limit_bytes=_vmem_limit(vmem)),
        name="merge",
    )(x, oa, ob, proj, proj, wa, wb, wo)


def kernel(x, ffn1_norm, ffn1_w_gate, ffn1_w_up, ffn1_w_down, mix_norm, w_in, gla_w_alpha,
           gla_b_alpha, gla_out_norm, w_branch_gla, w_branch_moba, w_out, ffn2_norm,
           ffn2_w_gate, ffn2_w_up, ffn2_w_down, final_norm):
    bsz, seq, d = x.shape
    t = bsz * seq
    depth = ffn1_norm.shape[0]
    assert d == D_MODEL and seq % GLA_CHUNK == 0 and seq % MOBA_BLOCK == 0
    assert t % FFN_TM == 0 and t % PROJ_TM == 0 and t % MERGE_TM == 0

    slopes = jnp.exp2(-8.0 * jnp.arange(1, MOBA_HEADS + 1, dtype=F32) / MOBA_HEADS)
    col_scale = jnp.ones((1, PROJ_W), F32)
    col_scale = col_scale.at[:, COL_GQ:COL_GQ + GLA_QK_W].set(GLA_DK ** -0.5)
    col_scale = col_scale.at[:, COL_MQ:COL_MQ + MOBA_W].set(MOBA_HD ** -0.5)
    final_g = final_norm.reshape(1, D_MODEL)

    xf = x.reshape(t, D_MODEL)
    for l in range(depth):
        last = l == depth - 1
        w_l = w_in[l]
        w_main = jnp.concatenate([w_l[:, :LR_START], w_l[:, LR_START + GLA_RANK:]],
                                 axis=1).astype(BF16)
        w_lr = jnp.pad(w_l[:, LR_START:LR_START + GLA_RANK],
                       ((0, 0), (0, LANES_V7X - GLA_RANK))).astype(BF16)
        w_alpha = jnp.pad(gla_w_alpha[l], ((0, LANES_V7X - GLA_RANK), (0, 0)))

        xf = _ffn(xf, ffn1_norm[l].reshape(1, D_MODEL), ffn1_w_gate[l].astype(BF16),
                  ffn1_w_up[l].astype(BF16), ffn1_w_down[l].astype(BF16), final_g, False)
        proj, log_a = _in_proj(xf, mix_norm[l].reshape(1, D_MODEL), w_main, col_scale, w_lr,
                               w_alpha, gla_b_alpha[l].reshape(1, GLA_QK_W))
        proj3 = proj.reshape(bsz, seq, PROJ_W)
        oa = _gla(proj3, log_a.reshape(bsz, seq, GLA_QK_W), gla_out_norm[l].reshape(1, GLA_DV))
        ob = _moba(proj3, slopes)
        xf = _merge(xf, oa.reshape(t, GLA_V_W), ob.reshape(t, MOBA_W), proj,
                    w_branch_gla[l].astype(BF16), w_branch_moba[l].astype(BF16),
                    w_out[l].astype(BF16))
        xf = _ffn(xf, ffn2_norm[l].reshape(1, D_MODEL), ffn2_w_gate[l].astype(BF16),
                  ffn2_w_up[l].astype(BF16), ffn2_w_down[l].astype(BF16), final_g, last)
    if depth == 0:
        raise NotImplementedError("depth must be >= 1")
    return xf.reshape(bsz, seq, D_MODEL)
```

```python
import functools

import jax
import jax.numpy as jnp
from jax import lax
from jax.experimental import pallas as pl
from jax.experimental.pallas import tpu as pltpu

F32 = jnp.float32
BF16 = jnp.bfloat16

D_MODEL = 2048
D_FF = 5632
GLA_HEADS = 4
GLA_DK = 128
GLA_DV = 256
GLA_RANK = 16
GLA_TAU = 16.0
MOBA_HEADS = 8
MOBA_HD = 128
MOBA_BLOCK = 256
MOBA_TOPK = 3
NORM_EPS = 1e-6

GLA_QK_W = GLA_HEADS * GLA_DK
GLA_V_W = GLA_HEADS * GLA_DV
MOBA_W = MOBA_HEADS * MOBA_HD

COL_GQ = 0
COL_GK = COL_GQ + GLA_QK_W
COL_GV = COL_GK + GLA_QK_W
COL_GR = COL_GV + GLA_V_W
COL_MQ = COL_GR + GLA_V_W
COL_MK = COL_MQ + MOBA_W
COL_MV = COL_MK + MOBA_W
COL_GA = COL_MV + MOBA_W
COL_GB = COL_GA + D_MODEL
PROJ_W = COL_GB + D_MODEL
LR_START = GLA_QK_W * 2 + GLA_V_W * 2

LANES_V7X = 128
SUBLANES_V7X = 8
VMEM_BYTES_V7X = 64 << 20
VMEM_CAP = VMEM_BYTES_V7X - (6 << 20)

FFN_TM = 1024
FFN_TF = 512
PROJ_TM = 1024
PROJ_TN = 1024
GLA_CHUNK = 256
GLA_BASE = SUBLANES_V7X
MERGE_TM = 512
NORM_ROWS = 256

MASK_VALUE = -1e30
MOBA_GROUP = 2
LOG2E = 1.4426950408889634


def _vmem_limit(nbytes):
    return int(min(VMEM_CAP, nbytes))


def _dot(a, b):
    return jnp.dot(a, b, preferred_element_type=F32)


def _dot_nt(a, b, precision=None):
    return lax.dot_general(a, b, (((1,), (1,)), ((), ())),
                           preferred_element_type=F32, precision=precision)


def _dot_tn(a, b):
    return lax.dot_general(a, b, (((0,), (0,)), ((), ())), preferred_element_type=F32)


def _rms_norm_rows(x_ref, g_ref, out_ref, rows):
    def body(r, carry):
        sl = pl.ds(pl.multiple_of(r * NORM_ROWS, NORM_ROWS), NORM_ROWS)
        x = x_ref[sl, :].astype(F32)
        ms = jnp.mean(x * x, axis=-1, keepdims=True)
        out_ref[sl, :] = (x * lax.rsqrt(ms + NORM_EPS) * g_ref[...]).astype(out_ref.dtype)
        return carry
    lax.fori_loop(0, rows // NORM_ROWS, body, 0)


def _ffn_kernel(x_ref, g_ref, wg_ref, wu_ref, wd_ref, fg_ref, o_ref, xn_ref, *, final_norm):
    j = pl.program_id(1)

    @pl.when(j == 0)
    def _():
        _rms_norm_rows(x_ref, g_ref, xn_ref, FFN_TM)
        o_ref[...] = x_ref[...]

    xn = xn_ref[...]
    hg = _dot(xn, wg_ref[...])
    hu = _dot(xn, wu_ref[...])
    a = (0.5 * hg * jax.nn.sigmoid(hg)) * hu
    o_ref[...] += _dot(a.astype(BF16), wd_ref[...])

    if final_norm:
        @pl.when(j == pl.num_programs(1) - 1)
        def _():
            _rms_norm_rows(o_ref, fg_ref, o_ref, FFN_TM)


def _ffn(x, norm_g, wg, wu, wd, final_g, final_norm):
    t = x.shape[0]
    vmem = (2 * 2 * FFN_TM * D_MODEL * 4
            + FFN_TM * D_MODEL * 2
            + 2 * 3 * D_MODEL * FFN_TF * 2
            + 4 * FFN_TM * FFN_TF * 4
            + (4 << 20))
    return pl.pallas_call(
        functools.partial(_ffn_kernel, final_norm=final_norm),
        grid=(t // FFN_TM, D_FF // FFN_TF),
        in_specs=[
            pl.BlockSpec((FFN_TM, D_MODEL), lambda i, j: (i, 0)),
            pl.BlockSpec((1, D_MODEL), lambda i, j: (0, 0)),
            pl.BlockSpec((D_MODEL, FFN_TF), lambda i, j: (0, j)),
            pl.BlockSpec((D_MODEL, FFN_TF), lambda i, j: (0, j)),
            pl.BlockSpec((FFN_TF, D_MODEL), lambda i, j: (j, 0)),
            pl.BlockSpec((1, D_MODEL), lambda i, j: (0, 0)),
        ],
        out_specs=pl.BlockSpec((FFN_TM, D_MODEL), lambda i, j: (i, 0)),
        out_shape=jax.ShapeDtypeStruct((t, D_MODEL), F32),
        scratch_shapes=[pltpu.VMEM((FFN_TM, D_MODEL), BF16)],
        compiler_params=pltpu.CompilerParams(
            dimension_semantics=("parallel", "arbitrary"),
            vmem_limit_bytes=_vmem_limit(vmem)),
        name="ffn_final" if final_norm else "ffn",
    )(x, norm_g, wg, wu, wd, final_g)


def _log_sigmoid(z):
    return jnp.minimum(z, 0.0) - jnp.log1p(jnp.exp(-jnp.abs(z)))


def _in_proj_kernel(x_ref, g_ref, w_ref, cs_ref, wlr_ref, wa_ref, ba_ref,
                    proj_ref, la_ref, xn_ref):
    j = pl.program_id(1)

    @pl.when(j == 0)
    def _():
        _rms_norm_rows(x_ref, g_ref, xn_ref, PROJ_TM)
        lr = _dot(xn_ref[...], wlr_ref[...])
        z = jnp.dot(lr, wa_ref[...], preferred_element_type=F32,
                    precision=lax.Precision.HIGHEST) + ba_ref[...]
        la_ref[...] = _log_sigmoid(z) * (1.0 / GLA_TAU)

    proj_ref[...] = (_dot(xn_ref[...], w_ref[...]) * cs_ref[...]).astype(proj_ref.dtype)


def _in_proj(x, norm_g, w_main, col_scale, w_lr, w_alpha, b_alpha):
    t = x.shape[0]
    vmem = (2 * PROJ_TM * D_MODEL * 4 + PROJ_TM * D_MODEL * 2
            + 2 * D_MODEL * PROJ_TN * 2 + 2 * PROJ_TM * PROJ_TN * 2
            + 2 * PROJ_TM * GLA_QK_W * 4 + 2 * PROJ_TM * PROJ_TN * 4
            + 2 * D_MODEL * LANES_V7X * 2 + (4 << 20))
    return pl.pallas_call(
        _in_proj_kernel,
        grid=(t // PROJ_TM, PROJ_W // PROJ_TN),
        in_specs=[
            pl.BlockSpec((PROJ_TM, D_MODEL), lambda i, j: (i, 0)),
            pl.BlockSpec((1, D_MODEL), lambda i, j: (0, 0)),
            pl.BlockSpec((D_MODEL, PROJ_TN), lambda i, j: (0, j)),
            pl.BlockSpec((1, PROJ_TN), lambda i, j: (0, j)),
            pl.BlockSpec((D_MODEL, LANES_V7X), lambda i, j: (0, 0)),
            pl.BlockSpec((LANES_V7X, GLA_QK_W), lambda i, j: (0, 0)),
            pl.BlockSpec((1, GLA_QK_W), lambda i, j: (0, 0)),
        ],
        out_specs=[
            pl.BlockSpec((PROJ_TM, PROJ_TN), lambda i, j: (i, j)),
            pl.BlockSpec((PROJ_TM, GLA_QK_W), lambda i, j: (i, 0)),
        ],
        out_shape=[
            jax.ShapeDtypeStruct((t, PROJ_W), BF16),
            jax.ShapeDtypeStruct((t, GLA_QK_W), F32),
        ],
        scratch_shapes=[pltpu.VMEM((PROJ_TM, D_MODEL), BF16)],
        compiler_params=pltpu.CompilerParams(
            dimension_semantics=("parallel", "arbitrary"),
            vmem_limit_bytes=_vmem_limit(vmem)),
        name="in_proj",
    )(x, norm_g, w_main, col_scale, w_lr, w_alpha, b_alpha)


def _bcast_rows(v, s, c):
    return jnp.broadcast_to(v, (c // s, s, GLA_DK)).reshape(c, GLA_DK)


def _gla_head(q, k, v, la, gr, gn, st_ref):
    c = q.shape[0]
    row = lax.broadcasted_iota(jnp.int32, (c, 1), 0)
    ri = lax.broadcasted_iota(jnp.int32, (c, c), 0)
    ci = lax.broadcasted_iota(jnp.int32, (c, c), 1)
    rxc = ri ^ ci

    cs = la.reshape(c // GLA_BASE, GLA_BASE, GLA_DK)
    sub = lax.broadcasted_iota(jnp.int32, cs.shape, 1)
    sh = 1
    while sh < GLA_BASE:
        cs = cs + jnp.where(sub >= sh, pltpu.roll(cs, sh, axis=1), 0.0)
        sh *= 2
    cs = cs.reshape(c, GLA_DK)

    p = _dot_nt((q * jnp.exp(cs)).astype(BF16), (k * jnp.exp(-cs)).astype(BF16))
    attn = jnp.where(rxc < GLA_BASE, jnp.where(ci <= ri, p, 0.0), 0.0)

    s = GLA_BASE
    while s < c:
        ends = cs.reshape(c // s, s, GLA_DK)[:, s - 1:s, :]
        prev = jnp.concatenate([jnp.zeros_like(ends[:1]), ends[:-1]], axis=0)
        odd = ((row // s) & 1) == 1
        qe = jnp.where(odd, jnp.exp(cs), 0.0)
        ke = jnp.where(odd, 0.0, jnp.exp(_bcast_rows(ends, s, c) - cs))
        p = _dot_nt((q * qe).astype(BF16), (k * ke).astype(BF16))
        attn = attn + jnp.where(rxc < 2 * s, p, 0.0)
        cs = cs + jnp.where(odd, _bcast_rows(prev, s, c), 0.0)
        s *= 2

    b_last = cs[c - 1:c, :]
    st = st_ref[...]
    o = _dot_nt((q * jnp.exp(cs)).astype(BF16), st.astype(BF16))
    o = o + _dot(attn.astype(BF16), v)
    kd = (k * jnp.exp(b_last - cs)).astype(BF16)
    st_ref[...] = st * jnp.exp(b_last) + _dot_tn(v, kd)

    ms = jnp.mean(o * o, axis=-1, keepdims=True)
    y = o * lax.rsqrt(ms + NORM_EPS) * gn
    g = gr.astype(F32)
    return y * (g * jax.nn.sigmoid(g))


def _gla_kernel(q_ref, k_ref, v_ref, gr_ref, la_ref, gn_ref, o_ref, st_ref):
    @pl.when(pl.program_id(1) == 0)
    def _():
        st_ref[...] = jnp.zeros_like(st_ref)

    gn = gn_ref[...]
    for h in range(GLA_HEADS):
        ks = slice(h * GLA_DK, (h + 1) * GLA_DK)
        vs = slice(h * GLA_DV, (h + 1) * GLA_DV)
        out = _gla_head(q_ref[:, ks].astype(F32), k_ref[:, ks].astype(F32), v_ref[:, vs],
                        la_ref[:, ks], gr_ref[:, vs], gn, st_ref.at[h])
        o_ref[:, vs] = out.astype(o_ref.dtype)


def _gla(proj3, la3, gn):
    b, s, _ = proj3.shape
    c = GLA_CHUNK
    return pl.pallas_call(
        _gla_kernel,
        grid=(b, s // c),
        in_specs=[
            pl.BlockSpec((None, c, GLA_QK_W), lambda i, j: (i, j, COL_GQ // GLA_QK_W)),
            pl.BlockSpec((None, c, GLA_QK_W), lambda i, j: (i, j, COL_GK // GLA_QK_W)),
            pl.BlockSpec((None, c, GLA_V_W), lambda i, j: (i, j, COL_GV // GLA_V_W)),
            pl.BlockSpec((None, c, GLA_V_W), lambda i, j: (i, j, COL_GR // GLA_V_W)),
            pl.BlockSpec((None, c, GLA_QK_W), lambda i, j: (i, j, 0)),
            pl.BlockSpec((1, GLA_DV), lambda i, j: (0, 0)),
        ],
        out_specs=pl.BlockSpec((None, c, GLA_V_W), lambda i, j: (i, j, 0)),
        out_shape=jax.ShapeDtypeStruct((b, s, GLA_V_W), BF16),
        scratch_shapes=[pltpu.VMEM((GLA_HEADS, GLA_DV, GLA_DK), F32)],
        compiler_params=pltpu.CompilerParams(
            dimension_semantics=("parallel", "arbitrary"),
            vmem_limit_bytes=_vmem_limit(32 << 20)),
        name="gla",
    )(proj3, proj3, proj3, proj3, la3, gn)


def _moba_kernel(slope_ref, q_ref, k_ref, v_ref, o_ref,
                 vt_ref, kmean_ref, selb_ref, sa_ref, sb_ref):
    h = pl.program_id(1)
    blk = MOBA_BLOCK
    nb = k_ref.shape[0] // blk
    grp = MOBA_GROUP
    gk = grp * blk
    ngrp = nb // grp
    slope2 = slope_ref[h] * LOG2E

    def prep(j, carry):
        sl = pl.ds(pl.multiple_of(j * blk, blk), blk)
        vt_ref[:, sl] = v_ref[sl, :].T
        kmean_ref[pl.ds(j, 1), :] = jnp.mean(k_ref[sl, :].astype(F32), axis=0, keepdims=True)
        return carry
    lax.fori_loop(0, nb, prep, 0)

    def select(i, carry):
        qs = pl.ds(pl.multiple_of(i * blk, blk), blk)
        gate = _dot_nt(kmean_ref[...], q_ref[qs, :].astype(F32),
                       precision=lax.Precision.HIGHEST)
        bidx = lax.broadcasted_iota(jnp.int32, gate.shape, 0)
        gate = jnp.where(bidx < i, gate, -jnp.inf)
        selb = jnp.full(gate.shape, MASK_VALUE, F32)
        for _ in range(MOBA_TOPK):
            m = jnp.max(gate, axis=0, keepdims=True)
            first = jnp.min(jnp.where(gate == m, bidx, nb), axis=0, keepdims=True)
            pick = bidx == jnp.where(m > -jnp.inf, first, -1)
            selb = jnp.where(pick, 0.0, selb)
            gate = jnp.where(pick, -jnp.inf, gate)
        selb_ref[:, qs] = selb
        return carry
    lax.fori_loop(0, nb, select, 0)

    kpos = lax.broadcasted_iota(jnp.int32, (blk, blk), 0)
    qpos = lax.broadcasted_iota(jnp.int32, (blk, blk), 1)
    bias0 = kpos.astype(F32) * slope2

    def qtile(i, carry):
        qs = pl.ds(pl.multiple_of(i * blk, blk), blk)
        q = q_ref[qs, :]

        def group_scores(g, s_ref):
            gc = jnp.minimum(g, ngrp - 1)
            ks = pl.ds(pl.multiple_of(gc * gk, gk), gk)
            s = _dot_nt(k_ref[ks, :], q)
            smax = None
            for a in range(grp):
                j = gc * grp + a
                rb = selb_ref[pl.ds(j, 1), qs] + slope2 * ((j - i) * blk).astype(F32)
                sa = s[a * blk:(a + 1) * blk] + bias0 + rb
                s_ref[a * blk:(a + 1) * blk, :] = sa
                amax = jnp.max(sa, axis=0, keepdims=True)
                smax = amax if smax is None else jnp.maximum(smax, amax)
            return smax

        def absorb(g, s_ref, smax, m, l, acc):
            gc = jnp.minimum(g, ngrp - 1)
            ks = pl.ds(pl.multiple_of(gc * gk, gk), gk)
            m_new = jnp.maximum(m, smax)
            alpha = jnp.exp2(m - m_new)
            p = jnp.exp2(s_ref[...] - m_new)
            l = alpha * l + jnp.sum(p, axis=0, keepdims=True)
            acc = alpha * acc + _dot(vt_ref[:, ks], p.astype(BF16))
            return m_new, l, acc

        max_a = group_scores(0, sa_ref)

        s = _dot_nt(k_ref[qs, :], q) + bias0
        s = jnp.where(kpos <= qpos, s, MASK_VALUE)
        m = jnp.max(s, axis=0, keepdims=True)
        p = jnp.exp2(s - m)
        l = jnp.sum(p, axis=0, keepdims=True)
        acc = _dot(vt_ref[:, qs], p.astype(BF16))

        def body(t, carry):
            m, l, acc, max_a = carry
            max_b = group_scores(2 * t + 1, sb_ref)
            m, l, acc = absorb(2 * t, sa_ref, max_a, m, l, acc)
            max_a = group_scores(2 * t + 2, sa_ref)
            m, l, acc = absorb(2 * t + 1, sb_ref, max_b, m, l, acc)
            return m, l, acc, max_a

        trips = (i + 2 * grp - 1) // (2 * grp)
        m, l, acc, _ = lax.fori_loop(0, trips, body, (m, l, acc, max_a))
        o_ref[qs, :] = (acc * (1.0 / l)).T.astype(o_ref.dtype)
        return carry
    lax.fori_loop(0, nb, qtile, 0)


def _moba(proj3, slopes):
    b, s, _ = proj3.shape
    blk = MOBA_BLOCK
    nb = s // blk
    hd = MOBA_HD
    gk = MOBA_GROUP * blk
    vmem = (2 * 4 * s * hd * 2 + s * hd * 2 + nb * s * 4 + 2 * gk * blk * 4
            + 6 * gk * blk * 4 + (4 << 20))
    return pl.pallas_call(
        _moba_kernel,
        grid=(b, MOBA_HEADS),
        in_specs=[
            pl.BlockSpec(memory_space=pltpu.SMEM),
            pl.BlockSpec((None, s, hd), lambda bi, h: (bi, 0, COL_MQ // hd + h)),
            pl.BlockSpec((None, s, hd), lambda bi, h: (bi, 0, COL_MK // hd + h)),
            pl.BlockSpec((None, s, hd), lambda bi, h: (bi, 0, COL_MV // hd + h)),
        ],
        out_specs=pl.BlockSpec((None, s, hd), lambda bi, h: (bi, 0, h)),
        out_shape=jax.ShapeDtypeStruct((b, s, MOBA_W), BF16),
        scratch_shapes=[
            pltpu.VMEM((hd, s), BF16),
            pltpu.VMEM((nb, hd), F32),
            pltpu.VMEM((nb, s), F32),
            pltpu.VMEM((gk, blk), F32),
            pltpu.VMEM((gk, blk), F32),
        ],
        compiler_params=pltpu.CompilerParams(
            dimension_semantics=("parallel", "parallel"),
            vmem_limit_bytes=_vmem_limit(vmem)),
        name="moba",
    )(slopes, proj3, proj3, proj3)


def _merge_kernel(x_ref, oa_ref, ob_ref, ga_ref, gb_ref, wa_ref, wb_ref, wo_ref, o_ref):
    ya = _dot(oa_ref[...], wa_ref[...])
    yb = _dot(ob_ref[...], wb_ref[...])
    mix = (jax.nn.sigmoid(ga_ref[...].astype(F32)) * ya
           + jax.nn.sigmoid(gb_ref[...].astype(F32)) * yb)
    o_ref[...] = x_ref[...] + _dot(mix.astype(BF16), wo_ref[...])


def _merge(x, oa, ob, proj, wa, wb, wo):
    t = x.shape[0]
    tm = MERGE_TM
    const = dict(pipeline_mode=pl.Buffered(1))
    vmem = (2 * 2 * tm * D_MODEL * 4 + 2 * 2 * tm * GLA_V_W * 2 + 2 * 2 * tm * D_MODEL * 2
            + (2 * GLA_V_W + D_MODEL) * D_MODEL * 2 + 4 * tm * D_MODEL * 4 + (4 << 20))
    return pl.pallas_call(
        _merge_kernel,
        grid=(t // tm,),
        in_specs=[
            pl.BlockSpec((tm, D_MODEL), lambda i: (i, 0)),
            pl.BlockSpec((tm, GLA_V_W), lambda i: (i, 0)),
            pl.BlockSpec((tm, MOBA_W), lambda i: (i, 0)),
            pl.BlockSpec((tm, D_MODEL), lambda i: (i, COL_GA // D_MODEL)),
            pl.BlockSpec((tm, D_MODEL), lambda i: (i, COL_GB // D_MODEL)),
            pl.BlockSpec((GLA_V_W, D_MODEL), lambda i: (0, 0), **const),
            pl.BlockSpec((MOBA_W, D_MODEL), lambda i: (0, 0), **const),
            pl.BlockSpec((D_MODEL, D_MODEL), lambda i: (0, 0), **const),
        ],
        out_specs=pl.BlockSpec((tm, D_MODEL), lambda i: (i, 0)),
        out_shape=jax.ShapeDtypeStruct((t, D_MODEL), F32),
        compiler_params=pltpu.CompilerParams(
            dimension_semantics=("parallel",),
            vmem_limit_bytes=_vmem_limit(vmem)),
        name="merge",
    )(x, oa, ob, proj, proj, wa, wb, wo)


def kernel(x, ffn1_norm, ffn1_w_gate, ffn1_w_up, ffn1_w_down, mix_norm, w_in, gla_w_alpha,
           gla_b_alpha, gla_out_norm, w_branch_gla, w_branch_moba, w_out, ffn2_norm,
           ffn2_w_gate, ffn2_w_up, ffn2_w_down, final_norm):
    bsz, seq, d = x.shape
    t = bsz * seq
    depth = ffn1_norm.shape[0]
    assert d == D_MODEL and seq % GLA_CHUNK == 0 and seq % MOBA_BLOCK == 0
    assert t % FFN_TM == 0 and t % PROJ_TM == 0 and t % MERGE_TM == 0

    slopes = jnp.exp2(-8.0 * jnp.arange(1, MOBA_HEADS + 1, dtype=F32) / MOBA_HEADS)
    col_scale = jnp.ones((1, PROJ_W), F32)
    col_scale = col_scale.at[:, COL_GQ:COL_GQ + GLA_QK_W].set(GLA_DK ** -0.5)
    col_scale = col_scale.at[:, COL_MQ:COL_MQ + MOBA_W].set(MOBA_HD ** -0.5 * LOG2E)
    final_g = final_norm.reshape(1, D_MODEL)

    xf = x.reshape(t, D_MODEL)
    for l in range(depth):
        last = l == depth - 1
        w_l = w_in[l]
        w_main = jnp.concatenate([w_l[:, :LR_START], w_l[:, LR_START + GLA_RANK:]],
                                 axis=1).astype(BF16)
        w_lr = jnp.pad(w_l[:, LR_START:LR_START + GLA_RANK],
                       ((0, 0), (0, LANES_V7X - GLA_RANK))).astype(BF16)
        w_alpha = jnp.pad(gla_w_alpha[l], ((0, LANES_V7X - GLA_RANK), (0, 0)))

        xf = _ffn(xf, ffn1_norm[l].reshape(1, D_MODEL), ffn1_w_gate[l].astype(BF16),
                  ffn1_w_up[l].astype(BF16), ffn1_w_down[l].astype(BF16), final_g, False)
        proj, log_a = _in_proj(xf, mix_norm[l].reshape(1, D_MODEL), w_main, col_scale, w_lr,
                               w_alpha, gla_b_alpha[l].reshape(1, GLA_QK_W))
        proj3 = proj.reshape(bsz, seq, PROJ_W)
        oa = _gla(proj3, log_a.reshape(bsz, seq, GLA_QK_W), gla_out_norm[l].reshape(1, GLA_DV))
        ob = _moba(proj3, slopes)
        xf = _merge(xf, oa.reshape(t, GLA_V_W), ob.reshape(t, MOBA_W), proj,
                    w_branch_gla[l].astype(BF16), w_branch_moba[l].astype(BF16),
                    w_out[l].astype(BF16))
        xf = _ffn(xf, ffn2_norm[l].reshape(1, D_MODEL), ffn2_w_gate[l].astype(BF16),
                  ffn2_w_up[l].astype(BF16), ffn2_w_down[l].astype(BF16), final_g, last)
    if depth == 0:
        raise NotImplementedError("depth must be >= 1")
    return xf.reshape(bsz, seq, D_MODEL)
```

```python
import functools

import jax
import jax.numpy as jnp
import numpy as np
from jax import lax
from jax.experimental import pallas as pl
from jax.experimental.pallas import tpu as pltpu

F32 = jnp.float32
BF16 = jnp.bfloat16

D_MODEL = 2048
D_FF = 5632
GLA_HEADS = 4
GLA_DK = 128
GLA_DV = 256
GLA_RANK = 16
GLA_TAU = 16.0
MOBA_HEADS = 8
MOBA_HD = 128
MOBA_BLOCK = 256
MOBA_TOPK = 3
NORM_EPS = 1e-6

GLA_QK_W = GLA_HEADS * GLA_DK
GLA_V_W = GLA_HEADS * GLA_DV
MOBA_W = MOBA_HEADS * MOBA_HD

COL_GQ = 0
COL_GK = COL_GQ + GLA_QK_W
COL_GV = COL_GK + GLA_QK_W
COL_GR = COL_GV + GLA_V_W
COL_MQ = COL_GR + GLA_V_W
COL_MK = COL_MQ + MOBA_W
COL_MV = COL_MK + MOBA_W
COL_GA = COL_MV + MOBA_W
COL_GB = COL_GA + D_MODEL
PROJ_W = COL_GB + D_MODEL
LR_START = GLA_QK_W * 2 + GLA_V_W * 2

LANES_V7X = 128
SUBLANES_V7X = 8
VMEM_BYTES_V7X = 64 << 20
VMEM_CAP = VMEM_BYTES_V7X - (6 << 20)

FFN_TM = 1024
FFN_TF = 512
PROJ_TM = 1024
PROJ_TN = 1024
GLA_CHUNK = 256
GLA_BASE = SUBLANES_V7X
MERGE_TM = 512
NORM_ROWS = 256

MASK_VALUE = -1e30
MOBA_UNROLL = 8
MOBA_SEL_CHUNK = 1024
MOBA_FINISH_UNROLL = 4
LOG2E = 1.4426950408889634


def _vmem_limit(nbytes):
    return int(min(VMEM_CAP, nbytes))


def _dot(a, b):
    return jnp.dot(a, b, preferred_element_type=F32)


def _dot_nt(a, b, precision=None):
    return lax.dot_general(a, b, (((1,), (1,)), ((), ())),
                           preferred_element_type=F32, precision=precision)


def _dot_tn(a, b):
    return lax.dot_general(a, b, (((0,), (0,)), ((), ())), preferred_element_type=F32)


def _rms_norm_rows(x_ref, g_ref, out_ref, rows):
    def body(r, carry):
        sl = pl.ds(pl.multiple_of(r * NORM_ROWS, NORM_ROWS), NORM_ROWS)
        x = x_ref[sl, :].astype(F32)
        ms = jnp.mean(x * x, axis=-1, keepdims=True)
        out_ref[sl, :] = (x * lax.rsqrt(ms + NORM_EPS) * g_ref[...]).astype(out_ref.dtype)
        return carry
    lax.fori_loop(0, rows // NORM_ROWS, body, 0)


def _ffn_kernel(x_ref, g_ref, wg_ref, wu_ref, wd_ref, fg_ref, o_ref, xn_ref, *, final_norm):
    j = pl.program_id(1)

    @pl.when(j == 0)
    def _():
        _rms_norm_rows(x_ref, g_ref, xn_ref, FFN_TM)
        o_ref[...] = x_ref[...]

    xn = xn_ref[...]
    hg = _dot(xn, wg_ref[...])
    hu = _dot(xn, wu_ref[...])
    a = (0.5 * hg * jax.nn.sigmoid(hg)) * hu
    o_ref[...] += _dot(a.astype(BF16), wd_ref[...])

    if final_norm:
        @pl.when(j == pl.num_programs(1) - 1)
        def _():
            _rms_norm_rows(o_ref, fg_ref, o_ref, FFN_TM)


def _ffn(x, norm_g, wg, wu, wd, final_g, final_norm):
    t = x.shape[0]
    vmem = (2 * 2 * FFN_TM * D_MODEL * 4
            + FFN_TM * D_MODEL * 2
            + 2 * 3 * D_MODEL * FFN_TF * 2
            + 4 * FFN_TM * FFN_TF * 4
            + (4 << 20))
    return pl.pallas_call(
        functools.partial(_ffn_kernel, final_norm=final_norm),
        grid=(t // FFN_TM, D_FF // FFN_TF),
        in_specs=[
            pl.BlockSpec((FFN_TM, D_MODEL), lambda i, j: (i, 0)),
            pl.BlockSpec((1, D_MODEL), lambda i, j: (0, 0)),
            pl.BlockSpec((D_MODEL, FFN_TF), lambda i, j: (0, j)),
            pl.BlockSpec((D_MODEL, FFN_TF), lambda i, j: (0, j)),
            pl.BlockSpec((FFN_TF, D_MODEL), lambda i, j: (j, 0)),
            pl.BlockSpec((1, D_MODEL), lambda i, j: (0, 0)),
        ],
        out_specs=pl.BlockSpec((FFN_TM, D_MODEL), lambda i, j: (i, 0)),
        out_shape=jax.ShapeDtypeStruct((t, D_MODEL), F32),
        scratch_shapes=[pltpu.VMEM((FFN_TM, D_MODEL), BF16)],
        compiler_params=pltpu.CompilerParams(
            dimension_semantics=("parallel", "arbitrary"),
            vmem_limit_bytes=_vmem_limit(vmem)),
        name="ffn_final" if final_norm else "ffn",
    )(x, norm_g, wg, wu, wd, final_g)


def _log_sigmoid(z):
    return jnp.minimum(z, 0.0) - jnp.log1p(jnp.exp(-jnp.abs(z)))


def _in_proj_kernel(x_ref, g_ref, w_ref, cs_ref, wlr_ref, wa_ref, ba_ref,
                    proj_ref, la_ref, xn_ref):
    j = pl.program_id(1)

    @pl.when(j == 0)
    def _():
        _rms_norm_rows(x_ref, g_ref, xn_ref, PROJ_TM)
        lr = _dot(xn_ref[...], wlr_ref[...])
        z = jnp.dot(lr, wa_ref[...], preferred_element_type=F32,
                    precision=lax.Precision.HIGHEST) + ba_ref[...]
        la_ref[...] = _log_sigmoid(z) * (1.0 / GLA_TAU)

    proj_ref[...] = (_dot(xn_ref[...], w_ref[...]) * cs_ref[...]).astype(proj_ref.dtype)


def _in_proj(x, norm_g, w_main, col_scale, w_lr, w_alpha, b_alpha):
    t = x.shape[0]
    vmem = (2 * PROJ_TM * D_MODEL * 4 + PROJ_TM * D_MODEL * 2
            + 2 * D_MODEL * PROJ_TN * 2 + 2 * PROJ_TM * PROJ_TN * 2
            + 2 * PROJ_TM * GLA_QK_W * 4 + 2 * PROJ_TM * PROJ_TN * 4
            + 2 * D_MODEL * LANES_V7X * 2 + (4 << 20))
    return pl.pallas_call(
        _in_proj_kernel,
        grid=(t // PROJ_TM, PROJ_W // PROJ_TN),
        in_specs=[
            pl.BlockSpec((PROJ_TM, D_MODEL), lambda i, j: (i, 0)),
            pl.BlockSpec((1, D_MODEL), lambda i, j: (0, 0)),
            pl.BlockSpec((D_MODEL, PROJ_TN), lambda i, j: (0, j)),
            pl.BlockSpec((1, PROJ_TN), lambda i, j: (0, j)),
            pl.BlockSpec((D_MODEL, LANES_V7X), lambda i, j: (0, 0)),
            pl.BlockSpec((LANES_V7X, GLA_QK_W), lambda i, j: (0, 0)),
            pl.BlockSpec((1, GLA_QK_W), lambda i, j: (0, 0)),
        ],
        out_specs=[
            pl.BlockSpec((PROJ_TM, PROJ_TN), lambda i, j: (i, j)),
            pl.BlockSpec((PROJ_TM, GLA_QK_W), lambda i, j: (i, 0)),
        ],
        out_shape=[
            jax.ShapeDtypeStruct((t, PROJ_W), BF16),
            jax.ShapeDtypeStruct((t, GLA_QK_W), F32),
        ],
        scratch_shapes=[pltpu.VMEM((PROJ_TM, D_MODEL), BF16)],
        compiler_params=pltpu.CompilerParams(
            dimension_semantics=("parallel", "arbitrary"),
            vmem_limit_bytes=_vmem_limit(vmem)),
        name="in_proj",
    )(x, norm_g, w_main, col_scale, w_lr, w_alpha, b_alpha)


def _bcast_rows(v, s, c):
    return jnp.broadcast_to(v, (c // s, s, GLA_DK)).reshape(c, GLA_DK)


def _gla_head(q, k, v, la, gr, gn, st_ref):
    c = q.shape[0]
    row = lax.broadcasted_iota(jnp.int32, (c, 1), 0)
    ri = lax.broadcasted_iota(jnp.int32, (c, c), 0)
    ci = lax.broadcasted_iota(jnp.int32, (c, c), 1)
    rxc = ri ^ ci

    cs = la.reshape(c // GLA_BASE, GLA_BASE, GLA_DK)
    sub = lax.broadcasted_iota(jnp.int32, cs.shape, 1)
    sh = 1
    while sh < GLA_BASE:
        cs = cs + jnp.where(sub >= sh, pltpu.roll(cs, sh, axis=1), 0.0)
        sh *= 2
    cs = cs.reshape(c, GLA_DK)

    p = _dot_nt((q * jnp.exp(cs)).astype(BF16), (k * jnp.exp(-cs)).astype(BF16))
    attn = jnp.where(rxc < GLA_BASE, jnp.where(ci <= ri, p, 0.0), 0.0)

    s = GLA_BASE
    while s < c:
        ends = cs.reshape(c // s, s, GLA_DK)[:, s - 1:s, :]
        prev = jnp.concatenate([jnp.zeros_like(ends[:1]), ends[:-1]], axis=0)
        odd = ((row // s) & 1) == 1
        qe = jnp.where(odd, jnp.exp(cs), 0.0)
        ke = jnp.where(odd, 0.0, jnp.exp(_bcast_rows(ends, s, c) - cs))
        p = _dot_nt((q * qe).astype(BF16), (k * ke).astype(BF16))
        attn = attn + jnp.where(rxc < 2 * s, p, 0.0)
        cs = cs + jnp.where(odd, _bcast_rows(prev, s, c), 0.0)
        s *= 2

    b_last = cs[c - 1:c, :]
    st = st_ref[...]
    o = _dot_nt((q * jnp.exp(cs)).astype(BF16), st.astype(BF16))
    o = o + _dot(attn.astype(BF16), v)
    kd = (k * jnp.exp(b_last - cs)).astype(BF16)
    st_ref[...] = st * jnp.exp(b_last) + _dot_tn(v, kd)

    ms = jnp.mean(o * o, axis=-1, keepdims=True)
    y = o * lax.rsqrt(ms + NORM_EPS) * gn
    g = gr.astype(F32)
    return y * (g * jax.nn.sigmoid(g))


def _gla_kernel(q_ref, k_ref, v_ref, gr_ref, la_ref, gn_ref, o_ref, st_ref):
    @pl.when(pl.program_id(1) == 0)
    def _():
        st_ref[...] = jnp.zeros_like(st_ref)

    gn = gn_ref[...]
    for h in range(GLA_HEADS):
        ks = slice(h * GLA_DK, (h + 1) * GLA_DK)
        vs = slice(h * GLA_DV, (h + 1) * GLA_DV)
        out = _gla_head(q_ref[:, ks].astype(F32), k_ref[:, ks].astype(F32), v_ref[:, vs],
                        la_ref[:, ks], gr_ref[:, vs], gn, st_ref.at[h])
        o_ref[:, vs] = out.astype(o_ref.dtype)


def _gla(proj3, la3, gn):
    b, s, _ = proj3.shape
    c = GLA_CHUNK
    return pl.pallas_call(
        _gla_kernel,
        grid=(b, s // c),
        in_specs=[
            pl.BlockSpec((None, c, GLA_QK_W), lambda i, j: (i, j, COL_GQ // GLA_QK_W)),
            pl.BlockSpec((None, c, GLA_QK_W), lambda i, j: (i, j, COL_GK // GLA_QK_W)),
            pl.BlockSpec((None, c, GLA_V_W), lambda i, j: (i, j, COL_GV // GLA_V_W)),
            pl.BlockSpec((None, c, GLA_V_W), lambda i, j: (i, j, COL_GR // GLA_V_W)),
            pl.BlockSpec((None, c, GLA_QK_W), lambda i, j: (i, j, 0)),
            pl.BlockSpec((1, GLA_DV), lambda i, j: (0, 0)),
        ],
        out_specs=pl.BlockSpec((None, c, GLA_V_W), lambda i, j: (i, j, 0)),
        out_shape=jax.ShapeDtypeStruct((b, s, GLA_V_W), BF16),
        scratch_shapes=[pltpu.VMEM((GLA_HEADS, GLA_DV, GLA_DK), F32)],
        compiler_params=pltpu.CompilerParams(
            dimension_semantics=("parallel", "arbitrary"),
            vmem_limit_bytes=_vmem_limit(32 << 20)),
        name="gla",
    )(proj3, proj3, proj3, proj3, la3, gn)


def _split3_bf16(z):
    hi = z.astype(BF16).astype(F32)
    r = z - hi
    mid = r.astype(BF16).astype(F32)
    lo = (r - mid).astype(BF16).astype(F32)
    return hi, mid, lo


def _moba_kernel(slope_ref, item_q_ref, item_k_ref, q_ref, k_ref, v_ref, o_ref,
                 vt_ref, qt_ref, ka_ref, kmean_ref, kms_ref, m_ref, l_ref, acc_ref,
                 s_ref, smax_ref, p_ref, al_ref):
    h = pl.program_id(1)
    blk = MOBA_BLOCK
    hd = MOBA_HD
    seq = k_ref.shape[0]
    nb = seq // blk
    unroll = MOBA_UNROLL
    n_items = item_q_ref.shape[0]
    slope2 = slope_ref[h] * LOG2E

    lane = lax.broadcasted_iota(jnp.int32, (blk, hd), 1)
    rowi = lax.broadcasted_iota(jnp.int32, (blk, hd), 0)

    def prep(j, carry):
        sl = pl.ds(pl.multiple_of(j * blk, blk), blk)
        kj = k_ref[sl, :]
        vt_ref[:, sl] = v_ref[sl, :].T
        qt_ref[0:hd, sl] = q_ref[sl, :].T
        ka_ref[sl, 0:hd] = kj
        hi, mid, lo = _split3_bf16((rowi + j * blk).astype(F32) * slope2)
        extra = jnp.where(lane == j, 1.0,
                          jnp.where(lane == nb, hi,
                                    jnp.where(lane == nb + 1, mid,
                                              jnp.where(lane == nb + 2, lo, 0.0))))
        ka_ref[sl, hd:2 * hd] = extra.astype(BF16)
        kmean_ref[pl.ds(j, 1), :] = jnp.mean(kj.astype(F32), axis=0, keepdims=True)
        return carry
    lax.fori_loop(0, nb, prep, 0)
    qt_ref[hd + nb:2 * hd, :] = jnp.ones((hd - nb, seq), BF16)
    kms_ref[...] = jnp.concatenate(_split3_bf16(kmean_ref[...]), axis=0).astype(BF16)

    chunk = MOBA_SEL_CHUNK

    def select(c, carry):
        cs = pl.ds(pl.multiple_of(c * chunk, chunk), chunk)
        g3 = _dot(kms_ref[...], qt_ref[0:hd, cs])
        gate = g3[0:nb] + g3[nb:2 * nb] + g3[2 * nb:3 * nb]
        bidx = lax.broadcasted_iota(jnp.int32, gate.shape, 0)
        qblk = c * (chunk // blk) + lax.broadcasted_iota(jnp.int32, gate.shape, 1) // blk
        gate = jnp.where(bidx < qblk, gate, -jnp.inf)
        mask = jnp.full(gate.shape, MASK_VALUE, F32)
        for _ in range(MOBA_TOPK):
            m = jnp.max(gate, axis=0, keepdims=True)
            first = jnp.min(jnp.where(gate == m, bidx, nb), axis=0, keepdims=True)
            pick = bidx == jnp.where(m > -jnp.inf, first, -1)
            mask = jnp.where(pick, 0.0, mask)
            gate = jnp.where(pick, -jnp.inf, gate)
        mask = jnp.where(bidx == qblk, 0.0, mask)
        qt_ref[hd:hd + nb, cs] = mask.astype(BF16)
        return carry
    lax.fori_loop(0, seq // chunk, select, 0)

    m_ref[...] = jnp.full(m_ref.shape, MASK_VALUE, F32)
    l_ref[...] = jnp.zeros(l_ref.shape, F32)

    def clear(i, carry):
        acc_ref[i] = jnp.zeros((hd, blk), F32)
        return carry
    lax.fori_loop(0, nb, clear, 0)

    kpos = lax.broadcasted_iota(jnp.int32, (blk, blk), 0)
    qpos = lax.broadcasted_iota(jnp.int32, (blk, blk), 1)

    def item_slices(n):
        i = item_q_ref[n]
        j = item_k_ref[n]
        return (i, pl.ds(pl.multiple_of(i * blk, blk), blk),
                pl.ds(pl.multiple_of(j * blk, blk), blk))

    def scores(n0, own_block):
        for u in range(unroll):
            _, qs, ks = item_slices(n0 + u)
            s = _dot(ka_ref[ks, :], qt_ref[:, qs])
            if own_block:
                s = jnp.where(kpos <= qpos, s, MASK_VALUE)
            s_ref[u] = s
            smax_ref[u] = jnp.max(s, axis=0, keepdims=True)

    def softmax(n0):
        for u in range(unroll):
            i, _, _ = item_slices(n0 + u)
            row = pl.ds(i, 1)
            m_old = m_ref[row, :]
            m_new = jnp.maximum(m_old, smax_ref[u])
            alpha = jnp.exp2(m_old - m_new)
            p = jnp.exp2(s_ref[u] - m_new)
            l_ref[row, :] = alpha * l_ref[row, :] + jnp.sum(p, axis=0, keepdims=True)
            m_ref[row, :] = m_new
            p_ref[u] = p.astype(BF16)
            al_ref[u] = alpha

    def values(n0):
        for u in range(unroll):
            i, _, ks = item_slices(n0 + u)
            acc_ref[i] = al_ref[u] * acc_ref[i] + _dot(vt_ref[:, ks], p_ref[u])

    steps = n_items // unroll
    own_steps = nb // unroll

    def stage(t, own_block):
        values(t * unroll)
        softmax((t + 1) * unroll)
        scores((t + 2) * unroll, own_block)

    scores(0, True)
    softmax(0)
    scores(unroll, True)
    lax.fori_loop(0, own_steps - 2, lambda t, c: (stage(t, True), c)[1], 0)
    lax.fori_loop(own_steps - 2, steps - 2, lambda t, c: (stage(t, False), c)[1], 0)
    values((steps - 2) * unroll)
    softmax((steps - 1) * unroll)
    values((steps - 1) * unroll)

    def finish(i0, carry):
        for u in range(MOBA_FINISH_UNROLL):
            i = i0 * MOBA_FINISH_UNROLL + u
            qs = pl.ds(pl.multiple_of(i * blk, blk), blk)
            inv = 1.0 / l_ref[pl.ds(i, 1), :]
            o_ref[qs, :] = (acc_ref[i] * inv).T.astype(o_ref.dtype)
        return carry
    lax.fori_loop(0, nb // MOBA_FINISH_UNROLL, finish, 0)


def _moba(proj3, slopes):
    b, s, _ = proj3.shape
    blk = MOBA_BLOCK
    nb = s // blk
    hd = MOBA_HD
    assert nb + 3 <= hd and s % MOBA_SEL_CHUNK == 0
    pairs = [(i, i) for i in range(nb)] + [(i, j) for i in range(nb) for j in range(i)]
    assert len(pairs) % MOBA_UNROLL == 0 and nb % MOBA_UNROLL == 0
    assert nb // MOBA_UNROLL >= 2 and len(pairs) // MOBA_UNROLL >= nb // MOBA_UNROLL + 2
    assert nb % MOBA_FINISH_UNROLL == 0
    item_q = jnp.asarray(np.array([p[0] for p in pairs], np.int32))
    item_k = jnp.asarray(np.array([p[1] for p in pairs], np.int32))
    vmem = (2 * 4 * s * hd * 2
            + s * hd * 2 + 2 * 2 * s * hd * 2
            + nb * hd * blk * 4
            + MOBA_UNROLL * blk * blk * 6
            + 8 * blk * blk * 4 + (4 << 20))
    smem = pl.BlockSpec(memory_space=pltpu.SMEM)
    return pl.pallas_call(
        _moba_kernel,
        grid=(b, MOBA_HEADS),
        in_specs=[
            smem, smem, smem,
            pl.BlockSpec((None, s, hd), lambda bi, h: (bi, 0, COL_MQ // hd + h)),
            pl.BlockSpec((None, s, hd), lambda bi, h: (bi, 0, COL_MK // hd + h)),
            pl.BlockSpec((None, s, hd), lambda bi, h: (bi, 0, COL_MV // hd + h)),
        ],
        out_specs=pl.BlockSpec((None, s, hd), lambda bi, h: (bi, 0, h)),
        out_shape=jax.ShapeDtypeStruct((b, s, MOBA_W), BF16),
        scratch_shapes=[
            pltpu.VMEM((hd, s), BF16),
            pltpu.VMEM((2 * hd, s), BF16),
            pltpu.VMEM((s, 2 * hd), BF16),
            pltpu.VMEM((nb, hd), F32),
            pltpu.VMEM((3 * nb, hd), BF16),
            pltpu.VMEM((nb, blk), F32),
            pltpu.VMEM((nb, blk), F32),
            pltpu.VMEM((nb, hd, blk), F32),
            pltpu.VMEM((MOBA_UNROLL, blk, blk), F32),
            pltpu.VMEM((MOBA_UNROLL, 1, blk), F32),
            pltpu.VMEM((MOBA_UNROLL, blk, blk), BF16),
            pltpu.VMEM((MOBA_UNROLL, 1, blk), F32),
        ],
        compiler_params=pltpu.CompilerParams(
            dimension_semantics=("parallel", "parallel"),
            vmem_limit_bytes=_vmem_limit(vmem)),
        name="moba",
    )(slopes, item_q, item_k, proj3, proj3, proj3)


def _merge_kernel(x_ref, oa_ref, ob_ref, ga_ref, gb_ref, wa_ref, wb_ref, wo_ref, o_ref):
    ya = _dot(oa_ref[...], wa_ref[...])
    yb = _dot(ob_ref[...], wb_ref[...])
    mix = (jax.nn.sigmoid(ga_ref[...].astype(F32)) * ya
           + jax.nn.sigmoid(gb_ref[...].astype(F32)) * yb)
    o_ref[...] = x_ref[...] + _dot(mix.astype(BF16), wo_ref[...])


def _merge(x, oa, ob, proj, wa, wb, wo):
    t = x.shape[0]
    tm = MERGE_TM
    const = dict(pipeline_mode=pl.Buffered(1))
    vmem = (2 * 2 * tm * D_MODEL * 4 + 2 * 2 * tm * GLA_V_W * 2 + 2 * 2 * tm * D_MODEL * 2
            + (2 * GLA_V_W + D_MODEL) * D_MODEL * 2 + 4 * tm * D_MODEL * 4 + (4 << 20))
    return pl.pallas_call(
        _merge_kernel,
        grid=(t // tm,),
        in_specs=[
            pl.BlockSpec((tm, D_MODEL), lambda i: (i, 0)),
            pl.BlockSpec((tm, GLA_V_W), lambda i: (i, 0)),
            pl.BlockSpec((tm, MOBA_W), lambda i: (i, 0)),
            pl.BlockSpec((tm, D_MODEL), lambda i: (i, COL_GA // D_MODEL)),
            pl.BlockSpec((tm, D_MODEL), lambda i: (i, COL_GB // D_MODEL)),
            pl.BlockSpec((GLA_V_W, D_MODEL), lambda i: (0, 0), **const),
            pl.BlockSpec((MOBA_W, D_MODEL), lambda i: (0, 0), **const),
            pl.BlockSpec((D_MODEL, D_MODEL), lambda i: (0, 0), **const),
        ],
        out_specs=pl.BlockSpec((tm, D_MODEL), lambda i: (i, 0)),
        out_shape=jax.ShapeDtypeStruct((t, D_MODEL), F32),
        compiler_params=pltpu.CompilerParams(
            dimension_semantics=("parallel",),
            vmem_limit_bytes=_vmem_limit(vmem)),
        name="merge",
    )(x, oa, ob, proj, proj, wa, wb, wo)


def kernel(x, ffn1_norm, ffn1_w_gate, ffn1_w_up, ffn1_w_down, mix_norm, w_in, gla_w_alpha,
           gla_b_alpha, gla_out_norm, w_branch_gla, w_branch_moba, w_out, ffn2_norm,
           ffn2_w_gate, ffn2_w_up, ffn2_w_down, final_norm):
    bsz, seq, d = x.shape
    t = bsz * seq
    depth = ffn1_norm.shape[0]
    assert depth >= 1 and d == D_MODEL and seq % GLA_CHUNK == 0 and seq % MOBA_BLOCK == 0
    assert t % FFN_TM == 0 and t % PROJ_TM == 0 and t % MERGE_TM == 0

    slopes = jnp.exp2(-8.0 * jnp.arange(1, MOBA_HEADS + 1, dtype=F32) / MOBA_HEADS)
    col_scale = jnp.ones((1, PROJ_W), F32)
    col_scale = col_scale.at[:, COL_GQ:COL_GQ + GLA_QK_W].set(GLA_DK ** -0.5)
    col_scale = col_scale.at[:, COL_MQ:COL_MQ + MOBA_W].set(MOBA_HD ** -0.5 * LOG2E)
    final_g = final_norm.reshape(1, D_MODEL)

    xf = x.reshape(t, D_MODEL)
    for l in range(depth):
        last = l == depth - 1
        w_l = w_in[l]
        w_main = jnp.concatenate([w_l[:, :LR_START], w_l[:, LR_START + GLA_RANK:]],
                                 axis=1).astype(BF16)
        w_lr = jnp.pad(w_l[:, LR_START:LR_START + GLA_RANK],
                       ((0, 0), (0, LANES_V7X - GLA_RANK))).astype(BF16)
        w_alpha = jnp.pad(gla_w_alpha[l], ((0, LANES_V7X - GLA_RANK), (0, 0)))

        xf = _ffn(xf, ffn1_norm[l].reshape(1, D_MODEL), ffn1_w_gate[l].astype(BF16),
                  ffn1_w_up[l].astype(BF16), ffn1_w_down[l].astype(BF16), final_g, False)
        proj, log_a = _in_proj(xf, mix_norm[l].reshape(1, D_MODEL), w_main, col_scale, w_lr,
                               w_alpha, gla_b_alpha[l].reshape(1, GLA_QK_W))
        proj3 = proj.reshape(bsz, seq, PROJ_W)
        oa = _gla(proj3, log_a.reshape(bsz, seq, GLA_QK_W), gla_out_norm[l].reshape(1, GLA_DV))
        ob = _moba(proj3, slopes)
        xf = _merge(xf, oa.reshape(t, GLA_V_W), ob.reshape(t, MOBA_W), proj,
                    w_branch_gla[l].astype(BF16), w_branch_moba[l].astype(BF16),
                    w_out[l].astype(BF16))
        xf = _ffn(xf, ffn2_norm[l].reshape(1, D_MODEL), ffn2_w_gate[l].astype(BF16),
                  ffn2_w_up[l].astype(BF16), ffn2_w_down[l].astype(BF16), final_g, last)
    return xf.reshape(bsz, seq, D_MODEL)
```

```python
import functools

import jax
import jax.numpy as jnp
import numpy as np
from jax import lax
from jax.experimental import pallas as pl
from jax.experimental.pallas import tpu as pltpu

F32 = jnp.float32
BF16 = jnp.bfloat16

D_MODEL = 2048
D_FF = 5632
GLA_HEADS = 4
GLA_DK = 128
GLA_DV = 256
GLA_RANK = 16
GLA_TAU = 16.0
MOBA_HEADS = 8
MOBA_HD = 128
MOBA_BLOCK = 256
MOBA_TOPK = 3
NORM_EPS = 1e-6

GLA_QK_W = GLA_HEADS * GLA_DK
GLA_V_W = GLA_HEADS * GLA_DV
MOBA_W = MOBA_HEADS * MOBA_HD

HEAD_COLS = 3 * MOBA_W
COL_GA = 0
COL_GB = COL_GA + D_MODEL
COL_GQ = COL_GB + D_MODEL
COL_GK = COL_GQ + GLA_QK_W
COL_GV = COL_GK + GLA_QK_W
COL_GR = COL_GV + GLA_V_W
OUT_W = COL_GR + GLA_V_W
PROJ_W = HEAD_COLS + OUT_W
W_IN_GLA = (0, 2 * GLA_QK_W + 2 * GLA_V_W)
W_IN_LR = (W_IN_GLA[1], W_IN_GLA[1] + GLA_RANK)
W_IN_MOBA = (W_IN_LR[1], W_IN_LR[1] + 3 * MOBA_W)
W_IN_GATES = (W_IN_MOBA[1], W_IN_MOBA[1] + 2 * D_MODEL)

LANES_V7X = 128
SUBLANES_V7X = 8
VMEM_BYTES_V7X = 64 << 20
VMEM_CAP = VMEM_BYTES_V7X - (6 << 20)

FFN_TM = 1024
FFN_TF = 512
PROJ_TM = 1024
PROJ_TN = MOBA_W
HEAD_TILES = HEAD_COLS // PROJ_TN
LA_ROWS = 128
GLA_CHUNK = 256
GLA_BASE = SUBLANES_V7X
MERGE_TM = 512
NORM_ROWS = 256

MASK_VALUE = -1e30
MOBA_UNROLL = 8
MOBA_SEL_CHUNK = 1024
MOBA_FINISH_UNROLL = 4
LOG2E = 1.4426950408889634


def _vmem_limit(nbytes):
    return int(min(VMEM_CAP, nbytes))


def _dot(a, b):
    return jnp.dot(a, b, preferred_element_type=F32)


def _dot_nt(a, b, precision=None):
    return lax.dot_general(a, b, (((1,), (1,)), ((), ())),
                           preferred_element_type=F32, precision=precision)


def _dot_tn(a, b):
    return lax.dot_general(a, b, (((0,), (0,)), ((), ())), preferred_element_type=F32)


def _rms_norm_rows(x_ref, g_ref, out_ref, rows):
    def body(r, carry):
        sl = pl.ds(pl.multiple_of(r * NORM_ROWS, NORM_ROWS), NORM_ROWS)
        x = x_ref[sl, :].astype(F32)
        ms = jnp.mean(x * x, axis=-1, keepdims=True)
        out_ref[sl, :] = (x * lax.rsqrt(ms + NORM_EPS) * g_ref[...]).astype(out_ref.dtype)
        return carry
    lax.fori_loop(0, rows // NORM_ROWS, body, 0)


def _ffn_kernel(x_ref, g_ref, wg_ref, wu_ref, wd_ref, fg_ref, o_ref, xn_ref, *, final_norm):
    j = pl.program_id(1)

    @pl.when(j == 0)
    def _():
        _rms_norm_rows(x_ref, g_ref, xn_ref, FFN_TM)
        o_ref[...] = x_ref[...]

    xn = xn_ref[...]
    hg = _dot(xn, wg_ref[...])
    hu = _dot(xn, wu_ref[...])
    a = (0.5 * hg * jax.nn.sigmoid(hg)) * hu
    o_ref[...] += _dot(a.astype(BF16), wd_ref[...])

    if final_norm:
        @pl.when(j == pl.num_programs(1) - 1)
        def _():
            _rms_norm_rows(o_ref, fg_ref, o_ref, FFN_TM)


def _ffn(x, norm_g, wg, wu, wd, final_g, final_norm):
    t = x.shape[0]
    vmem = (2 * 2 * FFN_TM * D_MODEL * 4
            + FFN_TM * D_MODEL * 2
            + 2 * 3 * D_MODEL * FFN_TF * 2
            + 4 * FFN_TM * FFN_TF * 4
            + (4 << 20))
    return pl.pallas_call(
        functools.partial(_ffn_kernel, final_norm=final_norm),
        grid=(t // FFN_TM, D_FF // FFN_TF),
        in_specs=[
            pl.BlockSpec((FFN_TM, D_MODEL), lambda i, j: (i, 0)),
            pl.BlockSpec((1, D_MODEL), lambda i, j: (0, 0)),
            pl.BlockSpec((D_MODEL, FFN_TF), lambda i, j: (0, j)),
            pl.BlockSpec((D_MODEL, FFN_TF), lambda i, j: (0, j)),
            pl.BlockSpec((FFN_TF, D_MODEL), lambda i, j: (j, 0)),
            pl.BlockSpec((1, D_MODEL), lambda i, j: (0, 0)),
        ],
        out_specs=pl.BlockSpec((FFN_TM, D_MODEL), lambda i, j: (i, 0)),
        out_shape=jax.ShapeDtypeStruct((t, D_MODEL), F32),
        scratch_shapes=[pltpu.VMEM((FFN_TM, D_MODEL), BF16)],
        compiler_params=pltpu.CompilerParams(
            dimension_semantics=("parallel", "arbitrary"),
            vmem_limit_bytes=_vmem_limit(vmem)),
        name="ffn_final" if final_norm else "ffn",
    )(x, norm_g, wg, wu, wd, final_g)


def _log_sigmoid(z):
    return jnp.minimum(z, 0.0) - jnp.log1p(jnp.exp(-jnp.abs(z)))


def _in_proj_kernel(x_ref, g_ref, w_ref, cs_ref, wlr_ref, wa_ref, ba_ref,
                    proj_ref, heads_ref, la_ref, xn_ref, lr_ref):
    j = pl.program_id(1)

    @pl.when(j == 0)
    def _():
        _rms_norm_rows(x_ref, g_ref, xn_ref, PROJ_TM)
        lr_ref[...] = _dot(xn_ref[...], wlr_ref[...])

    rows = pl.ds(pl.multiple_of(jnp.minimum(j, PROJ_TM // LA_ROWS - 1) * LA_ROWS, LA_ROWS), LA_ROWS)
    z = jnp.dot(lr_ref[rows, :], wa_ref[...], preferred_element_type=F32,
                precision=lax.Precision.HIGHEST) + ba_ref[...]
    la_ref[rows, :] = _log_sigmoid(z) * (1.0 / GLA_TAU)

    res = (_dot(xn_ref[...], w_ref[...]) * cs_ref[...]).astype(BF16)
    proj_ref[...] = res
    for h in range(MOBA_HEADS):
        heads_ref[h] = res[:, h * MOBA_HD:(h + 1) * MOBA_HD]


def _in_proj(x, norm_g, w_main, col_scale, w_lr, w_alpha, b_alpha, bsz):
    t = x.shape[0]
    rpb = t // bsz // PROJ_TM
    vmem = (2 * PROJ_TM * D_MODEL * 4 + PROJ_TM * D_MODEL * 2
            + 2 * D_MODEL * PROJ_TN * 2 + 2 * 2 * PROJ_TM * PROJ_TN * 2
            + 2 * PROJ_TM * GLA_QK_W * 4 + 2 * PROJ_TM * PROJ_TN * 4
            + 2 * D_MODEL * LANES_V7X * 2 + PROJ_TM * LANES_V7X * 4 + (4 << 20))
    return pl.pallas_call(
        _in_proj_kernel,
        grid=(t // PROJ_TM, PROJ_W // PROJ_TN),
        in_specs=[
            pl.BlockSpec((PROJ_TM, D_MODEL), lambda i, j: (i, 0)),
            pl.BlockSpec((1, D_MODEL), lambda i, j: (0, 0)),
            pl.BlockSpec((D_MODEL, PROJ_TN), lambda i, j: (0, j)),
            pl.BlockSpec((1, PROJ_TN), lambda i, j: (0, j)),
            pl.BlockSpec((D_MODEL, LANES_V7X), lambda i, j: (0, 0)),
            pl.BlockSpec((LANES_V7X, GLA_QK_W), lambda i, j: (0, 0)),
            pl.BlockSpec((1, GLA_QK_W), lambda i, j: (0, 0)),
        ],
        out_specs=[
            pl.BlockSpec((PROJ_TM, PROJ_TN), lambda i, j: (i, jnp.maximum(j - HEAD_TILES, 0))),
            pl.BlockSpec((None, None, MOBA_HEADS, PROJ_TM, MOBA_HD),
                         lambda i, j: (i // rpb, jnp.minimum(j, HEAD_TILES), 0, i % rpb, 0)),
            pl.BlockSpec((PROJ_TM, GLA_QK_W), lambda i, j: (i, 0)),
        ],
        out_shape=[
            jax.ShapeDtypeStruct((t, OUT_W), BF16),
            jax.ShapeDtypeStruct((bsz, HEAD_TILES + 1, MOBA_HEADS, t // bsz, MOBA_HD), BF16),
            jax.ShapeDtypeStruct((t, GLA_QK_W), F32),
        ],
        scratch_shapes=[pltpu.VMEM((PROJ_TM, D_MODEL), BF16),
                        pltpu.VMEM((PROJ_TM, LANES_V7X), F32)],
        compiler_params=pltpu.CompilerParams(
            dimension_semantics=("parallel", "arbitrary"),
            vmem_limit_bytes=_vmem_limit(vmem)),
        name="in_proj",
    )(x, norm_g, w_main, col_scale, w_lr, w_alpha, b_alpha)


def _bcast_rows(v, s, c):
    return jnp.broadcast_to(v, (c // s, s, GLA_DK)).reshape(c, GLA_DK)


def _gla_head(q, k, v, la, gr, gn, st_ref):
    c = q.shape[0]
    row = lax.broadcasted_iota(jnp.int32, (c, 1), 0)
    ri = lax.broadcasted_iota(jnp.int32, (c, c), 0)
    ci = lax.broadcasted_iota(jnp.int32, (c, c), 1)
    rxc = ri ^ ci

    cs = la.reshape(c // GLA_BASE, GLA_BASE, GLA_DK)
    sub = lax.broadcasted_iota(jnp.int32, cs.shape, 1)
    sh = 1
    while sh < GLA_BASE:
        cs = cs + jnp.where(sub >= sh, pltpu.roll(cs, sh, axis=1), 0.0)
        sh *= 2
    cs = cs.reshape(c, GLA_DK)

    p = _dot_nt((q * jnp.exp(cs)).astype(BF16), (k * jnp.exp(-cs)).astype(BF16))
    attn = jnp.where(rxc < GLA_BASE, jnp.where(ci <= ri, p, 0.0), 0.0)

    s = GLA_BASE
    while s < c:
        ends = cs.reshape(c // s, s, GLA_DK)[:, s - 1:s, :]
        prev = jnp.concatenate([jnp.zeros_like(ends[:1]), ends[:-1]], axis=0)
        odd = ((row // s) & 1) == 1
        qe = jnp.where(odd, jnp.exp(cs), 0.0)
        ke = jnp.where(odd, 0.0, jnp.exp(_bcast_rows(ends, s, c) - cs))
        p = _dot_nt((q * qe).astype(BF16), (k * ke).astype(BF16))
        attn = attn + jnp.where(rxc < 2 * s, p, 0.0)
        cs = cs + jnp.where(odd, _bcast_rows(prev, s, c), 0.0)
        s *= 2

    b_last = cs[c - 1:c, :]
    st = st_ref[...]
    o = _dot_nt((q * jnp.exp(cs)).astype(BF16), st.astype(BF16))
    o = o + _dot(attn.astype(BF16), v)
    kd = (k * jnp.exp(b_last - cs)).astype(BF16)
    st_ref[...] = st * jnp.exp(b_last) + _dot_tn(v, kd)

    ms = jnp.mean(o * o, axis=-1, keepdims=True)
    y = o * lax.rsqrt(ms + NORM_EPS) * gn
    g = gr.astype(F32)
    return y * (g * jax.nn.sigmoid(g))


def _gla_kernel(q_ref, k_ref, v_ref, gr_ref, la_ref, gn_ref, o_ref, st_ref):
    @pl.when(pl.program_id(1) == 0)
    def _():
        st_ref[...] = jnp.zeros_like(st_ref)

    gn = gn_ref[...]
    for h in range(GLA_HEADS):
        ks = slice(h * GLA_DK, (h + 1) * GLA_DK)
        vs = slice(h * GLA_DV, (h + 1) * GLA_DV)
        out = _gla_head(q_ref[:, ks].astype(F32), k_ref[:, ks].astype(F32), v_ref[:, vs],
                        la_ref[:, ks], gr_ref[:, vs], gn, st_ref.at[h])
        o_ref[:, vs] = out.astype(o_ref.dtype)


def _gla(proj3, la3, gn):
    b, s, _ = proj3.shape
    c = GLA_CHUNK
    return pl.pallas_call(
        _gla_kernel,
        grid=(b, s // c),
        in_specs=[
            pl.BlockSpec((None, c, GLA_QK_W), lambda i, j: (i, j, COL_GQ // GLA_QK_W)),
            pl.BlockSpec((None, c, GLA_QK_W), lambda i, j: (i, j, COL_GK // GLA_QK_W)),
            pl.BlockSpec((None, c, GLA_V_W), lambda i, j: (i, j, COL_GV // GLA_V_W)),
            pl.BlockSpec((None, c, GLA_V_W), lambda i, j: (i, j, COL_GR // GLA_V_W)),
            pl.BlockSpec((None, c, GLA_QK_W), lambda i, j: (i, j, 0)),
            pl.BlockSpec((1, GLA_DV), lambda i, j: (0, 0)),
        ],
        out_specs=pl.BlockSpec((None, c, GLA_V_W), lambda i, j: (i, j, 0)),
        out_shape=jax.ShapeDtypeStruct((b, s, GLA_V_W), BF16),
        scratch_shapes=[pltpu.VMEM((GLA_HEADS, GLA_DV, GLA_DK), F32)],
        compiler_params=pltpu.CompilerParams(
            dimension_semantics=("parallel", "arbitrary"),
            vmem_limit_bytes=_vmem_limit(32 << 20)),
        name="gla",
    )(proj3, proj3, proj3, proj3, la3, gn)


def _split3_bf16(z):
    hi = z.astype(BF16).astype(F32)
    r = z - hi
    mid = r.astype(BF16).astype(F32)
    lo = (r - mid).astype(BF16).astype(F32)
    return hi, mid, lo


def _moba_kernel(slope_ref, item_q_ref, item_k_ref, q_ref, k_ref, v_ref, o_ref,
                 vt_ref, qt_ref, ka_ref, kmean_ref, kms_ref, m_ref, l_ref, acc_ref,
                 s_ref, smax_ref, p_ref, al_ref):
    h = pl.program_id(1)
    blk = MOBA_BLOCK
    hd = MOBA_HD
    seq = k_ref.shape[0]
    nb = seq // blk
    unroll = MOBA_UNROLL
    n_items = item_q_ref.shape[0]
    slope2 = slope_ref[h] * LOG2E

    lane = lax.broadcasted_iota(jnp.int32, (blk, hd), 1)
    rowi = lax.broadcasted_iota(jnp.int32, (blk, hd), 0)

    def prep(j, carry):
        sl = pl.ds(pl.multiple_of(j * blk, blk), blk)
        kj = k_ref[sl, :]
        vt_ref[:, sl] = v_ref[sl, :].T
        qt_ref[0:hd, sl] = q_ref[sl, :].T
        ka_ref[sl, 0:hd] = kj
        hi, mid, lo = _split3_bf16((rowi + j * blk).astype(F32) * slope2)
        extra = jnp.where(lane == j, 1.0,
                          jnp.where(lane == nb, hi,
                                    jnp.where(lane == nb + 1, mid,
                                              jnp.where(lane == nb + 2, lo, 0.0))))
        ka_ref[sl, hd:2 * hd] = extra.astype(BF16)
        kmean_ref[pl.ds(j, 1), :] = jnp.mean(kj.astype(F32), axis=0, keepdims=True)
        return carry
    lax.fori_loop(0, nb, prep, 0)
    qt_ref[hd + nb:2 * hd, :] = jnp.ones((hd - nb, seq), BF16)
    kms_ref[...] = jnp.concatenate(_split3_bf16(kmean_ref[...]), axis=0).astype(BF16)

    chunk = MOBA_SEL_CHUNK

    def select(c, carry):
        cs = pl.ds(pl.multiple_of(c * chunk, chunk), chunk)
        g3 = _dot(kms_ref[...], qt_ref[0:hd, cs])
        gate = g3[0:nb] + g3[nb:2 * nb] + g3[2 * nb:3 * nb]
        bidx = lax.broadcasted_iota(jnp.int32, gate.shape, 0)
        qblk = c * (chunk // blk) + lax.broadcasted_iota(jnp.int32, gate.shape, 1) // blk
        gate = jnp.where(bidx < qblk, gate, -jnp.inf)
        mask = jnp.full(gate.shape, MASK_VALUE, F32)
        for _ in range(MOBA_TOPK):
            m = jnp.max(gate, axis=0, keepdims=True)
            first = jnp.min(jnp.where(gate == m, bidx, nb), axis=0, keepdims=True)
            pick = bidx == jnp.where(m > -jnp.inf, first, -1)
            mask = jnp.where(pick, 0.0, mask)
            gate = jnp.where(pick, -jnp.inf, gate)
        mask = jnp.where(bidx == qblk, 0.0, mask)
        qt_ref[hd:hd + nb, cs] = mask.astype(BF16)
        return carry
    lax.fori_loop(0, seq // chunk, select, 0)

    m_ref[...] = jnp.full(m_ref.shape, MASK_VALUE, F32)
    l_ref[...] = jnp.zeros(l_ref.shape, F32)

    def clear(i, carry):
        acc_ref[i] = jnp.zeros((hd, blk), F32)
        return carry
    lax.fori_loop(0, nb, clear, 0)

    kpos = lax.broadcasted_iota(jnp.int32, (blk, blk), 0)
    qpos = lax.broadcasted_iota(jnp.int32, (blk, blk), 1)

    def item_slices(n):
        i = item_q_ref[n]
        j = item_k_ref[n]
        return (i, pl.ds(pl.multiple_of(i * blk, blk), blk),
                pl.ds(pl.multiple_of(j * blk, blk), blk))

    def scores(n0, own_block):
        for u in range(unroll):
            _, qs, ks = item_slices(n0 + u)
            s = _dot(ka_ref[ks, :], qt_ref[:, qs])
            if own_block:
                s = jnp.where(kpos <= qpos, s, MASK_VALUE)
            s_ref[u] = s
            smax_ref[u] = jnp.max(s, axis=0, keepdims=True)

    def softmax(n0):
        for u in range(unroll):
            i, _, _ = item_slices(n0 + u)
            row = pl.ds(i, 1)
            m_old = m_ref[row, :]
            m_new = jnp.maximum(m_old, smax_ref[u])
            alpha = jnp.exp2(m_old - m_new)
            p = jnp.exp2(s_ref[u] - m_new)
            l_ref[row, :] = alpha * l_ref[row, :] + jnp.sum(p, axis=0, keepdims=True)
            m_ref[row, :] = m_new
            p_ref[u] = p.astype(BF16)
            al_ref[u] = alpha

    def values(n0):
        for u in range(unroll):
            i, _, ks = item_slices(n0 + u)
            acc_ref[i] = al_ref[u] * acc_ref[i] + _dot(vt_ref[:, ks], p_ref[u])

    steps = n_items // unroll
    own_steps = nb // unroll

    def stage(t, own_block):
        values(t * unroll)
        softmax((t + 1) * unroll)
        scores((t + 2) * unroll, own_block)

    scores(0, True)
    softmax(0)
    scores(unroll, True)
    lax.fori_loop(0, own_steps - 2, lambda t, c: (stage(t, True), c)[1], 0)
    lax.fori_loop(own_steps - 2, steps - 2, lambda t, c: (stage(t, False), c)[1], 0)
    values((steps - 2) * unroll)
    softmax((steps - 1) * unroll)
    values((steps - 1) * unroll)

    def finish(i0, carry):
        for u in range(MOBA_FINISH_UNROLL):
            i = i0 * MOBA_FINISH_UNROLL + u
            qs = pl.ds(pl.multiple_of(i * blk, blk), blk)
            inv = 1.0 / l_ref[pl.ds(i, 1), :]
            o_ref[qs, :] = (acc_ref[i] * inv).T.astype(o_ref.dtype)
        return carry
    lax.fori_loop(0, nb // MOBA_FINISH_UNROLL, finish, 0)


def _moba(qkv, slopes):
    b, _, _, s, _ = qkv.shape
    blk = MOBA_BLOCK
    nb = s // blk
    hd = MOBA_HD
    assert nb + 3 <= hd and s % MOBA_SEL_CHUNK == 0
    pairs = [(i, i) for i in range(nb)] + [(i, j) for i in range(nb) for j in range(i)]
    assert len(pairs) % MOBA_UNROLL == 0 and nb % MOBA_UNROLL == 0
    assert nb // MOBA_UNROLL >= 2 and len(pairs) // MOBA_UNROLL >= nb // MOBA_UNROLL + 2
    assert nb % MOBA_FINISH_UNROLL == 0
    item_q = jnp.asarray(np.array([p[0] for p in pairs], np.int32))
    item_k = jnp.asarray(np.array([p[1] for p in pairs], np.int32))
    vmem = (2 * 4 * s * hd * 2
            + s * hd * 2 + 2 * 2 * s * hd * 2
            + nb * hd * blk * 4
            + MOBA_UNROLL * blk * blk * 6
            + 8 * blk * blk * 4 + (4 << 20))
    smem = pl.BlockSpec(memory_space=pltpu.SMEM)
    return pl.pallas_call(
        _moba_kernel,
        grid=(b, MOBA_HEADS),
        in_specs=[
            smem, smem, smem,
            pl.BlockSpec((None, None, None, s, hd), lambda bi, h: (bi, 0, h, 0, 0)),
            pl.BlockSpec((None, None, None, s, hd), lambda bi, h: (bi, 1, h, 0, 0)),
            pl.BlockSpec((None, None, None, s, hd), lambda bi, h: (bi, 2, h, 0, 0)),
        ],
        out_specs=pl.BlockSpec((None, None, s, hd), lambda bi, h: (bi, h, 0, 0)),
        out_shape=jax.ShapeDtypeStruct((b, MOBA_HEADS, s, hd), BF16),
        scratch_shapes=[
            pltpu.VMEM((hd, s), BF16),
            pltpu.VMEM((2 * hd, s), BF16),
            pltpu.VMEM((s, 2 * hd), BF16),
            pltpu.VMEM((nb, hd), F32),
            pltpu.VMEM((3 * nb, hd), BF16),
            pltpu.VMEM((nb, blk), F32),
            pltpu.VMEM((nb, blk), F32),
            pltpu.VMEM((nb, hd, blk), F32),
            pltpu.VMEM((MOBA_UNROLL, blk, blk), F32),
            pltpu.VMEM((MOBA_UNROLL, 1, blk), F32),
            pltpu.VMEM((MOBA_UNROLL, blk, blk), BF16),
            pltpu.VMEM((MOBA_UNROLL, 1, blk), F32),
        ],
        compiler_params=pltpu.CompilerParams(
            dimension_semantics=("parallel", "parallel"),
            vmem_limit_bytes=_vmem_limit(vmem)),
        name="moba",
    )(slopes, item_q, item_k, qkv, qkv, qkv)


def _merge_kernel(x_ref, oa_ref, ob_ref, ga_ref, gb_ref, wa_ref, wb_ref, wo_ref, o_ref):
    ya = _dot(oa_ref[...], wa_ref[...])
    ob = jnp.concatenate([ob_ref[h] for h in range(MOBA_HEADS)], axis=1)
    yb = _dot(ob, wb_ref[...])
    mix = (jax.nn.sigmoid(ga_ref[...].astype(F32)) * ya
           + jax.nn.sigmoid(gb_ref[...].astype(F32)) * yb)
    o_ref[...] = x_ref[...] + _dot(mix.astype(BF16), wo_ref[...])


def _merge(x, oa, ob, proj, wa, wb, wo):
    t = x.shape[0]
    tm = MERGE_TM
    rpb = ob.shape[2] // tm
    const = dict(pipeline_mode=pl.Buffered(1))
    vmem = (2 * 2 * tm * D_MODEL * 4 + 2 * 2 * tm * GLA_V_W * 2 + 2 * 2 * tm * D_MODEL * 2
            + (2 * GLA_V_W + D_MODEL) * D_MODEL * 2 + 4 * tm * D_MODEL * 4 + (4 << 20))
    return pl.pallas_call(
        _merge_kernel,
        grid=(t // tm,),
        in_specs=[
            pl.BlockSpec((tm, D_MODEL), lambda i: (i, 0)),
            pl.BlockSpec((tm, GLA_V_W), lambda i: (i, 0)),
            pl.BlockSpec((None, MOBA_HEADS, tm, MOBA_HD), lambda i: (i // rpb, 0, i % rpb, 0)),
            pl.BlockSpec((tm, D_MODEL), lambda i: (i, COL_GA // D_MODEL)),
            pl.BlockSpec((tm, D_MODEL), lambda i: (i, COL_GB // D_MODEL)),
            pl.BlockSpec((GLA_V_W, D_MODEL), lambda i: (0, 0), **const),
            pl.BlockSpec((MOBA_W, D_MODEL), lambda i: (0, 0), **const),
            pl.BlockSpec((D_MODEL, D_MODEL), lambda i: (0, 0), **const),
        ],
        out_specs=pl.BlockSpec((tm, D_MODEL), lambda i: (i, 0)),
        out_shape=jax.ShapeDtypeStruct((t, D_MODEL), F32),
        compiler_params=pltpu.CompilerParams(
            dimension_semantics=("parallel",),
            vmem_limit_bytes=_vmem_limit(vmem)),
        name="merge",
    )(x, oa, ob, proj, proj, wa, wb, wo)


def kernel(x, ffn1_norm, ffn1_w_gate, ffn1_w_up, ffn1_w_down, mix_norm, w_in, gla_w_alpha,
           gla_b_alpha, gla_out_norm, w_branch_gla, w_branch_moba, w_out, ffn2_norm,
           ffn2_w_gate, ffn2_w_up, ffn2_w_down, final_norm):
    bsz, seq, d = x.shape
    t = bsz * seq
    depth = ffn1_norm.shape[0]
    assert depth >= 1 and d == D_MODEL and seq % GLA_CHUNK == 0 and seq % MOBA_BLOCK == 0
    assert t % FFN_TM == 0 and seq % PROJ_TM == 0 and seq % MERGE_TM == 0
    assert OUT_W % PROJ_TN == 0
    assert PROJ_TM % LA_ROWS == 0 and PROJ_TM // LA_ROWS <= PROJ_W // PROJ_TN

    slopes = jnp.exp2(-8.0 * jnp.arange(1, MOBA_HEADS + 1, dtype=F32) / MOBA_HEADS)
    col_scale = jnp.ones((1, PROJ_W), F32)
    col_scale = col_scale.at[:, HEAD_COLS + COL_GQ:HEAD_COLS + COL_GQ + GLA_QK_W].set(GLA_DK ** -0.5)
    col_scale = col_scale.at[:, :MOBA_W].set(MOBA_HD ** -0.5 * LOG2E)
    final_g = final_norm.reshape(1, D_MODEL)

    xf = x.reshape(t, D_MODEL)
    for l in range(depth):
        last = l == depth - 1
        w_l = w_in[l]
        w_main = jnp.concatenate([w_l[:, slice(*W_IN_MOBA)], w_l[:, slice(*W_IN_GATES)],
                                  w_l[:, slice(*W_IN_GLA)]], axis=1).astype(BF16)
        w_lr = jnp.pad(w_l[:, slice(*W_IN_LR)],
                       ((0, 0), (0, LANES_V7X - GLA_RANK))).astype(BF16)
        w_alpha = jnp.pad(gla_w_alpha[l], ((0, LANES_V7X - GLA_RANK), (0, 0)))

        xf = _ffn(xf, ffn1_norm[l].reshape(1, D_MODEL), ffn1_w_gate[l].astype(BF16),
                  ffn1_w_up[l].astype(BF16), ffn1_w_down[l].astype(BF16), final_g, False)
        proj, qkv, log_a = _in_proj(xf, mix_norm[l].reshape(1, D_MODEL), w_main, col_scale,
                                    w_lr, w_alpha, gla_b_alpha[l].reshape(1, GLA_QK_W), bsz)
        proj3 = proj.reshape(bsz, seq, OUT_W)
        oa = _gla(proj3, log_a.reshape(bsz, seq, GLA_QK_W), gla_out_norm[l].reshape(1, GLA_DV))
        ob = _moba(qkv, slopes)
        xf = _merge(xf, oa.reshape(t, GLA_V_W), ob, proj,
                    w_branch_gla[l].astype(BF16), w_branch_moba[l].astype(BF16),
                    w_out[l].astype(BF16))
        xf = _ffn(xf, ffn2_norm[l].reshape(1, D_MODEL), ffn2_w_gate[l].astype(BF16),
                  ffn2_w_up[l].astype(BF16), ffn2_w_down[l].astype(BF16), final_g, last)
    return xf.reshape(bsz, seq, D_MODEL)
```

```python
import functools

import jax
import jax.numpy as jnp
import numpy as np
from jax import lax
from jax.experimental import pallas as pl
from jax.experimental.pallas import tpu as pltpu

F32 = jnp.float32
BF16 = jnp.bfloat16

D_MODEL = 2048
D_FF = 5632
GLA_HEADS = 4
GLA_DK = 128
GLA_DV = 256
GLA_RANK = 16
GLA_TAU = 16.0
MOBA_HEADS = 8
MOBA_HD = 128
MOBA_BLOCK = 256
MOBA_TOPK = 3
NORM_EPS = 1e-6

GLA_QK_W = GLA_HEADS * GLA_DK
GLA_V_W = GLA_HEADS * GLA_DV
MOBA_W = MOBA_HEADS * MOBA_HD

HEAD_COLS = 3 * MOBA_W
COL_GA = 0
COL_GB = COL_GA + D_MODEL
COL_GQ = COL_GB + D_MODEL
COL_GK = COL_GQ + GLA_QK_W
COL_GV = COL_GK + GLA_QK_W
COL_GR = COL_GV + GLA_V_W
OUT_W = COL_GR + GLA_V_W
PROJ_W = HEAD_COLS + OUT_W
W_IN_GLA = (0, 2 * GLA_QK_W + 2 * GLA_V_W)
W_IN_LR = (W_IN_GLA[1], W_IN_GLA[1] + GLA_RANK)
W_IN_MOBA = (W_IN_LR[1], W_IN_LR[1] + 3 * MOBA_W)
W_IN_GATES = (W_IN_MOBA[1], W_IN_MOBA[1] + 2 * D_MODEL)

LANES_V7X = 128
SUBLANES_V7X = 8
VMEM_BYTES_V7X = 64 << 20
VMEM_CAP = VMEM_BYTES_V7X - (6 << 20)

FFN_TM = 1024
FFN_TF = 512
PROJ_TM = 1024
PROJ_TN = MOBA_W
HEAD_TILES = HEAD_COLS // PROJ_TN
LA_ROWS = 128
GLA_CHUNK = 256
GLA_BASE = SUBLANES_V7X
MERGE_TM = 512
NORM_ROWS = 256

MASK_VALUE = -1e30
MOBA_STEP_BLOCKS = 8
MOBA_GROUP = 4
MOBA_SEL_CHUNK = 1024
MOBA_FINISH_UNROLL = 4
LOG2E = 1.4426950408889634


def _vmem_limit(nbytes):
    return int(min(VMEM_CAP, nbytes))


def _dot(a, b):
    return jnp.dot(a, b, preferred_element_type=F32)


def _dot_nt(a, b, precision=None):
    return lax.dot_general(a, b, (((1,), (1,)), ((), ())),
                           preferred_element_type=F32, precision=precision)


def _dot_tn(a, b):
    return lax.dot_general(a, b, (((0,), (0,)), ((), ())), preferred_element_type=F32)


def _rms_norm_rows(x_ref, g_ref, out_ref, rows):
    def body(r, carry):
        sl = pl.ds(pl.multiple_of(r * NORM_ROWS, NORM_ROWS), NORM_ROWS)
        x = x_ref[sl, :].astype(F32)
        ms = jnp.mean(x * x, axis=-1, keepdims=True)
        out_ref[sl, :] = (x * lax.rsqrt(ms + NORM_EPS) * g_ref[...]).astype(out_ref.dtype)
        return carry
    lax.fori_loop(0, rows // NORM_ROWS, body, 0)


def _ffn_kernel(x_ref, g_ref, wg_ref, wu_ref, wd_ref, fg_ref, o_ref, xn_ref, *, final_norm):
    j = pl.program_id(1)

    @pl.when(j == 0)
    def _():
        _rms_norm_rows(x_ref, g_ref, xn_ref, FFN_TM)
        o_ref[...] = x_ref[...]

    xn = xn_ref[...]
    hg = _dot(xn, wg_ref[...])
    hu = _dot(xn, wu_ref[...])
    a = (0.5 * hg * jax.nn.sigmoid(hg)) * hu
    o_ref[...] += _dot(a.astype(BF16), wd_ref[...])

    if final_norm:
        @pl.when(j == pl.num_programs(1) - 1)
        def _():
            _rms_norm_rows(o_ref, fg_ref, o_ref, FFN_TM)


def _ffn(x, norm_g, wg, wu, wd, final_g, final_norm):
    t = x.shape[0]
    vmem = (2 * 2 * FFN_TM * D_MODEL * 4
            + FFN_TM * D_MODEL * 2
            + 2 * 3 * D_MODEL * FFN_TF * 2
            + 4 * FFN_TM * FFN_TF * 4
            + (4 << 20))
    return pl.pallas_call(
        functools.partial(_ffn_kernel, final_norm=final_norm),
        grid=(t // FFN_TM, D_FF // FFN_TF),
        in_specs=[
            pl.BlockSpec((FFN_TM, D_MODEL), lambda i, j: (i, 0)),
            pl.BlockSpec((1, D_MODEL), lambda i, j: (0, 0)),
            pl.BlockSpec((D_MODEL, FFN_TF), lambda i, j: (0, j)),
            pl.BlockSpec((D_MODEL, FFN_TF), lambda i, j: (0, j)),
            pl.BlockSpec((FFN_TF, D_MODEL), lambda i, j: (j, 0)),
            pl.BlockSpec((1, D_MODEL), lambda i, j: (0, 0)),
        ],
        out_specs=pl.BlockSpec((FFN_TM, D_MODEL), lambda i, j: (i, 0)),
        out_shape=jax.ShapeDtypeStruct((t, D_MODEL), F32),
        scratch_shapes=[pltpu.VMEM((FFN_TM, D_MODEL), BF16)],
        compiler_params=pltpu.CompilerParams(
            dimension_semantics=("parallel", "arbitrary"),
            vmem_limit_bytes=_vmem_limit(vmem)),
        name="ffn_final" if final_norm else "ffn",
    )(x, norm_g, wg, wu, wd, final_g)


def _log_sigmoid(z):
    return jnp.minimum(z, 0.0) - jnp.log1p(jnp.exp(-jnp.abs(z)))


def _in_proj_kernel(x_ref, g_ref, w_ref, cs_ref, wlr_ref, wa_ref, ba_ref,
                    proj_ref, heads_ref, la_ref, xn_ref, lr_ref):
    j = pl.program_id(1)

    @pl.when(j == 0)
    def _():
        _rms_norm_rows(x_ref, g_ref, xn_ref, PROJ_TM)
        lr_ref[...] = _dot(xn_ref[...], wlr_ref[...])

    rows = pl.ds(pl.multiple_of(jnp.minimum(j, PROJ_TM // LA_ROWS - 1) * LA_ROWS, LA_ROWS), LA_ROWS)
    z = jnp.dot(lr_ref[rows, :], wa_ref[...], preferred_element_type=F32,
                precision=lax.Precision.HIGHEST) + ba_ref[...]
    la_ref[rows, :] = _log_sigmoid(z) * (1.0 / GLA_TAU)

    res = (_dot(xn_ref[...], w_ref[...]) * cs_ref[...]).astype(BF16)
    proj_ref[...] = res
    for h in range(MOBA_HEADS):
        heads_ref[h] = res[:, h * MOBA_HD:(h + 1) * MOBA_HD]


def _in_proj(x, norm_g, w_main, col_scale, w_lr, w_alpha, b_alpha, bsz):
    t = x.shape[0]
    rpb = t // bsz // PROJ_TM
    vmem = (2 * PROJ_TM * D_MODEL * 4 + PROJ_TM * D_MODEL * 2
            + 2 * D_MODEL * PROJ_TN * 2 + 2 * 2 * PROJ_TM * PROJ_TN * 2
            + 2 * PROJ_TM * GLA_QK_W * 4 + 2 * PROJ_TM * PROJ_TN * 4
            + 2 * D_MODEL * LANES_V7X * 2 + PROJ_TM * LANES_V7X * 4 + (4 << 20))
    return pl.pallas_call(
        _in_proj_kernel,
        grid=(t // PROJ_TM, PROJ_W // PROJ_TN),
        in_specs=[
            pl.BlockSpec((PROJ_TM, D_MODEL), lambda i, j: (i, 0)),
            pl.BlockSpec((1, D_MODEL), lambda i, j: (0, 0)),
            pl.BlockSpec((D_MODEL, PROJ_TN), lambda i, j: (0, j)),
            pl.BlockSpec((1, PROJ_TN), lambda i, j: (0, j)),
            pl.BlockSpec((D_MODEL, LANES_V7X), lambda i, j: (0, 0)),
            pl.BlockSpec((LANES_V7X, GLA_QK_W), lambda i, j: (0, 0)),
            pl.BlockSpec((1, GLA_QK_W), lambda i, j: (0, 0)),
        ],
        out_specs=[
            pl.BlockSpec((PROJ_TM, PROJ_TN), lambda i, j: (i, jnp.maximum(j - HEAD_TILES, 0))),
            pl.BlockSpec((None, None, MOBA_HEADS, PROJ_TM, MOBA_HD),
                         lambda i, j: (i // rpb, jnp.minimum(j, HEAD_TILES), 0, i % rpb, 0)),
            pl.BlockSpec((PROJ_TM, GLA_QK_W), lambda i, j: (i, 0)),
        ],
        out_shape=[
            jax.ShapeDtypeStruct((t, OUT_W), BF16),
            jax.ShapeDtypeStruct((bsz, HEAD_TILES + 1, MOBA_HEADS, t // bsz, MOBA_HD), BF16),
            jax.ShapeDtypeStruct((t, GLA_QK_W), F32),
        ],
        scratch_shapes=[pltpu.VMEM((PROJ_TM, D_MODEL), BF16),
                        pltpu.VMEM((PROJ_TM, LANES_V7X), F32)],
        compiler_params=pltpu.CompilerParams(
            dimension_semantics=("parallel", "arbitrary"),
            vmem_limit_bytes=_vmem_limit(vmem)),
        name="in_proj",
    )(x, norm_g, w_main, col_scale, w_lr, w_alpha, b_alpha)


def _bcast_rows(v, s, c):
    return jnp.broadcast_to(v, (c // s, s, GLA_DK)).reshape(c, GLA_DK)


def _gla_head(q, k, v, la, gr, gn, st_ref):
    c = q.shape[0]
    row = lax.broadcasted_iota(jnp.int32, (c, 1), 0)
    ri = lax.broadcasted_iota(jnp.int32, (c, c), 0)
    ci = lax.broadcasted_iota(jnp.int32, (c, c), 1)
    rxc = ri ^ ci

    cs = la.reshape(c // GLA_BASE, GLA_BASE, GLA_DK)
    sub = lax.broadcasted_iota(jnp.int32, cs.shape, 1)
    sh = 1
    while sh < GLA_BASE:
        cs = cs + jnp.where(sub >= sh, pltpu.roll(cs, sh, axis=1), 0.0)
        sh *= 2
    cs = cs.reshape(c, GLA_DK)

    p = _dot_nt((q * jnp.exp(cs)).astype(BF16), (k * jnp.exp(-cs)).astype(BF16))
    attn = jnp.where(rxc < GLA_BASE, jnp.where(ci <= ri, p, 0.0), 0.0)

    s = GLA_BASE
    while s < c:
        ends = cs.reshape(c // s, s, GLA_DK)[:, s - 1:s, :]
        prev = jnp.concatenate([jnp.zeros_like(ends[:1]), ends[:-1]], axis=0)
        odd = ((row // s) & 1) == 1
        qe = jnp.where(odd, jnp.exp(cs), 0.0)
        ke = jnp.where(odd, 0.0, jnp.exp(_bcast_rows(ends, s, c) - cs))
        p = _dot_nt((q * qe).astype(BF16), (k * ke).astype(BF16))
        attn = attn + jnp.where(rxc < 2 * s, p, 0.0)
        cs = cs + jnp.where(odd, _bcast_rows(prev, s, c), 0.0)
        s *= 2

    b_last = cs[c - 1:c, :]
    st = st_ref[...]
    o = _dot_nt((q * jnp.exp(cs)).astype(BF16), st.astype(BF16))
    o = o + _dot(attn.astype(BF16), v)
    kd = (k * jnp.exp(b_last - cs)).astype(BF16)
    st_ref[...] = st * jnp.exp(b_last) + _dot_tn(v, kd)

    ms = jnp.mean(o * o, axis=-1, keepdims=True)
    y = o * lax.rsqrt(ms + NORM_EPS) * gn
    g = gr.astype(F32)
    return y * (g * jax.nn.sigmoid(g))


def _gla_kernel(q_ref, k_ref, v_ref, gr_ref, la_ref, gn_ref, o_ref, st_ref):
    @pl.when(pl.program_id(1) == 0)
    def _():
        st_ref[...] = jnp.zeros_like(st_ref)

    gn = gn_ref[...]
    for h in range(GLA_HEADS):
        ks = slice(h * GLA_DK, (h + 1) * GLA_DK)
        vs = slice(h * GLA_DV, (h + 1) * GLA_DV)
        out = _gla_head(q_ref[:, ks].astype(F32), k_ref[:, ks].astype(F32), v_ref[:, vs],
                        la_ref[:, ks], gr_ref[:, vs], gn, st_ref.at[h])
        o_ref[:, vs] = out.astype(o_ref.dtype)


def _gla(proj3, la3, gn):
    b, s, _ = proj3.shape
    c = GLA_CHUNK
    return pl.pallas_call(
        _gla_kernel,
        grid=(b, s // c),
        in_specs=[
            pl.BlockSpec((None, c, GLA_QK_W), lambda i, j: (i, j, COL_GQ // GLA_QK_W)),
            pl.BlockSpec((None, c, GLA_QK_W), lambda i, j: (i, j, COL_GK // GLA_QK_W)),
            pl.BlockSpec((None, c, GLA_V_W), lambda i, j: (i, j, COL_GV // GLA_V_W)),
            pl.BlockSpec((None, c, GLA_V_W), lambda i, j: (i, j, COL_GR // GLA_V_W)),
            pl.BlockSpec((None, c, GLA_QK_W), lambda i, j: (i, j, 0)),
            pl.BlockSpec((1, GLA_DV), lambda i, j: (0, 0)),
        ],
        out_specs=pl.BlockSpec((None, c, GLA_V_W), lambda i, j: (i, j, 0)),
        out_shape=jax.ShapeDtypeStruct((b, s, GLA_V_W), BF16),
        scratch_shapes=[pltpu.VMEM((GLA_HEADS, GLA_DV, GLA_DK), F32)],
        compiler_params=pltpu.CompilerParams(
            dimension_semantics=("parallel", "arbitrary"),
            vmem_limit_bytes=_vmem_limit(32 << 20)),
        name="gla",
    )(proj3, proj3, proj3, proj3, la3, gn)


def _split3_bf16(z):
    hi = z.astype(BF16).astype(F32)
    r = z - hi
    mid = r.astype(BF16).astype(F32)
    lo = (r - mid).astype(BF16).astype(F32)
    return hi, mid, lo


def _moba_kernel(slope_ref, item_q_ref, item_k_ref, q_ref, k_ref, v_ref, o_ref,
                 vt_ref, qt_ref, ka_ref, kmean_ref, kms_ref, m_ref, l_ref, acc_ref,
                 s_ref, smax_ref, p_ref, al_ref):
    h = pl.program_id(1)
    blk = MOBA_BLOCK
    hd = MOBA_HD
    seq = k_ref.shape[0]
    nb = seq // blk
    slope2 = slope_ref[h] * LOG2E

    lane = lax.broadcasted_iota(jnp.int32, (blk, hd), 1)
    rowi = lax.broadcasted_iota(jnp.int32, (blk, hd), 0)

    def prep(j, carry):
        sl = pl.ds(pl.multiple_of(j * blk, blk), blk)
        kj = k_ref[sl, :]
        vt_ref[:, sl] = v_ref[sl, :].T
        qt_ref[0:hd, sl] = q_ref[sl, :].T
        ka_ref[sl, 0:hd] = kj
        hi, mid, lo = _split3_bf16((rowi + j * blk).astype(F32) * slope2)
        extra = jnp.where(lane == j, 1.0,
                          jnp.where(lane == nb, hi,
                                    jnp.where(lane == nb + 1, mid,
                                              jnp.where(lane == nb + 2, lo, 0.0))))
        ka_ref[sl, hd:2 * hd] = extra.astype(BF16)
        kmean_ref[pl.ds(j, 1), :] = jnp.mean(kj.astype(F32), axis=0, keepdims=True)
        return carry
    lax.fori_loop(0, nb, prep, 0)
    qt_ref[hd + nb:2 * hd, :] = jnp.ones((hd - nb, seq), BF16)
    kms_ref[...] = jnp.concatenate(_split3_bf16(kmean_ref[...]), axis=0).astype(BF16)

    chunk = MOBA_SEL_CHUNK

    def select(c, carry):
        cs = pl.ds(pl.multiple_of(c * chunk, chunk), chunk)
        g3 = _dot(kms_ref[...], qt_ref[0:hd, cs])
        gate = g3[0:nb] + g3[nb:2 * nb] + g3[2 * nb:3 * nb]
        bidx = lax.broadcasted_iota(jnp.int32, gate.shape, 0)
        qblk = c * (chunk // blk) + lax.broadcasted_iota(jnp.int32, gate.shape, 1) // blk
        gate = jnp.where(bidx < qblk, gate, -jnp.inf)
        mask = jnp.full(gate.shape, MASK_VALUE, F32)
        for _ in range(MOBA_TOPK):
            m = jnp.max(gate, axis=0, keepdims=True)
            first = jnp.min(jnp.where(gate == m, bidx, nb), axis=0, keepdims=True)
            pick = bidx == jnp.where(m > -jnp.inf, first, -1)
            mask = jnp.where(pick, 0.0, mask)
            gate = jnp.where(pick, -jnp.inf, gate)
        qt_ref[hd:hd + nb, cs] = mask.astype(BF16)
        return carry
    lax.fori_loop(0, seq // chunk, select, 0)

    m_ref[...] = jnp.full(m_ref.shape, MASK_VALUE, F32)
    l_ref[...] = jnp.zeros(l_ref.shape, F32)

    def clear(i, carry):
        acc_ref[i] = jnp.zeros((hd, blk), F32)
        return carry
    lax.fori_loop(0, nb, clear, 0)

    kpos = lax.broadcasted_iota(jnp.int32, (blk, blk), 0)
    qpos = lax.broadcasted_iota(jnp.int32, (blk, blk), 1)
    step_rows = s_ref.shape[0]

    def make_stages(own_block):
        rows = blk if own_block else MOBA_GROUP * blk
        units = step_rows // rows

        def item(n):
            if own_block:
                i = j = n
            else:
                i = item_q_ref[n]
                j = item_k_ref[n]
            return (i, pl.ds(pl.multiple_of(i * blk, blk), blk),
                    pl.ds(pl.multiple_of(j * blk, blk), rows))

        def scores(n0):
            for u in range(units):
                i, qs, ks = item(n0 + u)
                if own_block:
                    bias = (kpos + i * blk).astype(F32) * slope2
                    s = _dot(k_ref[ks, :], qt_ref[0:hd, qs]) + bias
                    s = jnp.where(kpos <= qpos, s, MASK_VALUE)
                else:
                    s = _dot(ka_ref[ks, :], qt_ref[:, qs])
                s_ref[u * rows:(u + 1) * rows, :] = s
                smax_ref[u] = jnp.max(s, axis=0, keepdims=True)

        def softmax(n0):
            for u in range(units):
                i, _, _ = item(n0 + u)
                row = pl.ds(i, 1)
                m_old = m_ref[row, :]
                m_new = jnp.maximum(m_old, smax_ref[u])
                alpha = jnp.exp2(m_old - m_new)
                p = jnp.exp2(s_ref[u * rows:(u + 1) * rows, :] - m_new)
                l_ref[row, :] = alpha * l_ref[row, :] + jnp.sum(p, axis=0, keepdims=True)
                m_ref[row, :] = m_new
                p_ref[u * rows:(u + 1) * rows, :] = p.astype(BF16)
                al_ref[u] = alpha

        def values(n0):
            for u in range(units):
                i, _, ks = item(n0 + u)
                acc_ref[i] = (al_ref[u] * acc_ref[i]
                              + _dot(vt_ref[:, ks], p_ref[u * rows:(u + 1) * rows, :]))

        return units, scores, softmax, values

    def run_pipeline(n_items, stages):
        units, scores, softmax, values = stages
        steps = n_items // units
        scores(0)
        softmax(0)
        scores(units)

        def body(t, carry):
            values(t * units)
            softmax((t + 1) * units)
            scores((t + 2) * units)
            return carry
        lax.fori_loop(0, steps - 2, body, 0)
        values((steps - 2) * units)
        softmax((steps - 1) * units)
        values((steps - 1) * units)

    run_pipeline(nb, make_stages(True))
    run_pipeline(item_q_ref.shape[0], make_stages(False))

    def finish(i0, carry):
        for u in range(MOBA_FINISH_UNROLL):
            i = i0 * MOBA_FINISH_UNROLL + u
            qs = pl.ds(pl.multiple_of(i * blk, blk), blk)
            inv = 1.0 / l_ref[pl.ds(i, 1), :]
            o_ref[qs, :] = (acc_ref[i] * inv).T.astype(o_ref.dtype)
        return carry
    lax.fori_loop(0, nb // MOBA_FINISH_UNROLL, finish, 0)


def _moba(qkv, slopes):
    b, _, _, s, _ = qkv.shape
    blk = MOBA_BLOCK
    nb = s // blk
    hd = MOBA_HD
    assert nb + 3 <= hd and s % MOBA_SEL_CHUNK == 0
    groups = [(i, j) for i in range(nb) for j in range(0, i, MOBA_GROUP)]
    own_units = MOBA_STEP_BLOCKS
    grp_units = MOBA_STEP_BLOCKS // MOBA_GROUP
    assert MOBA_STEP_BLOCKS % MOBA_GROUP == 0 and nb % MOBA_GROUP == 0
    assert nb % own_units == 0 and nb // own_units >= 2
    assert len(groups) % grp_units == 0 and len(groups) // grp_units >= 2
    assert nb % MOBA_FINISH_UNROLL == 0
    item_q = jnp.asarray(np.array([g[0] for g in groups], np.int32))
    item_k = jnp.asarray(np.array([g[1] for g in groups], np.int32))
    step_rows = MOBA_STEP_BLOCKS * blk
    vmem = (2 * 4 * s * hd * 2
            + s * hd * 2 + 2 * 2 * s * hd * 2
            + nb * hd * blk * 4
            + step_rows * blk * 6
            + 2 * step_rows * blk * 4 + (4 << 20))
    smem = pl.BlockSpec(memory_space=pltpu.SMEM)
    return pl.pallas_call(
        _moba_kernel,
        grid=(b, MOBA_HEADS),
        in_specs=[
            smem, smem, smem,
            pl.BlockSpec((None, None, None, s, hd), lambda bi, h: (bi, 0, h, 0, 0)),
            pl.BlockSpec((None, None, None, s, hd), lambda bi, h: (bi, 1, h, 0, 0)),
            pl.BlockSpec((None, None, None, s, hd), lambda bi, h: (bi, 2, h, 0, 0)),
        ],
        out_specs=pl.BlockSpec((None, None, s, hd), lambda bi, h: (bi, h, 0, 0)),
        out_shape=jax.ShapeDtypeStruct((b, MOBA_HEADS, s, hd), BF16),
        scratch_shapes=[
            pltpu.VMEM((hd, s), BF16),
            pltpu.VMEM((2 * hd, s), BF16),
            pltpu.VMEM((s, 2 * hd), BF16),
            pltpu.VMEM((nb, hd), F32),
            pltpu.VMEM((3 * nb, hd), BF16),
            pltpu.VMEM((nb, blk), F32),
            pltpu.VMEM((nb, blk), F32),
            pltpu.VMEM((nb, hd, blk), F32),
            pltpu.VMEM((step_rows, blk), F32),
            pltpu.VMEM((own_units, 1, blk), F32),
            pltpu.VMEM((step_rows, blk), BF16),
            pltpu.VMEM((own_units, 1, blk), F32),
        ],
        compiler_params=pltpu.CompilerParams(
            dimension_semantics=("parallel", "parallel"),
            vmem_limit_bytes=_vmem_limit(vmem)),
        name="moba",
    )(slopes, item_q, item_k, qkv, qkv, qkv)


def _merge_kernel(x_ref, oa_ref, ob_ref, ga_ref, gb_ref, wa_ref, wb_ref, wo_ref, o_ref):
    ya = _dot(oa_ref[...], wa_ref[...])
    ob = jnp.concatenate([ob_ref[h] for h in range(MOBA_HEADS)], axis=1)
    yb = _dot(ob, wb_ref[...])
    mix = (jax.nn.sigmoid(ga_ref[...].astype(F32)) * ya
           + jax.nn.sigmoid(gb_ref[...].astype(F32)) * yb)
    o_ref[...] = x_ref[...] + _dot(mix.astype(BF16), wo_ref[...])


def _merge(x, oa, ob, proj, wa, wb, wo):
    t = x.shape[0]
    tm = MERGE_TM
    rpb = ob.shape[2] // tm
    const = dict(pipeline_mode=pl.Buffered(1))
    vmem = (2 * 2 * tm * D_MODEL * 4 + 2 * 2 * tm * GLA_V_W * 2 + 2 * 2 * tm * D_MODEL * 2
            + (2 * GLA_V_W + D_MODEL) * D_MODEL * 2 + 4 * tm * D_MODEL * 4 + (4 << 20))
    return pl.pallas_call(
        _merge_kernel,
        grid=(t // tm,),
        in_specs=[
            pl.BlockSpec((tm, D_MODEL), lambda i: (i, 0)),
            pl.BlockSpec((tm, GLA_V_W), lambda i: (i, 0)),
            pl.BlockSpec((None, MOBA_HEADS, tm, MOBA_HD), lambda i: (i // rpb, 0, i % rpb, 0)),
            pl.BlockSpec((tm, D_MODEL), lambda i: (i, COL_GA // D_MODEL)),
            pl.BlockSpec((tm, D_MODEL), lambda i: (i, COL_GB // D_MODEL)),
            pl.BlockSpec((GLA_V_W, D_MODEL), lambda i: (0, 0), **const),
            pl.BlockSpec((MOBA_W, D_MODEL), lambda i: (0, 0), **const),
            pl.BlockSpec((D_MODEL, D_MODEL), lambda i: (0, 0), **const),
        ],
        out_specs=pl.BlockSpec((tm, D_MODEL), lambda i: (i, 0)),
        out_shape=jax.ShapeDtypeStruct((t, D_MODEL), F32),
        compiler_params=pltpu.CompilerParams(
            dimension_semantics=("parallel",),
            vmem_limit_bytes=_vmem_limit(vmem)),
        name="merge",
    )(x, oa, ob, proj, proj, wa, wb, wo)


def kernel(x, ffn1_norm, ffn1_w_gate, ffn1_w_up, ffn1_w_down, mix_norm, w_in, gla_w_alpha,
           gla_b_alpha, gla_out_norm, w_branch_gla, w_branch_moba, w_out, ffn2_norm,
           ffn2_w_gate, ffn2_w_up, ffn2_w_down, final_norm):
    bsz, seq, d = x.shape
    t = bsz * seq
    depth = ffn1_norm.shape[0]
    assert depth >= 1 and d == D_MODEL and seq % GLA_CHUNK == 0 and seq % MOBA_BLOCK == 0
    assert t % FFN_TM == 0 and seq % PROJ_TM == 0 and seq % MERGE_TM == 0
    assert OUT_W % PROJ_TN == 0
    assert PROJ_TM % LA_ROWS == 0 and PROJ_TM // LA_ROWS <= PROJ_W // PROJ_TN

    slopes = jnp.exp2(-8.0 * jnp.arange(1, MOBA_HEADS + 1, dtype=F32) / MOBA_HEADS)
    col_scale = jnp.ones((1, PROJ_W), F32)
    col_scale = col_scale.at[:, HEAD_COLS + COL_GQ:HEAD_COLS + COL_GQ + GLA_QK_W].set(GLA_DK ** -0.5)
    col_scale = col_scale.at[:, :MOBA_W].set(MOBA_HD ** -0.5 * LOG2E)
    final_g = final_norm.reshape(1, D_MODEL)

    xf = x.reshape(t, D_MODEL)
    for l in range(depth):
        last = l == depth - 1
        w_l = w_in[l]
        w_main = jnp.concatenate([w_l[:, slice(*W_IN_MOBA)], w_l[:, slice(*W_IN_GATES)],
                                  w_l[:, slice(*W_IN_GLA)]], axis=1).astype(BF16)
        w_lr = jnp.pad(w_l[:, slice(*W_IN_LR)],
                       ((0, 0), (0, LANES_V7X - GLA_RANK))).astype(BF16)
        w_alpha = jnp.pad(gla_w_alpha[l], ((0, LANES_V7X - GLA_RANK), (0, 0)))

        xf = _ffn(xf, ffn1_norm[l].reshape(1, D_MODEL), ffn1_w_gate[l].astype(BF16),
                  ffn1_w_up[l].astype(BF16), ffn1_w_down[l].astype(BF16), final_g, False)
        proj, qkv, log_a = _in_proj(xf, mix_norm[l].reshape(1, D_MODEL), w_main, col_scale,
                                    w_lr, w_alpha, gla_b_alpha[l].reshape(1, GLA_QK_W), bsz)
        proj3 = proj.reshape(bsz, seq, OUT_W)
        oa = _gla(proj3, log_a.reshape(bsz, seq, GLA_QK_W), gla_out_norm[l].reshape(1, GLA_DV))
        ob = _moba(qkv, slopes)
        xf = _merge(xf, oa.reshape(t, GLA_V_W), ob, proj,
                    w_branch_gla[l].astype(BF16), w_branch_moba[l].astype(BF16),
                    w_out[l].astype(BF16))
        xf = _ffn(xf, ffn2_norm[l].reshape(1, D_MODEL), ffn2_w_gate[l].astype(BF16),
                  ffn2_w_up[l].astype(BF16), ffn2_w_down[l].astype(BF16), final_g, last)
    return xf.reshape(bsz, seq, D_MODEL)
```

```python
import functools

import jax
import jax.numpy as jnp
import numpy as np
from jax import lax
from jax.experimental import pallas as pl
from jax.experimental.pallas import tpu as pltpu

F32 = jnp.float32
BF16 = jnp.bfloat16

D_MODEL = 2048
D_FF = 5632
GLA_HEADS = 4
GLA_DK = 128
GLA_DV = 256
GLA_RANK = 16
GLA_TAU = 16.0
MOBA_HEADS = 8
MOBA_HD = 128
MOBA_BLOCK = 256
MOBA_TOPK = 3
NORM_EPS = 1e-6

GLA_QK_W = GLA_HEADS * GLA_DK
GLA_V_W = GLA_HEADS * GLA_DV
MOBA_W = MOBA_HEADS * MOBA_HD

HEAD_COLS = 3 * MOBA_W
COL_GA = 0
COL_GB = COL_GA + D_MODEL
COL_GQ = COL_GB + D_MODEL
COL_GK = COL_GQ + GLA_QK_W
COL_GV = COL_GK + GLA_QK_W
COL_GR = COL_GV + GLA_V_W
OUT_W = COL_GR + GLA_V_W
PROJ_W = HEAD_COLS + OUT_W
W_IN_GLA = (0, 2 * GLA_QK_W + 2 * GLA_V_W)
W_IN_LR = (W_IN_GLA[1], W_IN_GLA[1] + GLA_RANK)
W_IN_MOBA = (W_IN_LR[1], W_IN_LR[1] + 3 * MOBA_W)
W_IN_GATES = (W_IN_MOBA[1], W_IN_MOBA[1] + 2 * D_MODEL)

LANES_V7X = 128
SUBLANES_V7X = 8
VMEM_BYTES_V7X = 64 << 20
VMEM_CAP = VMEM_BYTES_V7X - (6 << 20)

FFN_TM = 1024
FFN_TF = 512
PROJ_TM = 1024
PROJ_TN = MOBA_W
HEAD_TILES = HEAD_COLS // PROJ_TN
LA_ROWS = 128
GLA_CHUNK = 256
GLA_BASE = SUBLANES_V7X
MERGE_TM = 512
NORM_ROWS = 256
AHEAD_ROWS = 128

MASK_VALUE = -1e30
MOBA_STEP_BLOCKS = 8
MOBA_GROUP = 4
MOBA_SEL_CHUNK = 1024
MOBA_FINISH_UNROLL = 4
LOG2E = 1.4426950408889634


def _vmem_limit(nbytes):
    return int(min(VMEM_CAP, nbytes))


def _dot(a, b):
    return jnp.dot(a, b, preferred_element_type=F32)


def _dot_nt(a, b, precision=None):
    return lax.dot_general(a, b, (((1,), (1,)), ((), ())),
                           preferred_element_type=F32, precision=precision)


def _dot_tn(a, b):
    return lax.dot_general(a, b, (((0,), (0,)), ((), ())), preferred_element_type=F32)


def _rms_norm_slice(x_ref, g_ref, out_ref, sl):
    x = x_ref[sl, :].astype(F32)
    ms = jnp.mean(x * x, axis=-1, keepdims=True)
    out_ref[sl, :] = (x * lax.rsqrt(ms + NORM_EPS) * g_ref[...]).astype(out_ref.dtype)


def _rms_norm_rows(x_ref, g_ref, out_ref, rows):
    def body(r, carry):
        _rms_norm_slice(x_ref, g_ref, out_ref,
                        pl.ds(pl.multiple_of(r * NORM_ROWS, NORM_ROWS), NORM_ROWS))
        return carry
    lax.fori_loop(0, rows // NORM_ROWS, body, 0)


def _next_row_index(i, j, n_rows):
    return jnp.minimum(i + jnp.minimum(j, 1), n_rows - 1)


def _with_norm_ahead(x_ref, g_ref, xn_even_ref, xn_odd_ref, rows, step):
    i = pl.program_id(0)
    j = pl.program_id(1)

    @pl.when(jnp.logical_and(i == 0, j == 0))
    def _():
        _rms_norm_rows(x_ref, g_ref, xn_even_ref, rows)

    r = jnp.clip(j - 1, 0, rows // AHEAD_ROWS - 1)
    sl = pl.ds(pl.multiple_of(r * AHEAD_ROWS, AHEAD_ROWS), AHEAD_ROWS)
    for parity, cur, nxt in ((0, xn_even_ref, xn_odd_ref), (1, xn_odd_ref, xn_even_ref)):
        @pl.when(i % 2 == parity)
        def _(cur=cur, nxt=nxt):
            step(cur, functools.partial(_rms_norm_slice, x_ref, g_ref, nxt, sl))


def _ffn_kernel(x_ref, g_ref, wg_ref, wu_ref, wd_ref, fg_ref, o_ref, xn_even_ref, xn_odd_ref,
                *, final_norm):
    j = pl.program_id(1)

    def step(xn_ref, norm_next):
        @pl.when(j == 0)
        def _():
            o_ref[...] = x_ref[...]

        norm_next()
        xn = xn_ref[...]
        hg = _dot(xn, wg_ref[...])
        hu = _dot(xn, wu_ref[...])
        a = (0.5 * hg * jax.nn.sigmoid(hg)) * hu
        o_ref[...] += _dot(a.astype(BF16), wd_ref[...])

    _with_norm_ahead(x_ref, g_ref, xn_even_ref, xn_odd_ref, FFN_TM, step)

    if final_norm:
        @pl.when(j == pl.num_programs(1) - 1)
        def _():
            _rms_norm_rows(o_ref, fg_ref, o_ref, FFN_TM)


def _ffn(x, norm_g, wg, wu, wd, final_g, final_norm):
    t = x.shape[0]
    assert D_FF // FFN_TF > FFN_TM // AHEAD_ROWS
    vmem = (2 * 2 * FFN_TM * D_MODEL * 4
            + 2 * FFN_TM * D_MODEL * 2
            + 2 * 3 * D_MODEL * FFN_TF * 2
            + 4 * FFN_TM * FFN_TF * 4
            + (4 << 20))
    return pl.pallas_call(
        functools.partial(_ffn_kernel, final_norm=final_norm),
        grid=(t // FFN_TM, D_FF // FFN_TF),
        in_specs=[
            pl.BlockSpec((FFN_TM, D_MODEL), lambda i, j: (_next_row_index(i, j, t // FFN_TM), 0)),
            pl.BlockSpec((1, D_MODEL), lambda i, j: (0, 0)),
            pl.BlockSpec((D_MODEL, FFN_TF), lambda i, j: (0, j)),
            pl.BlockSpec((D_MODEL, FFN_TF), lambda i, j: (0, j)),
            pl.BlockSpec((FFN_TF, D_MODEL), lambda i, j: (j, 0)),
            pl.BlockSpec((1, D_MODEL), lambda i, j: (0, 0)),
        ],
        out_specs=pl.BlockSpec((FFN_TM, D_MODEL), lambda i, j: (i, 0)),
        out_shape=jax.ShapeDtypeStruct((t, D_MODEL), F32),
        scratch_shapes=[pltpu.VMEM((FFN_TM, D_MODEL), BF16), pltpu.VMEM((FFN_TM, D_MODEL), BF16)],
        compiler_params=pltpu.CompilerParams(
            dimension_semantics=("arbitrary", "arbitrary"),
            vmem_limit_bytes=_vmem_limit(vmem)),
        name="ffn_final" if final_norm else "ffn",
    )(x, norm_g, wg, wu, wd, final_g)


def _log_sigmoid(z):
    return jnp.minimum(z, 0.0) - jnp.log1p(jnp.exp(-jnp.abs(z)))


def _in_proj_kernel(x_ref, g_ref, w_ref, cs_ref, wlr_ref, wa_ref, ba_ref,
                    proj_ref, heads_ref, la_ref, xn_even_ref, xn_odd_ref, lr_ref):
    j = pl.program_id(1)

    def step(xn_ref, norm_next):
        @pl.when(j == 0)
        def _():
            lr_ref[...] = _dot(xn_ref[...], wlr_ref[...])

        norm_next()

        rows = pl.ds(pl.multiple_of(jnp.minimum(j, PROJ_TM // LA_ROWS - 1) * LA_ROWS, LA_ROWS),
                     LA_ROWS)
        lr = lr_ref[rows, :]
        lr_hi = lr.astype(BF16).astype(F32)
        lane = lax.broadcasted_iota(jnp.int32, lr.shape, 1)
        packed = jnp.where(lane < 2 * GLA_RANK, lr_hi, lr - lr_hi).astype(BF16)
        z = _dot(packed, wa_ref[...]) + ba_ref[...]
        la_ref[rows, :] = _log_sigmoid(z) * (1.0 / GLA_TAU)

        res = (_dot(xn_ref[...], w_ref[...]) * cs_ref[...]).astype(BF16)
        proj_ref[...] = res
        for h in range(MOBA_HEADS):
            heads_ref[h] = res[:, h * MOBA_HD:(h + 1) * MOBA_HD]

    _with_norm_ahead(x_ref, g_ref, xn_even_ref, xn_odd_ref, PROJ_TM, step)


def _in_proj(x, norm_g, w_main, col_scale, w_lr, w_alpha, b_alpha, bsz):
    t = x.shape[0]
    rpb = t // bsz // PROJ_TM
    assert PROJ_W // PROJ_TN > PROJ_TM // AHEAD_ROWS and 3 * GLA_RANK <= LANES_V7X
    vmem = (2 * PROJ_TM * D_MODEL * 4 + 2 * PROJ_TM * D_MODEL * 2
            + 2 * D_MODEL * PROJ_TN * 2 + 2 * 2 * PROJ_TM * PROJ_TN * 2
            + 2 * PROJ_TM * GLA_QK_W * 4 + 2 * PROJ_TM * PROJ_TN * 4
            + 2 * D_MODEL * LANES_V7X * 2 + PROJ_TM * LANES_V7X * 4 + (4 << 20))
    return pl.pallas_call(
        _in_proj_kernel,
        grid=(t // PROJ_TM, PROJ_W // PROJ_TN),
        in_specs=[
            pl.BlockSpec((PROJ_TM, D_MODEL), lambda i, j: (_next_row_index(i, j, t // PROJ_TM), 0)),
            pl.BlockSpec((1, D_MODEL), lambda i, j: (0, 0)),
            pl.BlockSpec((D_MODEL, PROJ_TN), lambda i, j: (0, j)),
            pl.BlockSpec((1, PROJ_TN), lambda i, j: (0, j)),
            pl.BlockSpec((D_MODEL, LANES_V7X), lambda i, j: (0, 0)),
            pl.BlockSpec((LANES_V7X, GLA_QK_W), lambda i, j: (0, 0)),
            pl.BlockSpec((1, GLA_QK_W), lambda i, j: (0, 0)),
        ],
        out_specs=[
            pl.BlockSpec((PROJ_TM, PROJ_TN), lambda i, j: (i, jnp.maximum(j - HEAD_TILES, 0))),
            pl.BlockSpec((None, None, MOBA_HEADS, PROJ_TM, MOBA_HD),
                         lambda i, j: (i // rpb, jnp.minimum(j, HEAD_TILES), 0, i % rpb, 0)),
            pl.BlockSpec((PROJ_TM, GLA_QK_W), lambda i, j: (i, 0)),
        ],
        out_shape=[
            jax.ShapeDtypeStruct((t, OUT_W), BF16),
            jax.ShapeDtypeStruct((bsz, HEAD_TILES + 1, MOBA_HEADS, t // bsz, MOBA_HD), BF16),
            jax.ShapeDtypeStruct((t, GLA_QK_W), F32),
        ],
        scratch_shapes=[pltpu.VMEM((PROJ_TM, D_MODEL), BF16),
                        pltpu.VMEM((PROJ_TM, D_MODEL), BF16),
                        pltpu.VMEM((PROJ_TM, LANES_V7X), F32)],
        compiler_params=pltpu.CompilerParams(
            dimension_semantics=("arbitrary", "arbitrary"),
            vmem_limit_bytes=_vmem_limit(vmem)),
        name="in_proj",
    )(x, norm_g, w_main, col_scale, w_lr, w_alpha, b_alpha)


def _bcast_rows(v, s, c):
    return jnp.broadcast_to(v, (c // s, s, GLA_DK)).reshape(c, GLA_DK)


def _gla_head(q, k, v, la, gr, gn, st_ref):
    c = q.shape[0]
    row = lax.broadcasted_iota(jnp.int32, (c, 1), 0)
    ri = lax.broadcasted_iota(jnp.int32, (c, c), 0)
    ci = lax.broadcasted_iota(jnp.int32, (c, c), 1)
    rxc = ri ^ ci

    cs = la.reshape(c // GLA_BASE, GLA_BASE, GLA_DK)
    sub = lax.broadcasted_iota(jnp.int32, cs.shape, 1)
    sh = 1
    while sh < GLA_BASE:
        cs = cs + jnp.where(sub >= sh, pltpu.roll(cs, sh, axis=1), 0.0)
        sh *= 2
    cs = cs.reshape(c, GLA_DK)

    p = _dot_nt((q * jnp.exp(cs)).astype(BF16), (k * jnp.exp(-cs)).astype(BF16))
    attn = jnp.where(rxc < GLA_BASE, jnp.where(ci <= ri, p, 0.0), 0.0)

    s = GLA_BASE
    while s < c:
        ends = cs.reshape(c // s, s, GLA_DK)[:, s - 1:s, :]
        prev = jnp.concatenate([jnp.zeros_like(ends[:1]), ends[:-1]], axis=0)
        odd = ((row // s) & 1) == 1
        qe = jnp.where(odd, jnp.exp(cs), 0.0)
        ke = jnp.where(odd, 0.0, jnp.exp(_bcast_rows(ends, s, c) - cs))
        p = _dot_nt((q * qe).astype(BF16), (k * ke).astype(BF16))
        attn = attn + jnp.where(rxc < 2 * s, p, 0.0)
        cs = cs + jnp.where(odd, _bcast_rows(prev, s, c), 0.0)
        s *= 2

    b_last = cs[c - 1:c, :]
    st = st_ref[...]
    o = _dot_nt((q * jnp.exp(cs)).astype(BF16), st.astype(BF16))
    o = o + _dot(attn.astype(BF16), v)
    kd = (k * jnp.exp(b_last - cs)).astype(BF16)
    st_ref[...] = st * jnp.exp(b_last) + _dot_tn(v, kd)

    ms = jnp.mean(o * o, axis=-1, keepdims=True)
    y = o * lax.rsqrt(ms + NORM_EPS) * gn
    g = gr.astype(F32)
    return y * (g * jax.nn.sigmoid(g))


def _gla_kernel(q_ref, k_ref, v_ref, gr_ref, la_ref, gn_ref, o_ref, st_ref):
    @pl.when(pl.program_id(1) == 0)
    def _():
        st_ref[...] = jnp.zeros_like(st_ref)

    gn = gn_ref[...]
    for h in range(GLA_HEADS):
        ks = slice(h * GLA_DK, (h + 1) * GLA_DK)
        vs = slice(h * GLA_DV, (h + 1) * GLA_DV)
        out = _gla_head(q_ref[:, ks].astype(F32), k_ref[:, ks].astype(F32), v_ref[:, vs],
                        la_ref[:, ks], gr_ref[:, vs], gn, st_ref.at[h])
        o_ref[:, vs] = out.astype(o_ref.dtype)


def _gla(proj3, la3, gn):
    b, s, _ = proj3.shape
    c = GLA_CHUNK
    return pl.pallas_call(
        _gla_kernel,
        grid=(b, s // c),
        in_specs=[
            pl.BlockSpec((None, c, GLA_QK_W), lambda i, j: (i, j, COL_GQ // GLA_QK_W)),
            pl.BlockSpec((None, c, GLA_QK_W), lambda i, j: (i, j, COL_GK // GLA_QK_W)),
            pl.BlockSpec((None, c, GLA_V_W), lambda i, j: (i, j, COL_GV // GLA_V_W)),
            pl.BlockSpec((None, c, GLA_V_W), lambda i, j: (i, j, COL_GR // GLA_V_W)),
            pl.BlockSpec((None, c, GLA_QK_W), lambda i, j: (i, j, 0)),
            pl.BlockSpec((1, GLA_DV), lambda i, j: (0, 0)),
        ],
        out_specs=pl.BlockSpec((None, c, GLA_V_W), lambda i, j: (i, j, 0)),
        out_shape=jax.ShapeDtypeStruct((b, s, GLA_V_W), BF16),
        scratch_shapes=[pltpu.VMEM((GLA_HEADS, GLA_DV, GLA_DK), F32)],
        compiler_params=pltpu.CompilerParams(
            dimension_semantics=("parallel", "arbitrary"),
            vmem_limit_bytes=_vmem_limit(32 << 20)),
        name="gla",
    )(proj3, proj3, proj3, proj3, la3, gn)


def _split3_bf16(z):
    hi = z.astype(BF16).astype(F32)
    r = z - hi
    mid = r.astype(BF16).astype(F32)
    lo = (r - mid).astype(BF16).astype(F32)
    return hi, mid, lo


def _moba_kernel(slope_ref, item_q_ref, item_k_ref, q_ref, k_ref, v_ref, o_ref,
                 vt_ref, qt_ref, ka_ref, kmean_ref, kms_ref, m_ref, l_ref, acc_ref,
                 s_ref, smax_ref, p_ref, al_ref):
    h = pl.program_id(1)
    blk = MOBA_BLOCK
    hd = MOBA_HD
    seq = k_ref.shape[0]
    nb = seq // blk
    slope2 = slope_ref[h] * LOG2E

    lane = lax.broadcasted_iota(jnp.int32, (blk, hd), 1)
    rowi = lax.broadcasted_iota(jnp.int32, (blk, hd), 0)

    def prep(j, carry):
        sl = pl.ds(pl.multiple_of(j * blk, blk), blk)
        kj = k_ref[sl, :]
        vt_ref[:, sl] = v_ref[sl, :].T
        qt_ref[0:hd, sl] = q_ref[sl, :].T
        ka_ref[sl, 0:hd] = kj
        hi, mid, lo = _split3_bf16((rowi + j * blk).astype(F32) * slope2)
        extra = jnp.where(lane == j, 1.0,
                          jnp.where(lane == nb, hi,
                                    jnp.where(lane == nb + 1, mid,
                                              jnp.where(lane == nb + 2, lo, 0.0))))
        ka_ref[sl, hd:2 * hd] = extra.astype(BF16)
        kmean_ref[pl.ds(j, 1), :] = jnp.mean(kj.astype(F32), axis=0, keepdims=True)
        return carry
    lax.fori_loop(0, nb, prep, 0)
    qt_ref[hd + nb:2 * hd, :] = jnp.ones((hd - nb, seq), BF16)
    kms_ref[...] = jnp.concatenate(_split3_bf16(kmean_ref[...]), axis=0).astype(BF16)

    chunk = MOBA_SEL_CHUNK

    def select(c, carry):
        cs = pl.ds(pl.multiple_of(c * chunk, chunk), chunk)
        g3 = _dot(kms_ref[...], qt_ref[0:hd, cs])
        gate = g3[0:nb] + g3[nb:2 * nb] + g3[2 * nb:3 * nb]
        bidx = lax.broadcasted_iota(jnp.int32, gate.shape, 0)
        qblk = c * (chunk // blk) + lax.broadcasted_iota(jnp.int32, gate.shape, 1) // blk
        gate = jnp.where(bidx < qblk, gate, -jnp.inf)
        mask = jnp.full(gate.shape, MASK_VALUE, F32)
        for _ in range(MOBA_TOPK):
            m = jnp.max(gate, axis=0, keepdims=True)
            first = jnp.min(jnp.where(gate == m, bidx, nb), axis=0, keepdims=True)
            pick = bidx == jnp.where(m > -jnp.inf, first, -1)
            mask = jnp.where(pick, 0.0, mask)
            gate = jnp.where(pick, -jnp.inf, gate)
        qt_ref[hd:hd + nb, cs] = mask.astype(BF16)
        return carry
    lax.fori_loop(0, seq // chunk, select, 0)

    m_ref[...] = jnp.full(m_ref.shape, MASK_VALUE, F32)
    l_ref[...] = jnp.zeros(l_ref.shape, F32)

    def clear(i, carry):
        acc_ref[i] = jnp.zeros((hd, blk), F32)
        return carry
    lax.fori_loop(0, nb, clear, 0)

    kpos = lax.broadcasted_iota(jnp.int32, (blk, blk), 0)
    qpos = lax.broadcasted_iota(jnp.int32, (blk, blk), 1)
    step_rows = s_ref.shape[0]

    def make_stages(own_block):
        rows = blk if own_block else MOBA_GROUP * blk
        units = step_rows // rows

        def item(n):
            if own_block:
                i = j = n
            else:
                i = item_q_ref[n]
                j = item_k_ref[n]
            return (i, pl.ds(pl.multiple_of(i * blk, blk), blk),
                    pl.ds(pl.multiple_of(j * blk, blk), rows))

        def scores(n0):
            for u in range(units):
                i, qs, ks = item(n0 + u)
                if own_block:
                    bias = (kpos + i * blk).astype(F32) * slope2
                    s = _dot(k_ref[ks, :], qt_ref[0:hd, qs]) + bias
                    s = jnp.where(kpos <= qpos, s, MASK_VALUE)
                else:
                    s = _dot(ka_ref[ks, :], qt_ref[:, qs])
                s_ref[u * rows:(u + 1) * rows, :] = s
                smax_ref[u] = jnp.max(s, axis=0, keepdims=True)

        def softmax(n0):
            for u in range(units):
                i, _, _ = item(n0 + u)
                row = pl.ds(i, 1)
                m_old = m_ref[row, :]
                m_new = jnp.maximum(m_old, smax_ref[u])
                alpha = jnp.exp2(m_old - m_new)
                p = jnp.exp2(s_ref[u * rows:(u + 1) * rows, :] - m_new)
                l_ref[row, :] = alpha * l_ref[row, :] + jnp.sum(p, axis=0, keepdims=True)
                m_ref[row, :] = m_new
                p_ref[u * rows:(u + 1) * rows, :] = p.astype(BF16)
                al_ref[u] = alpha

        def values(n0):
            for u in range(units):
                i, _, ks = item(n0 + u)
                acc_ref[i] = (al_ref[u] * acc_ref[i]
                              + _dot(vt_ref[:, ks], p_ref[u * rows:(u + 1) * rows, :]))

        return units, scores, softmax, values

    def run_pipeline(n_items, stages):
        units, scores, softmax, values = stages
        steps = n_items // units
        scores(0)
        softmax(0)
        scores(units)

        def body(t, carry):
            values(t * units)
            softmax((t + 1) * units)
            scores((t + 2) * units)
            return carry
        lax.fori_loop(0, steps - 2, body, 0)
        values((steps - 2) * units)
        softmax((steps - 1) * units)
        values((steps - 1) * units)

    run_pipeline(nb, make_stages(True))
    run_pipeline(item_q_ref.shape[0], make_stages(False))

    def finish(i0, carry):
        for u in range(MOBA_FINISH_UNROLL):
            i = i0 * MOBA_FINISH_UNROLL + u
            qs = pl.ds(pl.multiple_of(i * blk, blk), blk)
            inv = 1.0 / l_ref[pl.ds(i, 1), :]
            o_ref[qs, :] = (acc_ref[i] * inv).T.astype(o_ref.dtype)
        return carry
    lax.fori_loop(0, nb // MOBA_FINISH_UNROLL, finish, 0)


def _moba(qkv, slopes):
    b, _, _, s, _ = qkv.shape
    blk = MOBA_BLOCK
    nb = s // blk
    hd = MOBA_HD
    assert nb + 3 <= hd and s % MOBA_SEL_CHUNK == 0
    groups = [(i, j) for i in range(nb) for j in range(0, i, MOBA_GROUP)]
    own_units = MOBA_STEP_BLOCKS
    grp_units = MOBA_STEP_BLOCKS // MOBA_GROUP
    assert MOBA_STEP_BLOCKS % MOBA_GROUP == 0 and nb % MOBA_GROUP == 0
    assert nb % own_units == 0 and nb // own_units >= 2
    assert len(groups) % grp_units == 0 and len(groups) // grp_units >= 2
    assert nb % MOBA_FINISH_UNROLL == 0
    item_q = jnp.asarray(np.array([g[0] for g in groups], np.int32))
    item_k = jnp.asarray(np.array([g[1] for g in groups], np.int32))
    step_rows = MOBA_STEP_BLOCKS * blk
    vmem = (2 * 4 * s * hd * 2
            + s * hd * 2 + 2 * 2 * s * hd * 2
            + nb * hd * blk * 4
            + step_rows * blk * 6
            + 2 * step_rows * blk * 4 + (4 << 20))
    smem = pl.BlockSpec(memory_space=pltpu.SMEM)
    return pl.pallas_call(
        _moba_kernel,
        grid=(b, MOBA_HEADS),
        in_specs=[
            smem, smem, smem,
            pl.BlockSpec((None, None, None, s, hd), lambda bi, h: (bi, 0, h, 0, 0)),
            pl.BlockSpec((None, None, None, s, hd), lambda bi, h: (bi, 1, h, 0, 0)),
            pl.BlockSpec((None, None, None, s, hd), lambda bi, h: (bi, 2, h, 0, 0)),
        ],
        out_specs=pl.BlockSpec((None, None, s, hd), lambda bi, h: (bi, h, 0, 0)),
        out_shape=jax.ShapeDtypeStruct((b, MOBA_HEADS, s, hd), BF16),
        scratch_shapes=[
            pltpu.VMEM((hd, s), BF16),
            pltpu.VMEM((2 * hd, s), BF16),
            pltpu.VMEM((s, 2 * hd), BF16),
            pltpu.VMEM((nb, hd), F32),
            pltpu.VMEM((3 * nb, hd), BF16),
            pltpu.VMEM((nb, blk), F32),
            pltpu.VMEM((nb, blk), F32),
            pltpu.VMEM((nb, hd, blk), F32),
            pltpu.VMEM((step_rows, blk), F32),
            pltpu.VMEM((own_units, 1, blk), F32),
            pltpu.VMEM((step_rows, blk), BF16),
            pltpu.VMEM((own_units, 1, blk), F32),
        ],
        compiler_params=pltpu.CompilerParams(
            dimension_semantics=("parallel", "parallel"),
            vmem_limit_bytes=_vmem_limit(vmem)),
        name="moba",
    )(slopes, item_q, item_k, qkv, qkv, qkv)


def _merge_kernel(x_ref, oa_ref, ob_ref, ga_ref, gb_ref, wa_ref, wb_ref, wo_ref, o_ref):
    ya = _dot(oa_ref[...], wa_ref[...])
    ob = jnp.concatenate([ob_ref[h] for h in range(MOBA_HEADS)], axis=1)
    yb = _dot(ob, wb_ref[...])
    mix = (jax.nn.sigmoid(ga_ref[...].astype(F32)) * ya
           + jax.nn.sigmoid(gb_ref[...].astype(F32)) * yb)
    o_ref[...] = x_ref[...] + _dot(mix.astype(BF16), wo_ref[...])


def _merge(x, oa, ob, proj, wa, wb, wo):
    t = x.shape[0]
    tm = MERGE_TM
    rpb = ob.shape[2] // tm
    const = dict(pipeline_mode=pl.Buffered(1))
    vmem = (2 * 2 * tm * D_MODEL * 4 + 2 * 2 * tm * GLA_V_W * 2 + 2 * 2 * tm * D_MODEL * 2
            + (2 * GLA_V_W + D_MODEL) * D_MODEL * 2 + 4 * tm * D_MODEL * 4 + (4 << 20))
    return pl.pallas_call(
        _merge_kernel,
        grid=(t // tm,),
        in_specs=[
            pl.BlockSpec((tm, D_MODEL), lambda i: (i, 0)),
            pl.BlockSpec((tm, GLA_V_W), lambda i: (i, 0)),
            pl.BlockSpec((None, MOBA_HEADS, tm, MOBA_HD), lambda i: (i // rpb, 0, i % rpb, 0)),
            pl.BlockSpec((tm, D_MODEL), lambda i: (i, COL_GA // D_MODEL)),
            pl.BlockSpec((tm, D_MODEL), lambda i: (i, COL_GB // D_MODEL)),
            pl.BlockSpec((GLA_V_W, D_MODEL), lambda i: (0, 0), **const),
            pl.BlockSpec((MOBA_W, D_MODEL), lambda i: (0, 0), **const),
            pl.BlockSpec((D_MODEL, D_MODEL), lambda i: (0, 0), **const),
        ],
        out_specs=pl.BlockSpec((tm, D_MODEL), lambda i: (i, 0)),
        out_shape=jax.ShapeDtypeStruct((t, D_MODEL), F32),
        compiler_params=pltpu.CompilerParams(
            dimension_semantics=("parallel",),
            vmem_limit_bytes=_vmem_limit(vmem)),
        name="merge",
    )(x, oa, ob, proj, proj, wa, wb, wo)


def kernel(x, ffn1_norm, ffn1_w_gate, ffn1_w_up, ffn1_w_down, mix_norm, w_in, gla_w_alpha,
           gla_b_alpha, gla_out_norm, w_branch_gla, w_branch_moba, w_out, ffn2_norm,
           ffn2_w_gate, ffn2_w_up, ffn2_w_down, final_norm):
    bsz, seq, d = x.shape
    t = bsz * seq
    depth = ffn1_norm.shape[0]
    assert depth >= 1 and d == D_MODEL and seq % GLA_CHUNK == 0 and seq % MOBA_BLOCK == 0
    assert t % FFN_TM == 0 and seq % PROJ_TM == 0 and seq % MERGE_TM == 0
    assert OUT_W % PROJ_TN == 0
    assert PROJ_TM % LA_ROWS == 0 and PROJ_TM // LA_ROWS <= PROJ_W // PROJ_TN

    slopes = jnp.exp2(-8.0 * jnp.arange(1, MOBA_HEADS + 1, dtype=F32) / MOBA_HEADS)
    col_scale = jnp.ones((1, PROJ_W), F32)
    col_scale = col_scale.at[:, HEAD_COLS + COL_GQ:HEAD_COLS + COL_GQ + GLA_QK_W].set(GLA_DK ** -0.5)
    col_scale = col_scale.at[:, :MOBA_W].set(MOBA_HD ** -0.5 * LOG2E)
    final_g = final_norm.reshape(1, D_MODEL)

    xf = x.reshape(t, D_MODEL)
    for l in range(depth):
        last = l == depth - 1
        w_l = w_in[l]
        w_main = jnp.concatenate([w_l[:, slice(*W_IN_MOBA)], w_l[:, slice(*W_IN_GATES)],
                                  w_l[:, slice(*W_IN_GLA)]], axis=1).astype(BF16)
        w_lr = jnp.pad(jnp.tile(w_l[:, slice(*W_IN_LR)], (1, 3)),
                       ((0, 0), (0, LANES_V7X - 3 * GLA_RANK))).astype(BF16)
        wa_hi = gla_w_alpha[l].astype(BF16)
        wa_mid = (gla_w_alpha[l] - wa_hi.astype(F32)).astype(BF16)
        w_alpha = jnp.pad(jnp.concatenate([wa_hi, wa_mid, wa_hi], axis=0),
                          ((0, LANES_V7X - 3 * GLA_RANK), (0, 0)))

        xf = _ffn(xf, ffn1_norm[l].reshape(1, D_MODEL), ffn1_w_gate[l].astype(BF16),
                  ffn1_w_up[l].astype(BF16), ffn1_w_down[l].astype(BF16), final_g, False)
        proj, qkv, log_a = _in_proj(xf, mix_norm[l].reshape(1, D_MODEL), w_main, col_scale,
                                    w_lr, w_alpha, gla_b_alpha[l].reshape(1, GLA_QK_W), bsz)
        proj3 = proj.reshape(bsz, seq, OUT_W)
        oa = _gla(proj3, log_a.reshape(bsz, seq, GLA_QK_W), gla_out_norm[l].reshape(1, GLA_DV))
        ob = _moba(qkv, slopes)
        xf = _merge(xf, oa.reshape(t, GLA_V_W), ob, proj,
                    w_branch_gla[l].astype(BF16), w_branch_moba[l].astype(BF16),
                    w_out[l].astype(BF16))
        xf = _ffn(xf, ffn2_norm[l].reshape(1, D_MODEL), ffn2_w_gate[l].astype(BF16),
                  ffn2_w_up[l].astype(BF16), ffn2_w_down[l].astype(BF16), final_g, last)
    return xf.reshape(bsz, seq, D_MODEL)
```

```python
import functools

import jax
import jax.numpy as jnp
import numpy as np
from jax import lax
from jax.experimental import pallas as pl
from jax.experimental.pallas import tpu as pltpu

F32 = jnp.float32
BF16 = jnp.bfloat16

D_MODEL = 2048
D_FF = 5632
GLA_HEADS = 4
GLA_DK = 128
GLA_DV = 256
GLA_RANK = 16
GLA_TAU = 16.0
MOBA_HEADS = 8
MOBA_HD = 128
MOBA_BLOCK = 256
MOBA_TOPK = 3
NORM_EPS = 1e-6

GLA_QK_W = GLA_HEADS * GLA_DK
GLA_V_W = GLA_HEADS * GLA_DV
MOBA_W = MOBA_HEADS * MOBA_HD

HEAD_COLS = 3 * MOBA_W
COL_GA = 0
COL_GB = COL_GA + D_MODEL
COL_GQ = COL_GB + D_MODEL
COL_GK = COL_GQ + GLA_QK_W
COL_GV = COL_GK + GLA_QK_W
COL_GR = COL_GV + GLA_V_W
OUT_W = COL_GR + GLA_V_W
PROJ_W = HEAD_COLS + OUT_W
W_IN_GLA = (0, 2 * GLA_QK_W + 2 * GLA_V_W)
W_IN_LR = (W_IN_GLA[1], W_IN_GLA[1] + GLA_RANK)
W_IN_MOBA = (W_IN_LR[1], W_IN_LR[1] + 3 * MOBA_W)
W_IN_GATES = (W_IN_MOBA[1], W_IN_MOBA[1] + 2 * D_MODEL)

LANES_V7X = 128
SUBLANES_V7X = 8
VMEM_BYTES_V7X = 64 << 20
VMEM_CAP = VMEM_BYTES_V7X - (6 << 20)

FFN_TM = 1024
FFN_TF = 512
PROJ_TM = 1024
PROJ_TN = MOBA_W
HEAD_TILES = HEAD_COLS // PROJ_TN
LA_ROWS = 128
GLA_CHUNK = 256
GLA_BASE = SUBLANES_V7X
MERGE_TM = 512
NORM_ROWS = 256
AHEAD_ROWS = 128

MASK_VALUE = -1e30
MOBA_STEP_BLOCKS = 8
MOBA_GROUP = 4
MOBA_SEL_CHUNK = 1024
MOBA_FINISH_UNROLL = 4
LOG2E = 1.4426950408889634


def _vmem_limit(nbytes):
    return int(min(VMEM_CAP, nbytes))


def _dot(a, b):
    return jnp.dot(a, b, preferred_element_type=F32)


def _dot_nt(a, b, precision=None):
    return lax.dot_general(a, b, (((1,), (1,)), ((), ())),
                           preferred_element_type=F32, precision=precision)


def _dot_tn(a, b):
    return lax.dot_general(a, b, (((0,), (0,)), ((), ())), preferred_element_type=F32)


def _rms_norm_slice(x_ref, g_ref, out_ref, sl):
    x = x_ref[sl, :].astype(F32)
    ms = jnp.mean(x * x, axis=-1, keepdims=True)
    out_ref[sl, :] = (x * lax.rsqrt(ms + NORM_EPS) * g_ref[...]).astype(out_ref.dtype)


def _rms_norm_rows(x_ref, g_ref, out_ref, rows):
    def body(r, carry):
        _rms_norm_slice(x_ref, g_ref, out_ref,
                        pl.ds(pl.multiple_of(r * NORM_ROWS, NORM_ROWS), NORM_ROWS))
        return carry
    lax.fori_loop(0, rows // NORM_ROWS, body, 0)


def _next_row_index(i, j, n_rows):
    return jnp.minimum(i + jnp.minimum(j, 1), n_rows - 1)


def _with_norm_ahead(x_ref, g_ref, xn_even_ref, xn_odd_ref, rows, step):
    i = pl.program_id(0)
    j = pl.program_id(1)

    @pl.when(jnp.logical_and(i == 0, j == 0))
    def _():
        _rms_norm_rows(x_ref, g_ref, xn_even_ref, rows)

    r = jnp.clip(j - 1, 0, rows // AHEAD_ROWS - 1)
    sl = pl.ds(pl.multiple_of(r * AHEAD_ROWS, AHEAD_ROWS), AHEAD_ROWS)
    for parity, cur, nxt in ((0, xn_even_ref, xn_odd_ref), (1, xn_odd_ref, xn_even_ref)):
        @pl.when(i % 2 == parity)
        def _(cur=cur, nxt=nxt):
            step(cur, functools.partial(_rms_norm_slice, x_ref, g_ref, nxt, sl))


def _ffn_kernel(x_ref, g_ref, wg_ref, wu_ref, wd_ref, fg_ref, o_ref, xn_ref, *, final_norm):
    j = pl.program_id(1)

    @pl.when(j == 0)
    def _():
        _rms_norm_rows(x_ref, g_ref, xn_ref, FFN_TM)
        o_ref[...] = x_ref[...]

    xn = xn_ref[...]
    hg = _dot(xn, wg_ref[...])
    hu = _dot(xn, wu_ref[...])
    a = (0.5 * hg * jax.nn.sigmoid(hg)) * hu
    o_ref[...] += _dot(a.astype(BF16), wd_ref[...])

    if final_norm:
        @pl.when(j == pl.num_programs(1) - 1)
        def _():
            _rms_norm_rows(o_ref, fg_ref, o_ref, FFN_TM)


def _ffn(x, norm_g, wg, wu, wd, final_g, final_norm):
    t = x.shape[0]
    vmem = (2 * 2 * FFN_TM * D_MODEL * 4
            + FFN_TM * D_MODEL * 2
            + 2 * 3 * D_MODEL * FFN_TF * 2
            + 4 * FFN_TM * FFN_TF * 4
            + (4 << 20))
    return pl.pallas_call(
        functools.partial(_ffn_kernel, final_norm=final_norm),
        grid=(t // FFN_TM, D_FF // FFN_TF),
        in_specs=[
            pl.BlockSpec((FFN_TM, D_MODEL), lambda i, j: (i, 0)),
            pl.BlockSpec((1, D_MODEL), lambda i, j: (0, 0)),
            pl.BlockSpec((D_MODEL, FFN_TF), lambda i, j: (0, j)),
            pl.BlockSpec((D_MODEL, FFN_TF), lambda i, j: (0, j)),
            pl.BlockSpec((FFN_TF, D_MODEL), lambda i, j: (j, 0)),
            pl.BlockSpec((1, D_MODEL), lambda i, j: (0, 0)),
        ],
        out_specs=pl.BlockSpec((FFN_TM, D_MODEL), lambda i, j: (i, 0)),
        out_shape=jax.ShapeDtypeStruct((t, D_MODEL), F32),
        scratch_shapes=[pltpu.VMEM((FFN_TM, D_MODEL), BF16)],
        compiler_params=pltpu.CompilerParams(
            dimension_semantics=("parallel", "arbitrary"),
            vmem_limit_bytes=_vmem_limit(vmem)),
        name="ffn_final" if final_norm else "ffn",
    )(x, norm_g, wg, wu, wd, final_g)


def _log_sigmoid(z):
    return jnp.minimum(z, 0.0) - jnp.log1p(jnp.exp(-jnp.abs(z)))


def _in_proj_kernel(x_ref, g_ref, w_ref, cs_ref, wlr_ref, wa_ref, ba_ref,
                    proj_ref, heads_ref, la_ref, xn_even_ref, xn_odd_ref, lr_ref):
    j = pl.program_id(1)

    def step(xn_ref, norm_next):
        @pl.when(j == 0)
        def _():
            lr_ref[...] = _dot(xn_ref[...], wlr_ref[...])

        norm_next()

        rows = pl.ds(pl.multiple_of(jnp.minimum(j, PROJ_TM // LA_ROWS - 1) * LA_ROWS, LA_ROWS),
                     LA_ROWS)
        lr = lr_ref[rows, :]
        lr_hi = lr.astype(BF16).astype(F32)
        lane = lax.broadcasted_iota(jnp.int32, lr.shape, 1)
        packed = jnp.where(lane < 2 * GLA_RANK, lr_hi, lr - lr_hi).astype(BF16)
        z = _dot(packed, wa_ref[...]) + ba_ref[...]
        la_ref[rows, :] = _log_sigmoid(z) * (1.0 / GLA_TAU)

        res = (_dot(xn_ref[...], w_ref[...]) * cs_ref[...]).astype(BF16)
        proj_ref[...] = res
        for h in range(MOBA_HEADS):
            heads_ref[h] = res[:, h * MOBA_HD:(h + 1) * MOBA_HD]

    _with_norm_ahead(x_ref, g_ref, xn_even_ref, xn_odd_ref, PROJ_TM, step)


def _in_proj(x, norm_g, w_main, col_scale, w_lr, w_alpha, b_alpha, bsz):
    t = x.shape[0]
    rpb = t // bsz // PROJ_TM
    assert PROJ_W // PROJ_TN > PROJ_TM // AHEAD_ROWS and 3 * GLA_RANK <= LANES_V7X
    vmem = (2 * PROJ_TM * D_MODEL * 4 + 2 * PROJ_TM * D_MODEL * 2
            + 2 * D_MODEL * PROJ_TN * 2 + 2 * 2 * PROJ_TM * PROJ_TN * 2
            + 2 * PROJ_TM * GLA_QK_W * 4 + 2 * PROJ_TM * PROJ_TN * 4
            + 2 * D_MODEL * LANES_V7X * 2 + PROJ_TM * LANES_V7X * 4 + (4 << 20))
    return pl.pallas_call(
        _in_proj_kernel,
        grid=(t // PROJ_TM, PROJ_W // PROJ_TN),
        in_specs=[
            pl.BlockSpec((PROJ_TM, D_MODEL), lambda i, j: (_next_row_index(i, j, t // PROJ_TM), 0)),
            pl.BlockSpec((1, D_MODEL), lambda i, j: (0, 0)),
            pl.BlockSpec((D_MODEL, PROJ_TN), lambda i, j: (0, j)),
            pl.BlockSpec((1, PROJ_TN), lambda i, j: (0, j)),
            pl.BlockSpec((D_MODEL, LANES_V7X), lambda i, j: (0, 0)),
            pl.BlockSpec((LANES_V7X, GLA_QK_W), lambda i, j: (0, 0)),
            pl.BlockSpec((1, GLA_QK_W), lambda i, j: (0, 0)),
        ],
        out_specs=[
            pl.BlockSpec((PROJ_TM, PROJ_TN), lambda i, j: (i, jnp.maximum(j - HEAD_TILES, 0))),
            pl.BlockSpec((None, None, MOBA_HEADS, PROJ_TM, MOBA_HD),
                         lambda i, j: (i // rpb, jnp.minimum(j, HEAD_TILES), 0, i % rpb, 0)),
            pl.BlockSpec((PROJ_TM, GLA_QK_W), lambda i, j: (i, 0)),
        ],
        out_shape=[
            jax.ShapeDtypeStruct((t, OUT_W), BF16),
            jax.ShapeDtypeStruct((bsz, HEAD_TILES + 1, MOBA_HEADS, t // bsz, MOBA_HD), BF16),
            jax.ShapeDtypeStruct((t, GLA_QK_W), F32),
        ],
        scratch_shapes=[pltpu.VMEM((PROJ_TM, D_MODEL), BF16),
                        pltpu.VMEM((PROJ_TM, D_MODEL), BF16),
                        pltpu.VMEM((PROJ_TM, LANES_V7X), F32)],
        compiler_params=pltpu.CompilerParams(
            dimension_semantics=("arbitrary", "arbitrary"),
            vmem_limit_bytes=_vmem_limit(vmem)),
        name="in_proj",
    )(x, norm_g, w_main, col_scale, w_lr, w_alpha, b_alpha)


def _bcast_rows(v, s, c):
    return jnp.broadcast_to(v, (c // s, s, GLA_DK)).reshape(c, GLA_DK)


def _gla_head(q, k, v, la, gr, gn, st_ref):
    c = q.shape[0]
    row = lax.broadcasted_iota(jnp.int32, (c, 1), 0)
    ri = lax.broadcasted_iota(jnp.int32, (c, c), 0)
    ci = lax.broadcasted_iota(jnp.int32, (c, c), 1)
    rxc = ri ^ ci

    cs = la.reshape(c // GLA_BASE, GLA_BASE, GLA_DK)
    sub = lax.broadcasted_iota(jnp.int32, cs.shape, 1)
    sh = 1
    while sh < GLA_BASE:
        cs = cs + jnp.where(sub >= sh, pltpu.roll(cs, sh, axis=1), 0.0)
        sh *= 2
    cs = cs.reshape(c, GLA_DK)

    p = _dot_nt((q * jnp.exp(cs)).astype(BF16), (k * jnp.exp(-cs)).astype(BF16))
    attn = jnp.where(rxc < GLA_BASE, jnp.where(ci <= ri, p, 0.0), 0.0)

    s = GLA_BASE
    while s < c:
        ends = cs.reshape(c // s, s, GLA_DK)[:, s - 1:s, :]
        prev = jnp.concatenate([jnp.zeros_like(ends[:1]), ends[:-1]], axis=0)
        odd = ((row // s) & 1) == 1
        qe = jnp.where(odd, jnp.exp(cs), 0.0)
        ke = jnp.where(odd, 0.0, jnp.exp(_bcast_rows(ends, s, c) - cs))
        p = _dot_nt((q * qe).astype(BF16), (k * ke).astype(BF16))
        attn = attn + jnp.where(rxc < 2 * s, p, 0.0)
        cs = cs + jnp.where(odd, _bcast_rows(prev, s, c), 0.0)
        s *= 2

    b_last = cs[c - 1:c, :]
    st = st_ref[...]
    o = _dot_nt((q * jnp.exp(cs)).astype(BF16), st.astype(BF16))
    o = o + _dot(attn.astype(BF16), v)
    kd = (k * jnp.exp(b_last - cs)).astype(BF16)
    st_ref[...] = st * jnp.exp(b_last) + _dot_tn(v, kd)

    ms = jnp.mean(o * o, axis=-1, keepdims=True)
    y = o * lax.rsqrt(ms + NORM_EPS) * gn
    g = gr.astype(F32)
    return y * (g * jax.nn.sigmoid(g))


def _gla_kernel(q_ref, k_ref, v_ref, gr_ref, la_ref, gn_ref, o_ref, st_ref):
    @pl.when(pl.program_id(1) == 0)
    def _():
        st_ref[...] = jnp.zeros_like(st_ref)

    gn = gn_ref[...]
    for h in range(GLA_HEADS):
        ks = slice(h * GLA_DK, (h + 1) * GLA_DK)
        vs = slice(h * GLA_DV, (h + 1) * GLA_DV)
        out = _gla_head(q_ref[:, ks].astype(F32), k_ref[:, ks].astype(F32), v_ref[:, vs],
                        la_ref[:, ks], gr_ref[:, vs], gn, st_ref.at[h])
        o_ref[:, vs] = out.astype(o_ref.dtype)


def _gla(proj3, la3, gn):
    b, s, _ = proj3.shape
    c = GLA_CHUNK
    return pl.pallas_call(
        _gla_kernel,
        grid=(b, s // c),
        in_specs=[
            pl.BlockSpec((None, c, GLA_QK_W), lambda i, j: (i, j, COL_GQ // GLA_QK_W)),
            pl.BlockSpec((None, c, GLA_QK_W), lambda i, j: (i, j, COL_GK // GLA_QK_W)),
            pl.BlockSpec((None, c, GLA_V_W), lambda i, j: (i, j, COL_GV // GLA_V_W)),
            pl.BlockSpec((None, c, GLA_V_W), lambda i, j: (i, j, COL_GR // GLA_V_W)),
            pl.BlockSpec((None, c, GLA_QK_W), lambda i, j: (i, j, 0)),
            pl.BlockSpec((1, GLA_DV), lambda i, j: (0, 0)),
        ],
        out_specs=pl.BlockSpec((None, c, GLA_V_W), lambda i, j: (i, j, 0)),
        out_shape=jax.ShapeDtypeStruct((b, s, GLA_V_W), BF16),
        scratch_shapes=[pltpu.VMEM((GLA_HEADS, GLA_DV, GLA_DK), F32)],
        compiler_params=pltpu.CompilerParams(
            dimension_semantics=("parallel", "arbitrary"),
            vmem_limit_bytes=_vmem_limit(32 << 20)),
        name="gla",
    )(proj3, proj3, proj3, proj3, la3, gn)


def _split3_bf16(z):
    hi = z.astype(BF16).astype(F32)
    r = z - hi
    mid = r.astype(BF16).astype(F32)
    lo = (r - mid).astype(BF16).astype(F32)
    return hi, mid, lo


def _trunc_bf16(x):
    bits = lax.bitcast_convert_type(x, jnp.uint32) & jnp.uint32(0xFFFF0000)
    return lax.bitcast_convert_type(bits, F32)


def _moba_key_table(slopes, seq):
    nb = seq // MOBA_BLOCK
    pos = jnp.arange(seq)
    z = (slopes * LOG2E)[:, None] * pos.astype(F32)[None, :]
    hi = _trunc_bf16(z)
    mid = _trunc_bf16(z - hi)
    lo = _trunc_bf16(z - hi - mid)
    lane = jnp.arange(MOBA_HD)
    onehot = (lane[None, :] == (pos // MOBA_BLOCK)[:, None]).astype(F32)
    table = (onehot[None] + hi[..., None] * (lane == nb) + mid[..., None] * (lane == nb + 1)
             + lo[..., None] * (lane == nb + 2))
    return table.astype(BF16)


def _moba_kernel(slope_ref, item_q_ref, item_k_ref, q_ref, k_ref, v_ref, kx_ref, o_ref,
                 vt_ref, qt_ref, ka_ref, kmean_ref, kms_ref, m_ref, l_ref, acc_ref,
                 s_ref, smax_ref, p_ref, al_ref):
    h = pl.program_id(1)
    blk = MOBA_BLOCK
    hd = MOBA_HD
    seq = k_ref.shape[0]
    nb = seq // blk
    slope2 = slope_ref[h] * LOG2E

    def prep(j, carry):
        sl = pl.ds(pl.multiple_of(j * blk, blk), blk)
        kj = k_ref[sl, :]
        vt_ref[:, sl] = v_ref[sl, :].T
        qt_ref[0:hd, sl] = q_ref[sl, :].T
        ka_ref[sl, 0:hd] = kj
        ka_ref[sl, hd:2 * hd] = kx_ref[sl, :]
        kmean_ref[pl.ds(j, 1), :] = jnp.mean(kj.astype(F32), axis=0, keepdims=True)
        return carry
    lax.fori_loop(0, nb, prep, 0)
    qt_ref[hd + nb:2 * hd, :] = jnp.ones((hd - nb, seq), BF16)
    kms_ref[...] = jnp.concatenate(_split3_bf16(kmean_ref[...]), axis=0).astype(BF16)

    chunk = MOBA_SEL_CHUNK

    def select(c, carry):
        cs = pl.ds(pl.multiple_of(c * chunk, chunk), chunk)
        g3 = _dot(kms_ref[...], qt_ref[0:hd, cs])
        gate = g3[0:nb] + g3[nb:2 * nb] + g3[2 * nb:3 * nb]
        bidx = lax.broadcasted_iota(jnp.int32, gate.shape, 0)
        qblk = c * (chunk // blk) + lax.broadcasted_iota(jnp.int32, gate.shape, 1) // blk
        gate = jnp.where(bidx < qblk, gate, -jnp.inf)
        mask = jnp.full(gate.shape, MASK_VALUE, F32)
        for _ in range(MOBA_TOPK):
            m = jnp.max(gate, axis=0, keepdims=True)
            first = jnp.min(jnp.where(gate == m, bidx, nb), axis=0, keepdims=True)
            pick = bidx == jnp.where(m > -jnp.inf, first, -1)
            mask = jnp.where(pick, 0.0, mask)
            gate = jnp.where(pick, -jnp.inf, gate)
        qt_ref[hd:hd + nb, cs] = mask.astype(BF16)
        return carry
    lax.fori_loop(0, seq // chunk, select, 0)

    m_ref[...] = jnp.full(m_ref.shape, MASK_VALUE, F32)
    l_ref[...] = jnp.zeros(l_ref.shape, F32)

    def clear(i, carry):
        acc_ref[i] = jnp.zeros((hd, blk), F32)
        return carry
    lax.fori_loop(0, nb, clear, 0)

    kpos = lax.broadcasted_iota(jnp.int32, (blk, blk), 0)
    qpos = lax.broadcasted_iota(jnp.int32, (blk, blk), 1)
    step_rows = s_ref.shape[0]

    def make_stages(own_block):
        rows = blk if own_block else MOBA_GROUP * blk
        units = step_rows // rows

        def item(n):
            if own_block:
                i = j = n
            else:
                i = item_q_ref[n]
                j = item_k_ref[n]
            return (i, pl.ds(pl.multiple_of(i * blk, blk), blk),
                    pl.ds(pl.multiple_of(j * blk, blk), rows))

        def scores(n0):
            for u in range(units):
                i, qs, ks = item(n0 + u)
                if own_block:
                    bias = (kpos + i * blk).astype(F32) * slope2
                    s = _dot(k_ref[ks, :], qt_ref[0:hd, qs]) + bias
                    s = jnp.where(kpos <= qpos, s, MASK_VALUE)
                else:
                    s = _dot(ka_ref[ks, :], qt_ref[:, qs])
                s_ref[u * rows:(u + 1) * rows, :] = s
                smax_ref[u] = jnp.max(s, axis=0, keepdims=True)

        def softmax(n0):
            for u in range(units):
                i, _, _ = item(n0 + u)
                row = pl.ds(i, 1)
                m_old = m_ref[row, :]
                m_new = jnp.maximum(m_old, smax_ref[u])
                alpha = jnp.exp2(m_old - m_new)
                p = jnp.exp2(s_ref[u * rows:(u + 1) * rows, :] - m_new)
                l_ref[row, :] = alpha * l_ref[row, :] + jnp.sum(p, axis=0, keepdims=True)
                m_ref[row, :] = m_new
                p_ref[u * rows:(u + 1) * rows, :] = p.astype(BF16)
                al_ref[u] = alpha

        def values(n0):
            for u in range(units):
                i, _, ks = item(n0 + u)
                acc_ref[i] = (al_ref[u] * acc_ref[i]
                              + _dot(vt_ref[:, ks], p_ref[u * rows:(u + 1) * rows, :]))

        return units, scores, softmax, values

    def run_pipeline(n_items, stages):
        units, scores, softmax, values = stages
        steps = n_items // units
        scores(0)
        softmax(0)
        scores(units)

        def body(t, carry):
            values(t * units)
            softmax((t + 1) * units)
            scores((t + 2) * units)
            return carry
        lax.fori_loop(0, steps - 2, body, 0)
        values((steps - 2) * units)
        softmax((steps - 1) * units)
        values((steps - 1) * units)

    run_pipeline(nb, make_stages(True))
    run_pipeline(item_q_ref.shape[0], make_stages(False))

    def finish(i0, carry):
        for u in range(MOBA_FINISH_UNROLL):
            i = i0 * MOBA_FINISH_UNROLL + u
            qs = pl.ds(pl.multiple_of(i * blk, blk), blk)
            inv = 1.0 / l_ref[pl.ds(i, 1), :]
            o_ref[qs, :] = (acc_ref[i] * inv).T.astype(o_ref.dtype)
        return carry
    lax.fori_loop(0, nb // MOBA_FINISH_UNROLL, finish, 0)


def _moba(qkv, slopes):
    b, _, _, s, _ = qkv.shape
    blk = MOBA_BLOCK
    nb = s // blk
    hd = MOBA_HD
    assert nb + 3 <= hd and s % MOBA_SEL_CHUNK == 0
    groups = [(i, j) for i in range(nb) for j in range(0, i, MOBA_GROUP)]
    own_units = MOBA_STEP_BLOCKS
    grp_units = MOBA_STEP_BLOCKS // MOBA_GROUP
    assert MOBA_STEP_BLOCKS % MOBA_GROUP == 0 and nb % MOBA_GROUP == 0
    assert nb % own_units == 0 and nb // own_units >= 2
    assert len(groups) % grp_units == 0 and len(groups) // grp_units >= 2
    assert nb % MOBA_FINISH_UNROLL == 0
    item_q = jnp.asarray(np.array([g[0] for g in groups], np.int32))
    item_k = jnp.asarray(np.array([g[1] for g in groups], np.int32))
    step_rows = MOBA_STEP_BLOCKS * blk
    vmem = (2 * 5 * s * hd * 2
            + s * hd * 2 + 2 * 2 * s * hd * 2
            + nb * hd * blk * 4
            + step_rows * blk * 6
            + 2 * step_rows * blk * 4 + (4 << 20))
    smem = pl.BlockSpec(memory_space=pltpu.SMEM)
    return pl.pallas_call(
        _moba_kernel,
        grid=(b, MOBA_HEADS),
        in_specs=[
            smem, smem, smem,
            pl.BlockSpec((None, None, None, s, hd), lambda bi, h: (bi, 0, h, 0, 0)),
            pl.BlockSpec((None, None, None, s, hd), lambda bi, h: (bi, 1, h, 0, 0)),
            pl.BlockSpec((None, None, None, s, hd), lambda bi, h: (bi, 2, h, 0, 0)),
            pl.BlockSpec((None, s, hd), lambda bi, h: (h, 0, 0)),
        ],
        out_specs=pl.BlockSpec((None, None, s, hd), lambda bi, h: (bi, h, 0, 0)),
        out_shape=jax.ShapeDtypeStruct((b, MOBA_HEADS, s, hd), BF16),
        scratch_shapes=[
            pltpu.VMEM((hd, s), BF16),
            pltpu.VMEM((2 * hd, s), BF16),
            pltpu.VMEM((s, 2 * hd), BF16),
            pltpu.VMEM((nb, hd), F32),
            pltpu.VMEM((3 * nb, hd), BF16),
            pltpu.VMEM((nb, blk), F32),
            pltpu.VMEM((nb, blk), F32),
            pltpu.VMEM((nb, hd, blk), F32),
            pltpu.VMEM((step_rows, blk), F32),
            pltpu.VMEM((own_units, 1, blk), F32),
            pltpu.VMEM((step_rows, blk), BF16),
            pltpu.VMEM((own_units, 1, blk), F32),
        ],
        compiler_params=pltpu.CompilerParams(
            dimension_semantics=("parallel", "parallel"),
            vmem_limit_bytes=_vmem_limit(vmem)),
        name="moba",
    )(slopes, item_q, item_k, qkv, qkv, qkv, _moba_key_table(slopes, s))


def _merge_kernel(x_ref, oa_ref, ob_ref, ga_ref, gb_ref, wa_ref, wb_ref, wo_ref, o_ref):
    ya = _dot(oa_ref[...], wa_ref[...])
    ob = jnp.concatenate([ob_ref[h] for h in range(MOBA_HEADS)], axis=1)
    yb = _dot(ob, wb_ref[...])
    mix = (jax.nn.sigmoid(ga_ref[...].astype(F32)) * ya
           + jax.nn.sigmoid(gb_ref[...].astype(F32)) * yb)
    o_ref[...] = x_ref[...] + _dot(mix.astype(BF16), wo_ref[...])


def _merge(x, oa, ob, proj, wa, wb, wo):
    t = x.shape[0]
    tm = MERGE_TM
    rpb = ob.shape[2] // tm
    const = dict(pipeline_mode=pl.Buffered(1))
    vmem = (2 * 2 * tm * D_MODEL * 4 + 2 * 2 * tm * GLA_V_W * 2 + 2 * 2 * tm * D_MODEL * 2
            + (2 * GLA_V_W + D_MODEL) * D_MODEL * 2 + 4 * tm * D_MODEL * 4 + (4 << 20))
    return pl.pallas_call(
        _merge_kernel,
        grid=(t // tm,),
        in_specs=[
            pl.BlockSpec((tm, D_MODEL), lambda i: (i, 0)),
            pl.BlockSpec((tm, GLA_V_W), lambda i: (i, 0)),
            pl.BlockSpec((None, MOBA_HEADS, tm, MOBA_HD), lambda i: (i // rpb, 0, i % rpb, 0)),
            pl.BlockSpec((tm, D_MODEL), lambda i: (i, COL_GA // D_MODEL)),
            pl.BlockSpec((tm, D_MODEL), lambda i: (i, COL_GB // D_MODEL)),
            pl.BlockSpec((GLA_V_W, D_MODEL), lambda i: (0, 0), **const),
            pl.BlockSpec((MOBA_W, D_MODEL), lambda i: (0, 0), **const),
            pl.BlockSpec((D_MODEL, D_MODEL), lambda i: (0, 0), **const),
        ],
        out_specs=pl.BlockSpec((tm, D_MODEL), lambda i: (i, 0)),
        out_shape=jax.ShapeDtypeStruct((t, D_MODEL), F32),
        compiler_params=pltpu.CompilerParams(
            dimension_semantics=("parallel",),
            vmem_limit_bytes=_vmem_limit(vmem)),
        name="merge",
    )(x, oa, ob, proj, proj, wa, wb, wo)


def kernel(x, ffn1_norm, ffn1_w_gate, ffn1_w_up, ffn1_w_down, mix_norm, w_in, gla_w_alpha,
           gla_b_alpha, gla_out_norm, w_branch_gla, w_branch_moba, w_out, ffn2_norm,
           ffn2_w_gate, ffn2_w_up, ffn2_w_down, final_norm):
    bsz, seq, d = x.shape
    t = bsz * seq
    depth = ffn1_norm.shape[0]
    assert depth >= 1 and d == D_MODEL and seq % GLA_CHUNK == 0 and seq % MOBA_BLOCK == 0
    assert t % FFN_TM == 0 and seq % PROJ_TM == 0 and seq % MERGE_TM == 0
    assert OUT_W % PROJ_TN == 0
    assert PROJ_TM % LA_ROWS == 0 and PROJ_TM // LA_ROWS <= PROJ_W // PROJ_TN

    slopes = jnp.exp2(-8.0 * jnp.arange(1, MOBA_HEADS + 1, dtype=F32) / MOBA_HEADS)
    col_scale = jnp.ones((1, PROJ_W), F32)
    col_scale = col_scale.at[:, HEAD_COLS + COL_GQ:HEAD_COLS + COL_GQ + GLA_QK_W].set(GLA_DK ** -0.5)
    col_scale = col_scale.at[:, :MOBA_W].set(MOBA_HD ** -0.5 * LOG2E)
    final_g = final_norm.reshape(1, D_MODEL)

    xf = x.reshape(t, D_MODEL)
    for l in range(depth):
        last = l == depth - 1
        w_l = w_in[l]
        w_b = w_l.astype(BF16)
        w_main = jnp.concatenate([w_b[:, slice(*W_IN_MOBA)], w_b[:, slice(*W_IN_GATES)],
                                  w_b[:, slice(*W_IN_GLA)]], axis=1)
        w_lr = jnp.pad(jnp.tile(w_l[:, slice(*W_IN_LR)], (1, 3)),
                       ((0, 0), (0, LANES_V7X - 3 * GLA_RANK))).astype(BF16)
        wa_hi = _trunc_bf16(gla_w_alpha[l])
        wa_mid = _trunc_bf16(gla_w_alpha[l] - wa_hi)
        w_alpha = jnp.pad(jnp.concatenate([wa_hi, wa_mid, wa_hi], axis=0),
                          ((0, LANES_V7X - 3 * GLA_RANK), (0, 0))).astype(BF16)

        xf = _ffn(xf, ffn1_norm[l].reshape(1, D_MODEL), ffn1_w_gate[l].astype(BF16),
                  ffn1_w_up[l].astype(BF16), ffn1_w_down[l].astype(BF16), final_g, False)
        proj, qkv, log_a = _in_proj(xf, mix_norm[l].reshape(1, D_MODEL), w_main, col_scale,
                                    w_lr, w_alpha, gla_b_alpha[l].reshape(1, GLA_QK_W), bsz)
        proj3 = proj.reshape(bsz, seq, OUT_W)
        oa = _gla(proj3, log_a.reshape(bsz, seq, GLA_QK_W), gla_out_norm[l].reshape(1, GLA_DV))
        ob = _moba(qkv, slopes)
        xf = _merge(xf, oa.reshape(t, GLA_V_W), ob, proj,
                    w_branch_gla[l].astype(BF16), w_branch_moba[l].astype(BF16),
                    w_out[l].astype(BF16))
        xf = _ffn(xf, ffn2_norm[l].reshape(1, D_MODEL), ffn2_w_gate[l].astype(BF16),
                  ffn2_w_up[l].astype(BF16), ffn2_w_down[l].astype(BF16), final_g, last)
    return xf.reshape(bsz, seq, D_MODEL)
```

```python
import functools

import jax
import jax.numpy as jnp
import numpy as np
from jax import lax
from jax.experimental import pallas as pl
from jax.experimental.pallas import tpu as pltpu

F32 = jnp.float32
BF16 = jnp.bfloat16

D_MODEL = 2048
D_FF = 5632
GLA_HEADS = 4
GLA_DK = 128
GLA_DV = 256
GLA_RANK = 16
GLA_TAU = 16.0
MOBA_HEADS = 8
MOBA_HD = 128
MOBA_BLOCK = 256
MOBA_TOPK = 3
NORM_EPS = 1e-6

GLA_QK_W = GLA_HEADS * GLA_DK
GLA_V_W = GLA_HEADS * GLA_DV
MOBA_W = MOBA_HEADS * MOBA_HD

HEAD_COLS = 3 * MOBA_W
COL_GA = 0
COL_GB = COL_GA + D_MODEL
COL_GQ = COL_GB + D_MODEL
COL_GK = COL_GQ + GLA_QK_W
COL_GV = COL_GK + GLA_QK_W
COL_GR = COL_GV + GLA_V_W
OUT_W = COL_GR + GLA_V_W
PROJ_W = HEAD_COLS + OUT_W
W_IN_GLA = (0, 2 * GLA_QK_W + 2 * GLA_V_W)
W_IN_LR = (W_IN_GLA[1], W_IN_GLA[1] + GLA_RANK)
W_IN_MOBA = (W_IN_LR[1], W_IN_LR[1] + 3 * MOBA_W)
W_IN_GATES = (W_IN_MOBA[1], W_IN_MOBA[1] + 2 * D_MODEL)

LANES_V7X = 128
SUBLANES_V7X = 8
VMEM_BYTES_V7X = 64 << 20
VMEM_CAP = VMEM_BYTES_V7X - (6 << 20)

FFN_TM = 1024
FFN_TF = 512
PROJ_TM = 1024
PROJ_TN = MOBA_W
HEAD_TILES = HEAD_COLS // PROJ_TN
LA_ROWS = 128
GLA_CHUNK = 256
GLA_BASE = 2 * SUBLANES_V7X
MERGE_TM = 512
NORM_ROWS = 256
AHEAD_ROWS = 128

MASK_VALUE = -1e30
MOBA_STEP_BLOCKS = 8
MOBA_GROUP = 4
MOBA_SEL_CHUNK = 1024
MOBA_FINISH_UNROLL = 4
LOG2E = 1.4426950408889634


def _vmem_limit(nbytes):
    return int(min(VMEM_CAP, nbytes))


def _dot(a, b):
    return jnp.dot(a, b, preferred_element_type=F32)


def _dot_nt(a, b, precision=None):
    return lax.dot_general(a, b, (((1,), (1,)), ((), ())),
                           preferred_element_type=F32, precision=precision)


def _dot_tn(a, b):
    return lax.dot_general(a, b, (((0,), (0,)), ((), ())), preferred_element_type=F32)


def _rms_norm_slice(x_ref, g_ref, out_ref, sl):
    x = x_ref[sl, :].astype(F32)
    ms = jnp.mean(x * x, axis=-1, keepdims=True)
    out_ref[sl, :] = (x * lax.rsqrt(ms + NORM_EPS) * g_ref[...]).astype(out_ref.dtype)


def _rms_norm_rows(x_ref, g_ref, out_ref, rows):
    def body(r, carry):
        _rms_norm_slice(x_ref, g_ref, out_ref,
                        pl.ds(pl.multiple_of(r * NORM_ROWS, NORM_ROWS), NORM_ROWS))
        return carry
    lax.fori_loop(0, rows // NORM_ROWS, body, 0)


def _next_row_index(i, j, n_rows):
    return jnp.minimum(i + jnp.minimum(j, 1), n_rows - 1)


def _with_norm_ahead(x_ref, g_ref, xn_even_ref, xn_odd_ref, rows, step):
    i = pl.program_id(0)
    j = pl.program_id(1)

    @pl.when(jnp.logical_and(i == 0, j == 0))
    def _():
        _rms_norm_rows(x_ref, g_ref, xn_even_ref, rows)

    r = jnp.clip(j - 1, 0, rows // AHEAD_ROWS - 1)
    sl = pl.ds(pl.multiple_of(r * AHEAD_ROWS, AHEAD_ROWS), AHEAD_ROWS)
    for parity, cur, nxt in ((0, xn_even_ref, xn_odd_ref), (1, xn_odd_ref, xn_even_ref)):
        @pl.when(i % 2 == parity)
        def _(cur=cur, nxt=nxt):
            step(cur, functools.partial(_rms_norm_slice, x_ref, g_ref, nxt, sl))


def _ffn_kernel(x_ref, g_ref, wg_ref, wu_ref, wd_ref, fg_ref, o_ref, xn_ref, *, final_norm):
    j = pl.program_id(1)

    @pl.when(j == 0)
    def _():
        _rms_norm_rows(x_ref, g_ref, xn_ref, FFN_TM)
        o_ref[...] = x_ref[...]

    xn = xn_ref[...]
    hg = _dot(xn, wg_ref[...])
    hu = _dot(xn, wu_ref[...])
    a = (0.5 * hg * jax.nn.sigmoid(hg)) * hu
    o_ref[...] += _dot(a.astype(BF16), wd_ref[...])

    if final_norm:
        @pl.when(j == pl.num_programs(1) - 1)
        def _():
            _rms_norm_rows(o_ref, fg_ref, o_ref, FFN_TM)


def _ffn(x, norm_g, wg, wu, wd, final_g, final_norm):
    t = x.shape[0]
    vmem = (2 * 2 * FFN_TM * D_MODEL * 4
            + FFN_TM * D_MODEL * 2
            + 2 * 3 * D_MODEL * FFN_TF * 2
            + 4 * FFN_TM * FFN_TF * 4
            + (4 << 20))
    return pl.pallas_call(
        functools.partial(_ffn_kernel, final_norm=final_norm),
        grid=(t // FFN_TM, D_FF // FFN_TF),
        in_specs=[
            pl.BlockSpec((FFN_TM, D_MODEL), lambda i, j: (i, 0)),
            pl.BlockSpec((1, D_MODEL), lambda i, j: (0, 0)),
            pl.BlockSpec((D_MODEL, FFN_TF), lambda i, j: (0, j)),
            pl.BlockSpec((D_MODEL, FFN_TF), lambda i, j: (0, j)),
            pl.BlockSpec((FFN_TF, D_MODEL), lambda i, j: (j, 0)),
            pl.BlockSpec((1, D_MODEL), lambda i, j: (0, 0)),
        ],
        out_specs=pl.BlockSpec((FFN_TM, D_MODEL), lambda i, j: (i, 0)),
        out_shape=jax.ShapeDtypeStruct((t, D_MODEL), F32),
        scratch_shapes=[pltpu.VMEM((FFN_TM, D_MODEL), BF16)],
        compiler_params=pltpu.CompilerParams(
            dimension_semantics=("parallel", "arbitrary"),
            vmem_limit_bytes=_vmem_limit(vmem)),
        name="ffn_final" if final_norm else "ffn",
    )(x, norm_g, wg, wu, wd, final_g)


def _log_sigmoid(z):
    return jnp.minimum(z, 0.0) - jnp.log1p(jnp.exp(-jnp.abs(z)))


def _in_proj_kernel(x_ref, g_ref, w_ref, cs_ref, wlr_ref, wa_ref, ba_ref,
                    proj_ref, heads_ref, la_ref, xn_even_ref, xn_odd_ref, lr_ref):
    j = pl.program_id(1)

    def step(xn_ref, norm_next):
        @pl.when(j == 0)
        def _():
            lr_ref[...] = _dot(xn_ref[...], wlr_ref[...])

        norm_next()

        rows = pl.ds(pl.multiple_of(jnp.minimum(j, PROJ_TM // LA_ROWS - 1) * LA_ROWS, LA_ROWS),
                     LA_ROWS)
        lr = lr_ref[rows, :]
        lr_hi = lr.astype(BF16).astype(F32)
        lane = lax.broadcasted_iota(jnp.int32, lr.shape, 1)
        packed = jnp.where(lane < 2 * GLA_RANK, lr_hi, lr - lr_hi).astype(BF16)
        z = _dot(packed, wa_ref[...]) + ba_ref[...]
        la_ref[rows, :] = _log_sigmoid(z) * (1.0 / GLA_TAU)

        res = (_dot(xn_ref[...], w_ref[...]) * cs_ref[...]).astype(BF16)
        proj_ref[...] = res
        for h in range(MOBA_HEADS):
            heads_ref[h] = res[:, h * MOBA_HD:(h + 1) * MOBA_HD]

    _with_norm_ahead(x_ref, g_ref, xn_even_ref, xn_odd_ref, PROJ_TM, step)


def _in_proj(x, norm_g, w_main, col_scale, w_lr, w_alpha, b_alpha, bsz):
    t = x.shape[0]
    rpb = t // bsz // PROJ_TM
    assert PROJ_W // PROJ_TN > PROJ_TM // AHEAD_ROWS and 3 * GLA_RANK <= LANES_V7X
    vmem = (2 * PROJ_TM * D_MODEL * 4 + 2 * PROJ_TM * D_MODEL * 2
            + 2 * D_MODEL * PROJ_TN * 2 + 2 * 2 * PROJ_TM * PROJ_TN * 2
            + 2 * PROJ_TM * GLA_QK_W * 4 + 2 * PROJ_TM * PROJ_TN * 4
            + 2 * D_MODEL * LANES_V7X * 2 + PROJ_TM * LANES_V7X * 4 + (4 << 20))
    return pl.pallas_call(
        _in_proj_kernel,
        grid=(t // PROJ_TM, PROJ_W // PROJ_TN),
        in_specs=[
            pl.BlockSpec((PROJ_TM, D_MODEL), lambda i, j: (_next_row_index(i, j, t // PROJ_TM), 0)),
            pl.BlockSpec((1, D_MODEL), lambda i, j: (0, 0)),
            pl.BlockSpec((D_MODEL, PROJ_TN), lambda i, j: (0, j)),
            pl.BlockSpec((1, PROJ_TN), lambda i, j: (0, j)),
            pl.BlockSpec((D_MODEL, LANES_V7X), lambda i, j: (0, 0)),
            pl.BlockSpec((LANES_V7X, GLA_QK_W), lambda i, j: (0, 0)),
            pl.BlockSpec((1, GLA_QK_W), lambda i, j: (0, 0)),
        ],
        out_specs=[
            pl.BlockSpec((PROJ_TM, PROJ_TN), lambda i, j: (i, jnp.maximum(j - HEAD_TILES, 0))),
            pl.BlockSpec((None, None, MOBA_HEADS, PROJ_TM, MOBA_HD),
                         lambda i, j: (i // rpb, jnp.minimum(j, HEAD_TILES), 0, i % rpb, 0)),
            pl.BlockSpec((PROJ_TM, GLA_QK_W), lambda i, j: (i, 0)),
        ],
        out_shape=[
            jax.ShapeDtypeStruct((t, OUT_W), BF16),
            jax.ShapeDtypeStruct((bsz, HEAD_TILES + 1, MOBA_HEADS, t // bsz, MOBA_HD), BF16),
            jax.ShapeDtypeStruct((t, GLA_QK_W), F32),
        ],
        scratch_shapes=[pltpu.VMEM((PROJ_TM, D_MODEL), BF16),
                        pltpu.VMEM((PROJ_TM, D_MODEL), BF16),
                        pltpu.VMEM((PROJ_TM, LANES_V7X), F32)],
        compiler_params=pltpu.CompilerParams(
            dimension_semantics=("arbitrary", "arbitrary"),
            vmem_limit_bytes=_vmem_limit(vmem)),
        name="in_proj",
    )(x, norm_g, w_main, col_scale, w_lr, w_alpha, b_alpha)


def _bcast_rows(v, s, c):
    return jnp.broadcast_to(v, (c // s, s, GLA_DK)).reshape(c, GLA_DK)


def _gla_head(q, k, v, la, gr, gn, st_ref):
    c = q.shape[0]
    row = lax.broadcasted_iota(jnp.int32, (c, 1), 0)
    ri = lax.broadcasted_iota(jnp.int32, (c, c), 0)
    ci = lax.broadcasted_iota(jnp.int32, (c, c), 1)
    rxc = ri ^ ci

    def next_level(cs, s):
        ends = cs.reshape(c // s, s, GLA_DK)[:, s - 1:s, :]
        prev = jnp.concatenate([jnp.zeros_like(ends[:1]), ends[:-1]], axis=0)
        odd = ((row // s) & 1) == 1
        return ends, odd, cs + jnp.where(odd, _bcast_rows(prev, s, c), 0.0)

    cs = la.reshape(c // SUBLANES_V7X, SUBLANES_V7X, GLA_DK)
    sub = lax.broadcasted_iota(jnp.int32, cs.shape, 1)
    sh = 1
    while sh < SUBLANES_V7X:
        cs = cs + jnp.where(sub >= sh, pltpu.roll(cs, sh, axis=1), 0.0)
        sh *= 2
    cs = cs.reshape(c, GLA_DK)
    s = SUBLANES_V7X
    while s < GLA_BASE:
        _, _, cs = next_level(cs, s)
        s *= 2

    p = _dot_nt((q * jnp.exp(cs)).astype(BF16), (k * jnp.exp(-cs)).astype(BF16))
    attn = jnp.where(ci <= ri, p, 0.0)
    while s < c:
        ends, odd, cs_next = next_level(cs, s)
        qe = jnp.where(odd, jnp.exp(cs), 0.0)
        ke = jnp.where(odd, 0.0, jnp.exp(_bcast_rows(ends, s, c) - cs))
        p = _dot_nt((q * qe).astype(BF16), (k * ke).astype(BF16))
        attn = jnp.where(rxc < s, attn, p)
        cs = cs_next
        s *= 2

    b_last = cs[c - 1:c, :]
    st = st_ref[...]
    o = _dot_nt((q * jnp.exp(cs)).astype(BF16), st.astype(BF16))
    o = o + _dot(attn.astype(BF16), v)
    kd = (k * jnp.exp(b_last - cs)).astype(BF16)
    st_ref[...] = st * jnp.exp(b_last) + _dot_tn(v, kd)

    ms = jnp.mean(o * o, axis=-1, keepdims=True)
    y = o * lax.rsqrt(ms + NORM_EPS) * gn
    g = gr.astype(F32)
    return y * (g * jax.nn.sigmoid(g))


def _gla_kernel(q_ref, k_ref, v_ref, gr_ref, la_ref, gn_ref, o_ref, st_ref):
    @pl.when(pl.program_id(1) == 0)
    def _():
        st_ref[...] = jnp.zeros_like(st_ref)

    gn = gn_ref[...]
    for h in range(GLA_HEADS):
        ks = slice(h * GLA_DK, (h + 1) * GLA_DK)
        vs = slice(h * GLA_DV, (h + 1) * GLA_DV)
        out = _gla_head(q_ref[:, ks].astype(F32), k_ref[:, ks].astype(F32), v_ref[:, vs],
                        la_ref[:, ks], gr_ref[:, vs], gn, st_ref.at[h])
        o_ref[:, vs] = out.astype(o_ref.dtype)


def _gla(proj3, la3, gn):
    b, s, _ = proj3.shape
    c = GLA_CHUNK
    return pl.pallas_call(
        _gla_kernel,
        grid=(b, s // c),
        in_specs=[
            pl.BlockSpec((None, c, GLA_QK_W), lambda i, j: (i, j, COL_GQ // GLA_QK_W)),
            pl.BlockSpec((None, c, GLA_QK_W), lambda i, j: (i, j, COL_GK // GLA_QK_W)),
            pl.BlockSpec((None, c, GLA_V_W), lambda i, j: (i, j, COL_GV // GLA_V_W)),
            pl.BlockSpec((None, c, GLA_V_W), lambda i, j: (i, j, COL_GR // GLA_V_W)),
            pl.BlockSpec((None, c, GLA_QK_W), lambda i, j: (i, j, 0)),
            pl.BlockSpec((1, GLA_DV), lambda i, j: (0, 0)),
        ],
        out_specs=pl.BlockSpec((None, c, GLA_V_W), lambda i, j: (i, j, 0)),
        out_shape=jax.ShapeDtypeStruct((b, s, GLA_V_W), BF16),
        scratch_shapes=[pltpu.VMEM((GLA_HEADS, GLA_DV, GLA_DK), F32)],
        compiler_params=pltpu.CompilerParams(
            dimension_semantics=("parallel", "arbitrary"),
            vmem_limit_bytes=_vmem_limit(32 << 20)),
        name="gla",
    )(proj3, proj3, proj3, proj3, la3, gn)


def _split3_bf16(z):
    hi = z.astype(BF16).astype(F32)
    r = z - hi
    mid = r.astype(BF16).astype(F32)
    lo = (r - mid).astype(BF16).astype(F32)
    return hi, mid, lo


def _trunc_bf16(x):
    bits = lax.bitcast_convert_type(x, jnp.uint32) & jnp.uint32(0xFFFF0000)
    return lax.bitcast_convert_type(bits, F32)


def _moba_key_table(slopes, seq):
    nb = seq // MOBA_BLOCK
    pos = jnp.arange(seq)
    z = (slopes * LOG2E)[:, None] * pos.astype(F32)[None, :]
    hi = _trunc_bf16(z)
    mid = _trunc_bf16(z - hi)
    lo = _trunc_bf16(z - hi - mid)
    lane = jnp.arange(MOBA_HD)
    onehot = (lane[None, :] == (pos // MOBA_BLOCK)[:, None]).astype(F32)
    table = (onehot[None] + hi[..., None] * (lane == nb) + mid[..., None] * (lane == nb + 1)
             + lo[..., None] * (lane == nb + 2))
    return table.astype(BF16)


def _moba_kernel(slope_ref, item_q_ref, item_k_ref, q_ref, k_ref, v_ref, kx_ref, o_ref,
                 vt_ref, qt_ref, ka_ref, kmean_ref, kms_ref, m_ref, l_ref, acc_ref,
                 s_ref, smax_ref, p_ref, al_ref):
    h = pl.program_id(1)
    blk = MOBA_BLOCK
    hd = MOBA_HD
    seq = k_ref.shape[0]
    nb = seq // blk
    slope2 = slope_ref[h] * LOG2E

    def prep(j, carry):
        sl = pl.ds(pl.multiple_of(j * blk, blk), blk)
        kj = k_ref[sl, :]
        vt_ref[:, sl] = v_ref[sl, :].T
        qt_ref[0:hd, sl] = q_ref[sl, :].T
        ka_ref[sl, 0:hd] = kj
        ka_ref[sl, hd:2 * hd] = kx_ref[sl, :]
        kmean_ref[pl.ds(j, 1), :] = jnp.mean(kj.astype(F32), axis=0, keepdims=True)
        return carry
    lax.fori_loop(0, nb, prep, 0)
    qt_ref[hd + nb:2 * hd, :] = jnp.ones((hd - nb, seq), BF16)
    kms_ref[...] = jnp.concatenate(_split3_bf16(kmean_ref[...]), axis=0).astype(BF16)

    chunk = MOBA_SEL_CHUNK
    for c in range(seq // chunk):
        live = min(nb, -(-((c + 1) * (chunk // blk) - 1) // SUBLANES_V7X) * SUBLANES_V7X)
        live = max(live, SUBLANES_V7X)
        cs = slice(c * chunk, (c + 1) * chunk)
        g3 = _dot(kms_ref[...], qt_ref[0:hd, cs])
        gate = g3[0:live] + g3[nb:nb + live] + g3[2 * nb:2 * nb + live]
        bidx = lax.broadcasted_iota(jnp.int32, gate.shape, 0)
        qblk = c * (chunk // blk) + lax.broadcasted_iota(jnp.int32, gate.shape, 1) // blk
        gate = jnp.where(bidx < qblk, gate, -jnp.inf)
        mask = jnp.full(gate.shape, MASK_VALUE, F32)
        for _ in range(MOBA_TOPK):
            m = jnp.max(gate, axis=0, keepdims=True)
            first = jnp.min(jnp.where(gate == m, bidx, nb), axis=0, keepdims=True)
            pick = bidx == jnp.where(m > -jnp.inf, first, -1)
            mask = jnp.where(pick, 0.0, mask)
            gate = jnp.where(pick, -jnp.inf, gate)
        if live < nb:
            mask = jnp.concatenate([mask, jnp.full((nb - live, chunk), MASK_VALUE, F32)], axis=0)
        qt_ref[hd:hd + nb, cs] = mask.astype(BF16)

    m_ref[...] = jnp.full(m_ref.shape, MASK_VALUE, F32)
    l_ref[...] = jnp.zeros(l_ref.shape, F32)

    def clear(i, carry):
        acc_ref[i] = jnp.zeros((hd, blk), F32)
        return carry
    lax.fori_loop(0, nb, clear, 0)

    kpos = lax.broadcasted_iota(jnp.int32, (blk, blk), 0)
    qpos = lax.broadcasted_iota(jnp.int32, (blk, blk), 1)
    step_rows = s_ref.shape[0]

    def make_stages(own_block):
        rows = blk if own_block else MOBA_GROUP * blk
        units = step_rows // rows

        def item(n):
            if own_block:
                i = j = n
            else:
                i = item_q_ref[n]
                j = item_k_ref[n]
            return (i, pl.ds(pl.multiple_of(i * blk, blk), blk),
                    pl.ds(pl.multiple_of(j * blk, blk), rows))

        def scores(n0):
            for u in range(units):
                i, qs, ks = item(n0 + u)
                if own_block:
                    bias = (kpos + i * blk).astype(F32) * slope2
                    s = _dot(k_ref[ks, :], qt_ref[0:hd, qs]) + bias
                    s = jnp.where(kpos <= qpos, s, MASK_VALUE)
                else:
                    s = _dot(ka_ref[ks, :], qt_ref[:, qs])
                s_ref[u * rows:(u + 1) * rows, :] = s
                smax_ref[u] = jnp.max(s, axis=0, keepdims=True)

        def softmax(n0):
            for u in range(units):
                i, _, _ = item(n0 + u)
                row = pl.ds(i, 1)
                m_old = m_ref[row, :]
                m_new = jnp.maximum(m_old, smax_ref[u])
                alpha = jnp.exp2(m_old - m_new)
                p = jnp.exp2(s_ref[u * rows:(u + 1) * rows, :] - m_new)
                l_ref[row, :] = alpha * l_ref[row, :] + jnp.sum(p, axis=0, keepdims=True)
                m_ref[row, :] = m_new
                p_ref[u * rows:(u + 1) * rows, :] = p.astype(BF16)
                al_ref[u] = alpha

        def values(n0):
            for u in range(units):
                i, _, ks = item(n0 + u)
                acc_ref[i] = (al_ref[u] * acc_ref[i]
                              + _dot(vt_ref[:, ks], p_ref[u * rows:(u + 1) * rows, :]))

        return units, scores, softmax, values

    def run_pipeline(n_items, stages):
        units, scores, softmax, values = stages
        steps = n_items // units
        scores(0)
        softmax(0)
        scores(units)

        def body(t, carry):
            values(t * units)
            softmax((t + 1) * units)
            scores((t + 2) * units)
            return carry
        lax.fori_loop(0, steps - 2, body, 0)
        values((steps - 2) * units)
        softmax((steps - 1) * units)
        values((steps - 1) * units)

    run_pipeline(nb, make_stages(True))
    run_pipeline(item_q_ref.shape[0], make_stages(False))

    def finish(i0, carry):
        for u in range(MOBA_FINISH_UNROLL):
            i = i0 * MOBA_FINISH_UNROLL + u
            qs = pl.ds(pl.multiple_of(i * blk, blk), blk)
            inv = 1.0 / l_ref[pl.ds(i, 1), :]
            o_ref[qs, :] = (acc_ref[i] * inv).T.astype(o_ref.dtype)
        return carry
    lax.fori_loop(0, nb // MOBA_FINISH_UNROLL, finish, 0)


def _moba(qkv, slopes):
    b, _, _, s, _ = qkv.shape
    blk = MOBA_BLOCK
    nb = s // blk
    hd = MOBA_HD
    assert nb + 3 <= hd and s % MOBA_SEL_CHUNK == 0
    groups = [(i, j) for i in range(nb) for j in range(0, i, MOBA_GROUP)]
    own_units = MOBA_STEP_BLOCKS
    grp_units = MOBA_STEP_BLOCKS // MOBA_GROUP
    assert MOBA_STEP_BLOCKS % MOBA_GROUP == 0 and nb % MOBA_GROUP == 0
    assert nb % own_units == 0 and nb // own_units >= 2
    assert len(groups) % grp_units == 0 and len(groups) // grp_units >= 2
    assert nb % MOBA_FINISH_UNROLL == 0
    item_q = jnp.asarray(np.array([g[0] for g in groups], np.int32))
    item_k = jnp.asarray(np.array([g[1] for g in groups], np.int32))
    step_rows = MOBA_STEP_BLOCKS * blk
    vmem = (2 * 5 * s * hd * 2
            + s * hd * 2 + 2 * 2 * s * hd * 2
            + nb * hd * blk * 4
            + step_rows * blk * 6
            + 2 * step_rows * blk * 4 + (4 << 20))
    smem = pl.BlockSpec(memory_space=pltpu.SMEM)
    return pl.pallas_call(
        _moba_kernel,
        grid=(b, MOBA_HEADS),
        in_specs=[
            smem, smem, smem,
            pl.BlockSpec((None, None, None, s, hd), lambda bi, h: (bi, 0, h, 0, 0)),
            pl.BlockSpec((None, None, None, s, hd), lambda bi, h: (bi, 1, h, 0, 0)),
            pl.BlockSpec((None, None, None, s, hd), lambda bi, h: (bi, 2, h, 0, 0)),
            pl.BlockSpec((None, s, hd), lambda bi, h: (h, 0, 0)),
        ],
        out_specs=pl.BlockSpec((None, None, s, hd), lambda bi, h: (bi, h, 0, 0)),
        out_shape=jax.ShapeDtypeStruct((b, MOBA_HEADS, s, hd), BF16),
        scratch_shapes=[
            pltpu.VMEM((hd, s), BF16),
            pltpu.VMEM((2 * hd, s), BF16),
            pltpu.VMEM((s, 2 * hd), BF16),
            pltpu.VMEM((nb, hd), F32),
            pltpu.VMEM((3 * nb, hd), BF16),
            pltpu.VMEM((nb, blk), F32),
            pltpu.VMEM((nb, blk), F32),
            pltpu.VMEM((nb, hd, blk), F32),
            pltpu.VMEM((step_rows, blk), F32),
            pltpu.VMEM((own_units, 1, blk), F32),
            pltpu.VMEM((step_rows, blk), BF16),
            pltpu.VMEM((own_units, 1, blk), F32),
        ],
        compiler_params=pltpu.CompilerParams(
            dimension_semantics=("parallel", "parallel"),
            vmem_limit_bytes=_vmem_limit(vmem)),
        name="moba",
    )(slopes, item_q, item_k, qkv, qkv, qkv, _moba_key_table(slopes, s))


def _merge_kernel(x_ref, oa_ref, ob_ref, ga_ref, gb_ref, wa_ref, wb_ref, wo_ref, o_ref):
    ya = _dot(oa_ref[...], wa_ref[...])
    ob = jnp.concatenate([ob_ref[h] for h in range(MOBA_HEADS)], axis=1)
    yb = _dot(ob, wb_ref[...])
    mix = (jax.nn.sigmoid(ga_ref[...].astype(F32)) * ya
           + jax.nn.sigmoid(gb_ref[...].astype(F32)) * yb)
    o_ref[...] = x_ref[...] + _dot(mix.astype(BF16), wo_ref[...])


def _merge(x, oa, ob, proj, wa, wb, wo):
    t = x.shape[0]
    tm = MERGE_TM
    rpb = ob.shape[2] // tm
    const = dict(pipeline_mode=pl.Buffered(1))
    vmem = (2 * 2 * tm * D_MODEL * 4 + 2 * 2 * tm * GLA_V_W * 2 + 2 * 2 * tm * D_MODEL * 2
            + (2 * GLA_V_W + D_MODEL) * D_MODEL * 2 + 4 * tm * D_MODEL * 4 + (4 << 20))
    return pl.pallas_call(
        _merge_kernel,
        grid=(t // tm,),
        in_specs=[
            pl.BlockSpec((tm, D_MODEL), lambda i: (i, 0)),
            pl.BlockSpec((tm, GLA_V_W), lambda i: (i, 0)),
            pl.BlockSpec((None, MOBA_HEADS, tm, MOBA_HD), lambda i: (i // rpb, 0, i % rpb, 0)),
            pl.BlockSpec((tm, D_MODEL), lambda i: (i, COL_GA // D_MODEL)),
            pl.BlockSpec((tm, D_MODEL), lambda i: (i, COL_GB // D_MODEL)),
            pl.BlockSpec((GLA_V_W, D_MODEL), lambda i: (0, 0), **const),
            pl.BlockSpec((MOBA_W, D_MODEL), lambda i: (0, 0), **const),
            pl.BlockSpec((D_MODEL, D_MODEL), lambda i: (0, 0), **const),
        ],
        out_specs=pl.BlockSpec((tm, D_MODEL), lambda i: (i, 0)),
        out_shape=jax.ShapeDtypeStruct((t, D_MODEL), F32),
        compiler_params=pltpu.CompilerParams(
            dimension_semantics=("parallel",),
            vmem_limit_bytes=_vmem_limit(vmem)),
        name="merge",
    )(x, oa, ob, proj, proj, wa, wb, wo)


def kernel(x, ffn1_norm, ffn1_w_gate, ffn1_w_up, ffn1_w_down, mix_norm, w_in, gla_w_alpha,
           gla_b_alpha, gla_out_norm, w_branch_gla, w_branch_moba, w_out, ffn2_norm,
           ffn2_w_gate, ffn2_w_up, ffn2_w_down, final_norm):
    bsz, seq, d = x.shape
    t = bsz * seq
    depth = ffn1_norm.shape[0]
    assert depth >= 1 and d == D_MODEL and seq % GLA_CHUNK == 0 and seq % MOBA_BLOCK == 0
    assert t % FFN_TM == 0 and seq % PROJ_TM == 0 and seq % MERGE_TM == 0
    assert OUT_W % PROJ_TN == 0
    assert PROJ_TM % LA_ROWS == 0 and PROJ_TM // LA_ROWS <= PROJ_W // PROJ_TN

    slopes = jnp.exp2(-8.0 * jnp.arange(1, MOBA_HEADS + 1, dtype=F32) / MOBA_HEADS)
    col_scale = jnp.ones((1, PROJ_W), F32)
    col_scale = col_scale.at[:, HEAD_COLS + COL_GQ:HEAD_COLS + COL_GQ + GLA_QK_W].set(GLA_DK ** -0.5)
    col_scale = col_scale.at[:, :MOBA_W].set(MOBA_HD ** -0.5 * LOG2E)
    final_g = final_norm.reshape(1, D_MODEL)

    xf = x.reshape(t, D_MODEL)
    for l in range(depth):
        last = l == depth - 1
        w_l = w_in[l]
        w_b = w_l.astype(BF16)
        w_main = jnp.concatenate([w_b[:, slice(*W_IN_MOBA)], w_b[:, slice(*W_IN_GATES)],
                                  w_b[:, slice(*W_IN_GLA)]], axis=1)
        w_lr = jnp.pad(jnp.tile(w_l[:, slice(*W_IN_LR)], (1, 3)),
                       ((0, 0), (0, LANES_V7X - 3 * GLA_RANK))).astype(BF16)
        wa_hi = _trunc_bf16(gla_w_alpha[l])
        wa_mid = _trunc_bf16(gla_w_alpha[l] - wa_hi)
        w_alpha = jnp.pad(jnp.concatenate([wa_hi, wa_mid, wa_hi], axis=0),
                          ((0, LANES_V7X - 3 * GLA_RANK), (0, 0))).astype(BF16)

        xf = _ffn(xf, ffn1_norm[l].reshape(1, D_MODEL), ffn1_w_gate[l].astype(BF16),
                  ffn1_w_up[l].astype(BF16), ffn1_w_down[l].astype(BF16), final_g, False)
        proj, qkv, log_a = _in_proj(xf, mix_norm[l].reshape(1, D_MODEL), w_main, col_scale,
                                    w_lr, w_alpha, gla_b_alpha[l].reshape(1, GLA_QK_W), bsz)
        proj3 = proj.reshape(bsz, seq, OUT_W)
        oa = _gla(proj3, log_a.reshape(bsz, seq, GLA_QK_W), gla_out_norm[l].reshape(1, GLA_DV))
        ob = _moba(qkv, slopes)
        xf = _merge(xf, oa.reshape(t, GLA_V_W), ob, proj,
                    w_branch_gla[l].astype(BF16), w_branch_moba[l].astype(BF16),
                    w_out[l].astype(BF16))
        xf = _ffn(xf, ffn2_norm[l].reshape(1, D_MODEL), ffn2_w_gate[l].astype(BF16),
                  ffn2_w_up[l].astype(BF16), ffn2_w_down[l].astype(BF16), final_g, last)
    return xf.reshape(bsz, seq, D_MODEL)
```

```python
import functools

import jax
import jax.numpy as jnp
import numpy as np
from jax import lax
from jax.experimental import pallas as pl
from jax.experimental.pallas import tpu as pltpu

F32 = jnp.float32
BF16 = jnp.bfloat16

D_MODEL = 2048
D_FF = 5632
GLA_HEADS = 4
GLA_DK = 128
GLA_DV = 256
GLA_RANK = 16
GLA_TAU = 16.0
MOBA_HEADS = 8
MOBA_HD = 128
MOBA_BLOCK = 256
MOBA_TOPK = 3
NORM_EPS = 1e-6

GLA_QK_W = GLA_HEADS * GLA_DK
GLA_V_W = GLA_HEADS * GLA_DV
MOBA_W = MOBA_HEADS * MOBA_HD

HEAD_COLS = 3 * MOBA_W
COL_GA = 0
COL_GB = COL_GA + D_MODEL
COL_GQ = COL_GB + D_MODEL
COL_GK = COL_GQ + GLA_QK_W
COL_GV = COL_GK + GLA_QK_W
COL_GR = COL_GV + GLA_V_W
OUT_W = COL_GR + GLA_V_W
PROJ_W = HEAD_COLS + OUT_W
W_IN_GLA = (0, 2 * GLA_QK_W + 2 * GLA_V_W)
W_IN_LR = (W_IN_GLA[1], W_IN_GLA[1] + GLA_RANK)
W_IN_MOBA = (W_IN_LR[1], W_IN_LR[1] + 3 * MOBA_W)
W_IN_GATES = (W_IN_MOBA[1], W_IN_MOBA[1] + 2 * D_MODEL)

LANES_V7X = 128
SUBLANES_V7X = 8
VMEM_BYTES_V7X = 64 << 20
VMEM_CAP = VMEM_BYTES_V7X - (6 << 20)

FFN_TM = 1024
FFN_TF = 512
PROJ_TM = 1024
PROJ_TN = MOBA_W
HEAD_TILES = HEAD_COLS // PROJ_TN
LA_ROWS = 128
GLA_CHUNK = 256
GLA_BASE = 2 * SUBLANES_V7X
MERGE_TM = 512
NORM_ROWS = 256
AHEAD_ROWS = 128

MASK_VALUE = -1e30
MOBA_STEP_BLOCKS = 16
MOBA_GROUP = 4
MOBA_SEL_CHUNK = 1024
MOBA_FINISH_UNROLL = 4
LOG2E = 1.4426950408889634


def _vmem_limit(nbytes):
    return int(min(VMEM_CAP, nbytes))


def _dot(a, b):
    return jnp.dot(a, b, preferred_element_type=F32)


def _dot_nt(a, b, precision=None):
    return lax.dot_general(a, b, (((1,), (1,)), ((), ())),
                           preferred_element_type=F32, precision=precision)


def _dot_tn(a, b):
    return lax.dot_general(a, b, (((0,), (0,)), ((), ())), preferred_element_type=F32)


def _rms_norm_slice(x_ref, g_ref, out_ref, sl):
    x = x_ref[sl, :].astype(F32)
    ms = jnp.mean(x * x, axis=-1, keepdims=True)
    out_ref[sl, :] = (x * lax.rsqrt(ms + NORM_EPS) * g_ref[...]).astype(out_ref.dtype)


def _rms_norm_rows(x_ref, g_ref, out_ref, rows):
    def body(r, carry):
        _rms_norm_slice(x_ref, g_ref, out_ref,
                        pl.ds(pl.multiple_of(r * NORM_ROWS, NORM_ROWS), NORM_ROWS))
        return carry
    lax.fori_loop(0, rows // NORM_ROWS, body, 0)


def _next_row_index(i, j, n_rows):
    return jnp.minimum(i + jnp.minimum(j, 1), n_rows - 1)


def _with_norm_ahead(x_ref, g_ref, xn_even_ref, xn_odd_ref, rows, step):
    i = pl.program_id(0)
    j = pl.program_id(1)

    @pl.when(jnp.logical_and(i == 0, j == 0))
    def _():
        _rms_norm_rows(x_ref, g_ref, xn_even_ref, rows)

    r = jnp.clip(j - 1, 0, rows // AHEAD_ROWS - 1)
    sl = pl.ds(pl.multiple_of(r * AHEAD_ROWS, AHEAD_ROWS), AHEAD_ROWS)
    for parity, cur, nxt in ((0, xn_even_ref, xn_odd_ref), (1, xn_odd_ref, xn_even_ref)):
        @pl.when(i % 2 == parity)
        def _(cur=cur, nxt=nxt):
            step(cur, functools.partial(_rms_norm_slice, x_ref, g_ref, nxt, sl))


def _ffn_kernel(x_ref, g_ref, wg_ref, wu_ref, wd_ref, fg_ref, o_ref, xn_ref, *, final_norm):
    j = pl.program_id(1)

    @pl.when(j == 0)
    def _():
        _rms_norm_rows(x_ref, g_ref, xn_ref, FFN_TM)
        o_ref[...] = x_ref[...]

    xn = xn_ref[...]
    hg = _dot(xn, wg_ref[...])
    hu = _dot(xn, wu_ref[...])
    a = (0.5 * hg * jax.nn.sigmoid(hg)) * hu
    o_ref[...] += _dot(a.astype(BF16), wd_ref[...])

    if final_norm:
        @pl.when(j == pl.num_programs(1) - 1)
        def _():
            _rms_norm_rows(o_ref, fg_ref, o_ref, FFN_TM)


def _ffn(x, norm_g, wg, wu, wd, final_g, final_norm):
    t = x.shape[0]
    vmem = (2 * 2 * FFN_TM * D_MODEL * 4
            + FFN_TM * D_MODEL * 2
            + 2 * 3 * D_MODEL * FFN_TF * 2
            + 4 * FFN_TM * FFN_TF * 4
            + (4 << 20))
    return pl.pallas_call(
        functools.partial(_ffn_kernel, final_norm=final_norm),
        grid=(t // FFN_TM, D_FF // FFN_TF),
        in_specs=[
            pl.BlockSpec((FFN_TM, D_MODEL), lambda i, j: (i, 0)),
            pl.BlockSpec((1, D_MODEL), lambda i, j: (0, 0)),
            pl.BlockSpec((D_MODEL, FFN_TF), lambda i, j: (0, j)),
            pl.BlockSpec((D_MODEL, FFN_TF), lambda i, j: (0, j)),
            pl.BlockSpec((FFN_TF, D_MODEL), lambda i, j: (j, 0)),
            pl.BlockSpec((1, D_MODEL), lambda i, j: (0, 0)),
        ],
        out_specs=pl.BlockSpec((FFN_TM, D_MODEL), lambda i, j: (i, 0)),
        out_shape=jax.ShapeDtypeStruct((t, D_MODEL), F32),
        scratch_shapes=[pltpu.VMEM((FFN_TM, D_MODEL), BF16)],
        compiler_params=pltpu.CompilerParams(
            dimension_semantics=("parallel", "arbitrary"),
            vmem_limit_bytes=_vmem_limit(vmem)),
        name="ffn_final" if final_norm else "ffn",
    )(x, norm_g, wg, wu, wd, final_g)


def _log_sigmoid(z):
    return jnp.minimum(z, 0.0) - jnp.log1p(jnp.exp(-jnp.abs(z)))


def _in_proj_kernel(x_ref, g_ref, w_ref, cs_ref, wlr_ref, wa_ref, ba_ref,
                    proj_ref, heads_ref, la_ref, xn_even_ref, xn_odd_ref, lr_ref):
    j = pl.program_id(1)

    def step(xn_ref, norm_next):
        @pl.when(j == 0)
        def _():
            lr_ref[...] = _dot(xn_ref[...], wlr_ref[...])

        norm_next()

        rows = pl.ds(pl.multiple_of(jnp.minimum(j, PROJ_TM // LA_ROWS - 1) * LA_ROWS, LA_ROWS),
                     LA_ROWS)
        lr = lr_ref[rows, :]
        lr_hi = lr.astype(BF16).astype(F32)
        lane = lax.broadcasted_iota(jnp.int32, lr.shape, 1)
        packed = jnp.where(lane < 2 * GLA_RANK, lr_hi, lr - lr_hi).astype(BF16)
        z = _dot(packed, wa_ref[...]) + ba_ref[...]
        la_ref[rows, :] = _log_sigmoid(z) * (1.0 / GLA_TAU)

        res = (_dot(xn_ref[...], w_ref[...]) * cs_ref[...]).astype(BF16)
        proj_ref[...] = res
        for h in range(MOBA_HEADS):
            heads_ref[h] = res[:, h * MOBA_HD:(h + 1) * MOBA_HD]

    _with_norm_ahead(x_ref, g_ref, xn_even_ref, xn_odd_ref, PROJ_TM, step)


def _in_proj(x, norm_g, w_main, col_scale, w_lr, w_alpha, b_alpha, bsz):
    t = x.shape[0]
    rpb = t // bsz // PROJ_TM
    assert PROJ_W // PROJ_TN > PROJ_TM // AHEAD_ROWS and 3 * GLA_RANK <= LANES_V7X
    vmem = (2 * PROJ_TM * D_MODEL * 4 + 2 * PROJ_TM * D_MODEL * 2
            + 2 * D_MODEL * PROJ_TN * 2 + 2 * 2 * PROJ_TM * PROJ_TN * 2
            + 2 * PROJ_TM * GLA_QK_W * 4 + 2 * PROJ_TM * PROJ_TN * 4
            + 2 * D_MODEL * LANES_V7X * 2 + PROJ_TM * LANES_V7X * 4 + (4 << 20))
    return pl.pallas_call(
        _in_proj_kernel,
        grid=(t // PROJ_TM, PROJ_W // PROJ_TN),
        in_specs=[
            pl.BlockSpec((PROJ_TM, D_MODEL), lambda i, j: (_next_row_index(i, j, t // PROJ_TM), 0)),
            pl.BlockSpec((1, D_MODEL), lambda i, j: (0, 0)),
            pl.BlockSpec((D_MODEL, PROJ_TN), lambda i, j: (0, j)),
            pl.BlockSpec((1, PROJ_TN), lambda i, j: (0, j)),
            pl.BlockSpec((D_MODEL, LANES_V7X), lambda i, j: (0, 0)),
            pl.BlockSpec((LANES_V7X, GLA_QK_W), lambda i, j: (0, 0)),
            pl.BlockSpec((1, GLA_QK_W), lambda i, j: (0, 0)),
        ],
        out_specs=[
            pl.BlockSpec((PROJ_TM, PROJ_TN), lambda i, j: (i, jnp.maximum(j - HEAD_TILES, 0))),
            pl.BlockSpec((None, None, MOBA_HEADS, PROJ_TM, MOBA_HD),
                         lambda i, j: (i // rpb, jnp.minimum(j, HEAD_TILES), 0, i % rpb, 0)),
            pl.BlockSpec((PROJ_TM, GLA_QK_W), lambda i, j: (i, 0)),
        ],
        out_shape=[
            jax.ShapeDtypeStruct((t, OUT_W), BF16),
            jax.ShapeDtypeStruct((bsz, HEAD_TILES + 1, MOBA_HEADS, t // bsz, MOBA_HD), BF16),
            jax.ShapeDtypeStruct((t, GLA_QK_W), F32),
        ],
        scratch_shapes=[pltpu.VMEM((PROJ_TM, D_MODEL), BF16),
                        pltpu.VMEM((PROJ_TM, D_MODEL), BF16),
                        pltpu.VMEM((PROJ_TM, LANES_V7X), F32)],
        compiler_params=pltpu.CompilerParams(
            dimension_semantics=("arbitrary", "arbitrary"),
            vmem_limit_bytes=_vmem_limit(vmem)),
        name="in_proj",
    )(x, norm_g, w_main, col_scale, w_lr, w_alpha, b_alpha)


def _bcast_rows(v, s, c):
    return jnp.broadcast_to(v, (c // s, s, GLA_DK)).reshape(c, GLA_DK)


def _gla_head(q, k, v, la, gr, gn, st_ref):
    c = q.shape[0]
    row = lax.broadcasted_iota(jnp.int32, (c, 1), 0)
    ri = lax.broadcasted_iota(jnp.int32, (c, c), 0)
    ci = lax.broadcasted_iota(jnp.int32, (c, c), 1)
    rxc = ri ^ ci

    def next_level(cs, s):
        ends = cs.reshape(c // s, s, GLA_DK)[:, s - 1:s, :]
        prev = jnp.concatenate([jnp.zeros_like(ends[:1]), ends[:-1]], axis=0)
        odd = ((row // s) & 1) == 1
        return ends, odd, cs + jnp.where(odd, _bcast_rows(prev, s, c), 0.0)

    cs = la.reshape(c // SUBLANES_V7X, SUBLANES_V7X, GLA_DK)
    sub = lax.broadcasted_iota(jnp.int32, cs.shape, 1)
    sh = 1
    while sh < SUBLANES_V7X:
        cs = cs + jnp.where(sub >= sh, pltpu.roll(cs, sh, axis=1), 0.0)
        sh *= 2
    cs = cs.reshape(c, GLA_DK)
    s = SUBLANES_V7X
    while s < GLA_BASE:
        _, _, cs = next_level(cs, s)
        s *= 2

    p = _dot_nt((q * jnp.exp(cs)).astype(BF16), (k * jnp.exp(-cs)).astype(BF16))
    attn = jnp.where(ci <= ri, p, 0.0)
    while s < c:
        ends, odd, cs_next = next_level(cs, s)
        qe = jnp.where(odd, jnp.exp(cs), 0.0)
        ke = jnp.where(odd, 0.0, jnp.exp(_bcast_rows(ends, s, c) - cs))
        p = _dot_nt((q * qe).astype(BF16), (k * ke).astype(BF16))
        attn = jnp.where(rxc < s, attn, p)
        cs = cs_next
        s *= 2

    b_last = cs[c - 1:c, :]
    st = st_ref[...]
    o = _dot_nt((q * jnp.exp(cs)).astype(BF16), st.astype(BF16))
    o = o + _dot(attn.astype(BF16), v)
    kd = (k * jnp.exp(b_last - cs)).astype(BF16)
    st_ref[...] = st * jnp.exp(b_last) + _dot_tn(v, kd)

    ms = jnp.mean(o * o, axis=-1, keepdims=True)
    y = o * lax.rsqrt(ms + NORM_EPS) * gn
    g = gr.astype(F32)
    return y * (g * jax.nn.sigmoid(g))


def _gla_kernel(q_ref, k_ref, v_ref, gr_ref, la_ref, gn_ref, o_ref, st_ref):
    @pl.when(pl.program_id(1) == 0)
    def _():
        st_ref[...] = jnp.zeros_like(st_ref)

    gn = gn_ref[...]
    for h in range(GLA_HEADS):
        ks = slice(h * GLA_DK, (h + 1) * GLA_DK)
        vs = slice(h * GLA_DV, (h + 1) * GLA_DV)
        out = _gla_head(q_ref[:, ks].astype(F32), k_ref[:, ks].astype(F32), v_ref[:, vs],
                        la_ref[:, ks], gr_ref[:, vs], gn, st_ref.at[h])
        o_ref[:, vs] = out.astype(o_ref.dtype)


def _gla(proj3, la3, gn):
    b, s, _ = proj3.shape
    c = GLA_CHUNK
    return pl.pallas_call(
        _gla_kernel,
        grid=(b, s // c),
        in_specs=[
            pl.BlockSpec((None, c, GLA_QK_W), lambda i, j: (i, j, COL_GQ // GLA_QK_W)),
            pl.BlockSpec((None, c, GLA_QK_W), lambda i, j: (i, j, COL_GK // GLA_QK_W)),
            pl.BlockSpec((None, c, GLA_V_W), lambda i, j: (i, j, COL_GV // GLA_V_W)),
            pl.BlockSpec((None, c, GLA_V_W), lambda i, j: (i, j, COL_GR // GLA_V_W)),
            pl.BlockSpec((None, c, GLA_QK_W), lambda i, j: (i, j, 0)),
            pl.BlockSpec((1, GLA_DV), lambda i, j: (0, 0)),
        ],
        out_specs=pl.BlockSpec((None, c, GLA_V_W), lambda i, j: (i, j, 0)),
        out_shape=jax.ShapeDtypeStruct((b, s, GLA_V_W), BF16),
        scratch_shapes=[pltpu.VMEM((GLA_HEADS, GLA_DV, GLA_DK), F32)],
        compiler_params=pltpu.CompilerParams(
            dimension_semantics=("parallel", "arbitrary"),
            vmem_limit_bytes=_vmem_limit(32 << 20)),
        name="gla",
    )(proj3, proj3, proj3, proj3, la3, gn)


def _split3_bf16(z):
    hi = z.astype(BF16).astype(F32)
    r = z - hi
    mid = r.astype(BF16).astype(F32)
    lo = (r - mid).astype(BF16).astype(F32)
    return hi, mid, lo


def _trunc_bf16(x):
    bits = lax.bitcast_convert_type(x, jnp.uint32) & jnp.uint32(0xFFFF0000)
    return lax.bitcast_convert_type(bits, F32)


def _moba_key_table(slopes, seq):
    nb = seq // MOBA_BLOCK
    pos = jnp.arange(seq)
    z = (slopes * LOG2E)[:, None] * pos.astype(F32)[None, :]
    hi = _trunc_bf16(z)
    mid = _trunc_bf16(z - hi)
    lo = _trunc_bf16(z - hi - mid)
    lane = jnp.arange(MOBA_HD)
    onehot = (lane[None, :] == (pos // MOBA_BLOCK)[:, None]).astype(F32)
    table = (onehot[None] + hi[..., None] * (lane == nb) + mid[..., None] * (lane == nb + 1)
             + lo[..., None] * (lane == nb + 2))
    return table.astype(BF16)


def _moba_kernel(slope_ref, item_q_ref, item_k_ref, q_ref, k_ref, v_ref, kx_ref, o_ref,
                 vt_ref, qt_ref, ka_ref, kmean_ref, kms_ref, m_ref, l_ref, acc_ref,
                 s_ref, smax_ref, p_ref, al_ref):
    h = pl.program_id(1)
    blk = MOBA_BLOCK
    hd = MOBA_HD
    seq = k_ref.shape[0]
    nb = seq // blk
    slope2 = slope_ref[h] * LOG2E

    def prep(j, carry):
        sl = pl.ds(pl.multiple_of(j * blk, blk), blk)
        kj = k_ref[sl, :]
        vt_ref[:, sl] = v_ref[sl, :].T
        qt_ref[0:hd, sl] = q_ref[sl, :].T
        ka_ref[sl, 0:hd] = kj
        ka_ref[sl, hd:2 * hd] = kx_ref[sl, :]
        kmean_ref[pl.ds(j, 1), :] = jnp.mean(kj.astype(F32), axis=0, keepdims=True)
        return carry
    lax.fori_loop(0, nb, prep, 0)
    qt_ref[hd + nb:2 * hd, :] = jnp.ones((hd - nb, seq), BF16)
    kms_ref[...] = jnp.concatenate(_split3_bf16(kmean_ref[...]), axis=0).astype(BF16)

    chunk = MOBA_SEL_CHUNK
    for c in range(seq // chunk):
        live = min(nb, -(-((c + 1) * (chunk // blk) - 1) // SUBLANES_V7X) * SUBLANES_V7X)
        live = max(live, SUBLANES_V7X)
        cs = slice(c * chunk, (c + 1) * chunk)
        g3 = _dot(kms_ref[...], qt_ref[0:hd, cs])
        gate = g3[0:live] + g3[nb:nb + live] + g3[2 * nb:2 * nb + live]
        bidx = lax.broadcasted_iota(jnp.int32, gate.shape, 0)
        qblk = c * (chunk // blk) + lax.broadcasted_iota(jnp.int32, gate.shape, 1) // blk
        gate = jnp.where(bidx < qblk, gate, -jnp.inf)
        mask = jnp.full(gate.shape, MASK_VALUE, F32)
        for _ in range(MOBA_TOPK):
            m = jnp.max(gate, axis=0, keepdims=True)
            first = jnp.min(jnp.where(gate == m, bidx, nb), axis=0, keepdims=True)
            pick = bidx == jnp.where(m > -jnp.inf, first, -1)
            mask = jnp.where(pick, 0.0, mask)
            gate = jnp.where(pick, -jnp.inf, gate)
        if live < nb:
            mask = jnp.concatenate([mask, jnp.full((nb - live, chunk), MASK_VALUE, F32)], axis=0)
        qt_ref[hd:hd + nb, cs] = mask.astype(BF16)

    m_ref[...] = jnp.full(m_ref.shape, MASK_VALUE, F32)
    l_ref[...] = jnp.zeros(l_ref.shape, F32)

    def clear(i, carry):
        acc_ref[i] = jnp.zeros((hd, blk), F32)
        return carry
    lax.fori_loop(0, nb, clear, 0)

    kpos = lax.broadcasted_iota(jnp.int32, (blk, blk), 0)
    qpos = lax.broadcasted_iota(jnp.int32, (blk, blk), 1)
    step_rows = s_ref.shape[0]

    def make_stages(own_block):
        rows = blk if own_block else MOBA_GROUP * blk
        units = step_rows // rows

        def item(n):
            if own_block:
                i = j = n
            else:
                i = item_q_ref[n]
                j = item_k_ref[n]
            return (i, pl.ds(pl.multiple_of(i * blk, blk), blk),
                    pl.ds(pl.multiple_of(j * blk, blk), rows))

        def scores(n0):
            for u in range(units):
                i, qs, ks = item(n0 + u)
                if own_block:
                    bias = (kpos + i * blk).astype(F32) * slope2
                    s = _dot(k_ref[ks, :], qt_ref[0:hd, qs]) + bias
                    s = jnp.where(kpos <= qpos, s, MASK_VALUE)
                else:
                    s = _dot(ka_ref[ks, :], qt_ref[:, qs])
                s_ref[u * rows:(u + 1) * rows, :] = s
                smax_ref[u] = jnp.max(s, axis=0, keepdims=True)

        def softmax(n0):
            for u in range(units):
                i, _, _ = item(n0 + u)
                row = pl.ds(i, 1)
                m_old = m_ref[row, :]
                m_new = jnp.maximum(m_old, smax_ref[u])
                alpha = jnp.exp2(m_old - m_new)
                p = jnp.exp2(s_ref[u * rows:(u + 1) * rows, :] - m_new)
                l_ref[row, :] = alpha * l_ref[row, :] + jnp.sum(p, axis=0, keepdims=True)
                m_ref[row, :] = m_new
                p_ref[u * rows:(u + 1) * rows, :] = p.astype(BF16)
                al_ref[u] = alpha

        def values(n0):
            for u in range(units):
                i, _, ks = item(n0 + u)
                acc_ref[i] = (al_ref[u] * acc_ref[i]
                              + _dot(vt_ref[:, ks], p_ref[u * rows:(u + 1) * rows, :]))

        return units, scores, softmax, values

    def run_pipeline(n_items, stages):
        units, scores, softmax, values = stages
        steps = n_items // units
        scores(0)
        softmax(0)
        scores(units)

        def body(t, carry):
            values(t * units)
            softmax((t + 1) * units)
            scores((t + 2) * units)
            return carry
        lax.fori_loop(0, steps - 2, body, 0)
        values((steps - 2) * units)
        softmax((steps - 1) * units)
        values((steps - 1) * units)

    run_pipeline(nb, make_stages(True))
    run_pipeline(item_q_ref.shape[0], make_stages(False))

    def finish(i0, carry):
        for u in range(MOBA_FINISH_UNROLL):
            i = i0 * MOBA_FINISH_UNROLL + u
            qs = pl.ds(pl.multiple_of(i * blk, blk), blk)
            inv = 1.0 / l_ref[pl.ds(i, 1), :]
            o_ref[qs, :] = (acc_ref[i] * inv).T.astype(o_ref.dtype)
        return carry
    lax.fori_loop(0, nb // MOBA_FINISH_UNROLL, finish, 0)


def _moba(qkv, slopes):
    b, _, _, s, _ = qkv.shape
    blk = MOBA_BLOCK
    nb = s // blk
    hd = MOBA_HD
    assert nb + 3 <= hd and s % MOBA_SEL_CHUNK == 0
    groups = [(i, j) for i in range(nb) for j in range(0, i, MOBA_GROUP)]
    own_units = MOBA_STEP_BLOCKS
    grp_units = MOBA_STEP_BLOCKS // MOBA_GROUP
    assert MOBA_STEP_BLOCKS % MOBA_GROUP == 0 and nb % MOBA_GROUP == 0
    assert nb % own_units == 0 and nb // own_units >= 2
    assert len(groups) % grp_units == 0 and len(groups) // grp_units >= 2
    assert nb % MOBA_FINISH_UNROLL == 0
    item_q = jnp.asarray(np.array([g[0] for g in groups], np.int32))
    item_k = jnp.asarray(np.array([g[1] for g in groups], np.int32))
    step_rows = MOBA_STEP_BLOCKS * blk
    vmem = (2 * 5 * s * hd * 2
            + s * hd * 2 + 2 * 2 * s * hd * 2
            + nb * hd * blk * 4
            + step_rows * blk * 6
            + 2 * step_rows * blk * 4 + (4 << 20))
    smem = pl.BlockSpec(memory_space=pltpu.SMEM)
    return pl.pallas_call(
        _moba_kernel,
        grid=(b, MOBA_HEADS),
        in_specs=[
            smem, smem, smem,
            pl.BlockSpec((None, None, None, s, hd), lambda bi, h: (bi, 0, h, 0, 0)),
            pl.BlockSpec((None, None, None, s, hd), lambda bi, h: (bi, 1, h, 0, 0)),
            pl.BlockSpec((None, None, None, s, hd), lambda bi, h: (bi, 2, h, 0, 0)),
            pl.BlockSpec((None, s, hd), lambda bi, h: (h, 0, 0)),
        ],
        out_specs=pl.BlockSpec((None, None, s, hd), lambda bi, h: (bi, h, 0, 0)),
        out_shape=jax.ShapeDtypeStruct((b, MOBA_HEADS, s, hd), BF16),
        scratch_shapes=[
            pltpu.VMEM((hd, s), BF16),
            pltpu.VMEM((2 * hd, s), BF16),
            pltpu.VMEM((s, 2 * hd), BF16),
            pltpu.VMEM((nb, hd), F32),
            pltpu.VMEM((3 * nb, hd), BF16),
            pltpu.VMEM((nb, blk), F32),
            pltpu.VMEM((nb, blk), F32),
            pltpu.VMEM((nb, hd, blk), F32),
            pltpu.VMEM((step_rows, blk), F32),
            pltpu.VMEM((own_units, 1, blk), F32),
            pltpu.VMEM((step_rows, blk), BF16),
            pltpu.VMEM((own_units, 1, blk), F32),
        ],
        compiler_params=pltpu.CompilerParams(
            dimension_semantics=("parallel", "parallel"),
            vmem_limit_bytes=_vmem_limit(vmem)),
        name="moba",
    )(slopes, item_q, item_k, qkv, qkv, qkv, _moba_key_table(slopes, s))


def _merge_kernel(x_ref, oa_ref, ob_ref, ga_ref, gb_ref, wa_ref, wb_ref, wo_ref, o_ref):
    ya = _dot(oa_ref[...], wa_ref[...])
    ob = jnp.concatenate([ob_ref[h] for h in range(MOBA_HEADS)], axis=1)
    yb = _dot(ob, wb_ref[...])
    mix = (jax.nn.sigmoid(ga_ref[...].astype(F32)) * ya
           + jax.nn.sigmoid(gb_ref[...].astype(F32)) * yb)
    o_ref[...] = x_ref[...] + _dot(mix.astype(BF16), wo_ref[...])


def _merge(x, oa, ob, proj, wa, wb, wo):
    t = x.shape[0]
    tm = MERGE_TM
    rpb = ob.shape[2] // tm
    const = dict(pipeline_mode=pl.Buffered(1))
    vmem = (2 * 2 * tm * D_MODEL * 4 + 2 * 2 * tm * GLA_V_W * 2 + 2 * 2 * tm * D_MODEL * 2
            + (2 * GLA_V_W + D_MODEL) * D_MODEL * 2 + 4 * tm * D_MODEL * 4 + (4 << 20))
    return pl.pallas_call(
        _merge_kernel,
        grid=(t // tm,),
        in_specs=[
            pl.BlockSpec((tm, D_MODEL), lambda i: (i, 0)),
            pl.BlockSpec((tm, GLA_V_W), lambda i: (i, 0)),
            pl.BlockSpec((None, MOBA_HEADS, tm, MOBA_HD), lambda i: (i // rpb, 0, i % rpb, 0)),
            pl.BlockSpec((tm, D_MODEL), lambda i: (i, COL_GA // D_MODEL)),
            pl.BlockSpec((tm, D_MODEL), lambda i: (i, COL_GB // D_MODEL)),
            pl.BlockSpec((GLA_V_W, D_MODEL), lambda i: (0, 0), **const),
            pl.BlockSpec((MOBA_W, D_MODEL), lambda i: (0, 0), **const),
            pl.BlockSpec((D_MODEL, D_MODEL), lambda i: (0, 0), **const),
        ],
        out_specs=pl.BlockSpec((tm, D_MODEL), lambda i: (i, 0)),
        out_shape=jax.ShapeDtypeStruct((t, D_MODEL), F32),
        compiler_params=pltpu.CompilerParams(
            dimension_semantics=("parallel",),
            vmem_limit_bytes=_vmem_limit(vmem)),
        name="merge",
    )(x, oa, ob, proj, proj, wa, wb, wo)


def kernel(x, ffn1_norm, ffn1_w_gate, ffn1_w_up, ffn1_w_down, mix_norm, w_in, gla_w_alpha,
           gla_b_alpha, gla_out_norm, w_branch_gla, w_branch_moba, w_out, ffn2_norm,
           ffn2_w_gate, ffn2_w_up, ffn2_w_down, final_norm):
    bsz, seq, d = x.shape
    t = bsz * seq
    depth = ffn1_norm.shape[0]
    assert depth >= 1 and d == D_MODEL and seq % GLA_CHUNK == 0 and seq % MOBA_BLOCK == 0
    assert t % FFN_TM == 0 and seq % PROJ_TM == 0 and seq % MERGE_TM == 0
    assert OUT_W % PROJ_TN == 0
    assert PROJ_TM % LA_ROWS == 0 and PROJ_TM // LA_ROWS <= PROJ_W // PROJ_TN

    slopes = jnp.exp2(-8.0 * jnp.arange(1, MOBA_HEADS + 1, dtype=F32) / MOBA_HEADS)
    col_scale = jnp.ones((1, PROJ_W), F32)
    col_scale = col_scale.at[:, HEAD_COLS + COL_GQ:HEAD_COLS + COL_GQ + GLA_QK_W].set(GLA_DK ** -0.5)
    col_scale = col_scale.at[:, :MOBA_W].set(MOBA_HD ** -0.5 * LOG2E)
    final_g = final_norm.reshape(1, D_MODEL)

    xf = x.reshape(t, D_MODEL)
    for l in range(depth):
        last = l == depth - 1
        w_l = w_in[l]
        w_b = w_l.astype(BF16)
        w_main = jnp.concatenate([w_b[:, slice(*W_IN_MOBA)], w_b[:, slice(*W_IN_GATES)],
                                  w_b[:, slice(*W_IN_GLA)]], axis=1)
        w_lr = jnp.pad(jnp.tile(w_l[:, slice(*W_IN_LR)], (1, 3)),
                       ((0, 0), (0, LANES_V7X - 3 * GLA_RANK))).astype(BF16)
        wa_hi = _trunc_bf16(gla_w_alpha[l])
        wa_mid = _trunc_bf16(gla_w_alpha[l] - wa_hi)
        w_alpha = jnp.pad(jnp.concatenate([wa_hi, wa_mid, wa_hi], axis=0),
                          ((0, LANES_V7X - 3 * GLA_RANK), (0, 0))).astype(BF16)

        xf = _ffn(xf, ffn1_norm[l].reshape(1, D_MODEL), ffn1_w_gate[l].astype(BF16),
                  ffn1_w_up[l].astype(BF16), ffn1_w_down[l].astype(BF16), final_g, False)
        proj, qkv, log_a = _in_proj(xf, mix_norm[l].reshape(1, D_MODEL), w_main, col_scale,
                                    w_lr, w_alpha, gla_b_alpha[l].reshape(1, GLA_QK_W), bsz)
        proj3 = proj.reshape(bsz, seq, OUT_W)
        oa = _gla(proj3, log_a.reshape(bsz, seq, GLA_QK_W), gla_out_norm[l].reshape(1, GLA_DV))
        ob = _moba(qkv, slopes)
        xf = _merge(xf, oa.reshape(t, GLA_V_W), ob, proj,
                    w_branch_gla[l].astype(BF16), w_branch_moba[l].astype(BF16),
                    w_out[l].astype(BF16))
        xf = _ffn(xf, ffn2_norm[l].reshape(1, D_MODEL), ffn2_w_gate[l].astype(BF16),
                  ffn2_w_up[l].astype(BF16), ffn2_w_down[l].astype(BF16), final_g, last)
    return xf.reshape(bsz, seq, D_MODEL)
```

```python
import functools

import jax
import jax.numpy as jnp
import numpy as np
from jax import lax
from jax.experimental import pallas as pl
from jax.experimental.pallas import tpu as pltpu

F32 = jnp.float32
BF16 = jnp.bfloat16

D_MODEL = 2048
D_FF = 5632
GLA_HEADS = 4
GLA_DK = 128
GLA_DV = 256
GLA_RANK = 16
GLA_TAU = 16.0
MOBA_HEADS = 8
MOBA_HD = 128
MOBA_BLOCK = 256
MOBA_TOPK = 3
NORM_EPS = 1e-6

GLA_QK_W = GLA_HEADS * GLA_DK
GLA_V_W = GLA_HEADS * GLA_DV
MOBA_W = MOBA_HEADS * MOBA_HD

HEAD_COLS = 3 * MOBA_W
COL_GA = 0
COL_GB = COL_GA + D_MODEL
COL_GQ = COL_GB + D_MODEL
COL_GK = COL_GQ + GLA_QK_W
COL_GV = COL_GK + GLA_QK_W
COL_GR = COL_GV + GLA_V_W
OUT_W = COL_GR + GLA_V_W
PROJ_W = HEAD_COLS + OUT_W
W_IN_GLA = (0, 2 * GLA_QK_W + 2 * GLA_V_W)
W_IN_LR = (W_IN_GLA[1], W_IN_GLA[1] + GLA_RANK)
W_IN_MOBA = (W_IN_LR[1], W_IN_LR[1] + 3 * MOBA_W)
W_IN_GATES = (W_IN_MOBA[1], W_IN_MOBA[1] + 2 * D_MODEL)

LANES_V7X = 128
SUBLANES_V7X = 8
VMEM_BYTES_V7X = 64 << 20
VMEM_CAP = VMEM_BYTES_V7X - (6 << 20)

FFN_TM = 1024
FFN_TF = 512
PROJ_TM = 1024
PROJ_TN = MOBA_W
HEAD_TILES = HEAD_COLS // PROJ_TN
LA_ROWS = 128
GLA_CHUNK = 256
GLA_STEP_CHUNKS = 2
GLA_BASE = 2 * SUBLANES_V7X
MERGE_TM = 512
NORM_ROWS = 256
AHEAD_ROWS = 128

MASK_VALUE = -1e30
MOBA_STEP_BLOCKS = 16
MOBA_OWN_STEP_BLOCKS = 16
MOBA_SUM_ROWS = 16
MOBA_GROUP = 4
MOBA_SEL_CHUNK = 1024
MOBA_FINISH_UNROLL = 4
LOG2E = 1.4426950408889634


def _vmem_limit(nbytes):
    return int(min(VMEM_CAP, nbytes))


def _dot(a, b):
    return jnp.dot(a, b, preferred_element_type=F32)


def _dot_nt(a, b, precision=None):
    return lax.dot_general(a, b, (((1,), (1,)), ((), ())),
                           preferred_element_type=F32, precision=precision)


def _dot_tn(a, b):
    return lax.dot_general(a, b, (((0,), (0,)), ((), ())), preferred_element_type=F32)


def _rms_norm_slice(x_ref, g_ref, out_ref, sl):
    x = x_ref[sl, :].astype(F32)
    ms = jnp.mean(x * x, axis=-1, keepdims=True)
    out_ref[sl, :] = (x * lax.rsqrt(ms + NORM_EPS) * g_ref[...]).astype(out_ref.dtype)


def _rms_norm_rows(x_ref, g_ref, out_ref, rows):
    def body(r, carry):
        _rms_norm_slice(x_ref, g_ref, out_ref,
                        pl.ds(pl.multiple_of(r * NORM_ROWS, NORM_ROWS), NORM_ROWS))
        return carry
    lax.fori_loop(0, rows // NORM_ROWS, body, 0)


def _next_row_index(i, j, n_rows):
    return jnp.minimum(i + jnp.minimum(j, 1), n_rows - 1)


def _with_norm_ahead(x_ref, g_ref, xn_even_ref, xn_odd_ref, rows, step):
    i = pl.program_id(0)
    j = pl.program_id(1)

    @pl.when(jnp.logical_and(i == 0, j == 0))
    def _():
        _rms_norm_rows(x_ref, g_ref, xn_even_ref, rows)

    r = jnp.clip(j - 1, 0, rows // AHEAD_ROWS - 1)
    sl = pl.ds(pl.multiple_of(r * AHEAD_ROWS, AHEAD_ROWS), AHEAD_ROWS)
    for parity, cur, nxt in ((0, xn_even_ref, xn_odd_ref), (1, xn_odd_ref, xn_even_ref)):
        @pl.when(i % 2 == parity)
        def _(cur=cur, nxt=nxt):
            step(cur, functools.partial(_rms_norm_slice, x_ref, g_ref, nxt, sl))


def _ffn_kernel(x_ref, g_ref, wg_ref, wu_ref, wd_ref, fg_ref, o_ref, xn_ref, *, final_norm):
    j = pl.program_id(1)

    @pl.when(j == 0)
    def _():
        _rms_norm_rows(x_ref, g_ref, xn_ref, FFN_TM)
        o_ref[...] = x_ref[...]

    xn = xn_ref[...]
    hg = _dot(xn, wg_ref[...])
    hu = _dot(xn, wu_ref[...])
    a = (0.5 * hg * jax.nn.sigmoid(hg)) * hu
    o_ref[...] += _dot(a.astype(BF16), wd_ref[...])

    if final_norm:
        @pl.when(j == pl.num_programs(1) - 1)
        def _():
            _rms_norm_rows(o_ref, fg_ref, o_ref, FFN_TM)


def _ffn(x, norm_g, wg, wu, wd, final_g, final_norm):
    t = x.shape[0]
    vmem = (2 * 2 * FFN_TM * D_MODEL * 4
            + FFN_TM * D_MODEL * 2
            + 2 * 3 * D_MODEL * FFN_TF * 2
            + 4 * FFN_TM * FFN_TF * 4
            + (4 << 20))
    return pl.pallas_call(
        functools.partial(_ffn_kernel, final_norm=final_norm),
        grid=(t // FFN_TM, D_FF // FFN_TF),
        in_specs=[
            pl.BlockSpec((FFN_TM, D_MODEL), lambda i, j: (i, 0)),
            pl.BlockSpec((1, D_MODEL), lambda i, j: (0, 0)),
            pl.BlockSpec((D_MODEL, FFN_TF), lambda i, j: (0, j)),
            pl.BlockSpec((D_MODEL, FFN_TF), lambda i, j: (0, j)),
            pl.BlockSpec((FFN_TF, D_MODEL), lambda i, j: (j, 0)),
            pl.BlockSpec((1, D_MODEL), lambda i, j: (0, 0)),
        ],
        out_specs=pl.BlockSpec((FFN_TM, D_MODEL), lambda i, j: (i, 0)),
        out_shape=jax.ShapeDtypeStruct((t, D_MODEL), F32),
        scratch_shapes=[pltpu.VMEM((FFN_TM, D_MODEL), BF16)],
        compiler_params=pltpu.CompilerParams(
            dimension_semantics=("parallel", "arbitrary"),
            vmem_limit_bytes=_vmem_limit(vmem)),
        name="ffn_final" if final_norm else "ffn",
    )(x, norm_g, wg, wu, wd, final_g)


def _log_sigmoid(z):
    return jnp.minimum(z, 0.0) - jnp.log1p(jnp.exp(-jnp.abs(z)))


def _in_proj_kernel(x_ref, g_ref, w_ref, cs_ref, wlr_ref, wa_ref, ba_ref,
                    proj_ref, heads_ref, la_ref, xn_even_ref, xn_odd_ref, lr_ref):
    j = pl.program_id(1)

    def step(xn_ref, norm_next):
        @pl.when(j == 0)
        def _():
            lr_ref[...] = _dot(xn_ref[...], wlr_ref[...])

        norm_next()

        rows = pl.ds(pl.multiple_of(jnp.minimum(j, PROJ_TM // LA_ROWS - 1) * LA_ROWS, LA_ROWS),
                     LA_ROWS)
        lr = lr_ref[rows, :]
        lr_hi = lr.astype(BF16).astype(F32)
        lane = lax.broadcasted_iota(jnp.int32, lr.shape, 1)
        packed = jnp.where(lane < 2 * GLA_RANK, lr_hi, lr - lr_hi).astype(BF16)
        z = _dot(packed, wa_ref[...]) + ba_ref[...]
        la_ref[rows, :] = _log_sigmoid(z) * (1.0 / GLA_TAU)

        res = (_dot(xn_ref[...], w_ref[...]) * cs_ref[...]).astype(BF16)
        proj_ref[...] = res
        for h in range(MOBA_HEADS):
            heads_ref[h] = res[:, h * MOBA_HD:(h + 1) * MOBA_HD]

    _with_norm_ahead(x_ref, g_ref, xn_even_ref, xn_odd_ref, PROJ_TM, step)


def _in_proj(x, norm_g, w_main, col_scale, w_lr, w_alpha, b_alpha, bsz):
    t = x.shape[0]
    rpb = t // bsz // PROJ_TM
    assert PROJ_W // PROJ_TN > PROJ_TM // AHEAD_ROWS and 3 * GLA_RANK <= LANES_V7X
    vmem = (2 * PROJ_TM * D_MODEL * 4 + 2 * PROJ_TM * D_MODEL * 2
            + 2 * D_MODEL * PROJ_TN * 2 + 2 * 2 * PROJ_TM * PROJ_TN * 2
            + 2 * PROJ_TM * GLA_QK_W * 4 + 2 * PROJ_TM * PROJ_TN * 4
            + 2 * D_MODEL * LANES_V7X * 2 + PROJ_TM * LANES_V7X * 4 + (4 << 20))
    return pl.pallas_call(
        _in_proj_kernel,
        grid=(t // PROJ_TM, PROJ_W // PROJ_TN),
        in_specs=[
            pl.BlockSpec((PROJ_TM, D_MODEL), lambda i, j: (_next_row_index(i, j, t // PROJ_TM), 0)),
            pl.BlockSpec((1, D_MODEL), lambda i, j: (0, 0)),
            pl.BlockSpec((D_MODEL, PROJ_TN), lambda i, j: (0, j)),
            pl.BlockSpec((1, PROJ_TN), lambda i, j: (0, j)),
            pl.BlockSpec((D_MODEL, LANES_V7X), lambda i, j: (0, 0)),
            pl.BlockSpec((LANES_V7X, GLA_QK_W), lambda i, j: (0, 0)),
            pl.BlockSpec((1, GLA_QK_W), lambda i, j: (0, 0)),
        ],
        out_specs=[
            pl.BlockSpec((PROJ_TM, PROJ_TN), lambda i, j: (i, jnp.maximum(j - HEAD_TILES, 0))),
            pl.BlockSpec((None, None, MOBA_HEADS, PROJ_TM, MOBA_HD),
                         lambda i, j: (i // rpb, jnp.minimum(j, HEAD_TILES), 0, i % rpb, 0)),
            pl.BlockSpec((PROJ_TM, GLA_QK_W), lambda i, j: (i, 0)),
        ],
        out_shape=[
            jax.ShapeDtypeStruct((t, OUT_W), BF16),
            jax.ShapeDtypeStruct((bsz, HEAD_TILES + 1, MOBA_HEADS, t // bsz, MOBA_HD), BF16),
            jax.ShapeDtypeStruct((t, GLA_QK_W), F32),
        ],
        scratch_shapes=[pltpu.VMEM((PROJ_TM, D_MODEL), BF16),
                        pltpu.VMEM((PROJ_TM, D_MODEL), BF16),
                        pltpu.VMEM((PROJ_TM, LANES_V7X), F32)],
        compiler_params=pltpu.CompilerParams(
            dimension_semantics=("arbitrary", "arbitrary"),
            vmem_limit_bytes=_vmem_limit(vmem)),
        name="in_proj",
    )(x, norm_g, w_main, col_scale, w_lr, w_alpha, b_alpha)


def _bcast_rows(v, s, c):
    return jnp.broadcast_to(v, (c // s, s, GLA_DK)).reshape(c, GLA_DK)


def _gla_head(q, k, v, la, gr, gn, st_ref):
    c = q.shape[0]
    row = lax.broadcasted_iota(jnp.int32, (c, 1), 0)
    ri = lax.broadcasted_iota(jnp.int32, (c, c), 0)
    ci = lax.broadcasted_iota(jnp.int32, (c, c), 1)
    rxc = ri ^ ci

    def next_level(cs, s):
        ends = cs.reshape(c // s, s, GLA_DK)[:, s - 1:s, :]
        prev = jnp.concatenate([jnp.zeros_like(ends[:1]), ends[:-1]], axis=0)
        odd = ((row // s) & 1) == 1
        return ends, odd, cs + jnp.where(odd, _bcast_rows(prev, s, c), 0.0)

    cs = la.reshape(c // SUBLANES_V7X, SUBLANES_V7X, GLA_DK)
    sub = lax.broadcasted_iota(jnp.int32, cs.shape, 1)
    sh = 1
    while sh < SUBLANES_V7X:
        cs = cs + jnp.where(sub >= sh, pltpu.roll(cs, sh, axis=1), 0.0)
        sh *= 2
    cs = cs.reshape(c, GLA_DK)
    s = SUBLANES_V7X
    while s < GLA_BASE:
        _, _, cs = next_level(cs, s)
        s *= 2

    p = _dot_nt((q * jnp.exp(cs)).astype(BF16), (k * jnp.exp(-cs)).astype(BF16))
    attn = jnp.where(ci <= ri, p, 0.0)
    while s < c:
        ends, odd, cs_next = next_level(cs, s)
        qe = jnp.where(odd, jnp.exp(cs), 0.0)
        ke = jnp.where(odd, 0.0, jnp.exp(_bcast_rows(ends, s, c) - cs))
        p = _dot_nt((q * qe).astype(BF16), (k * ke).astype(BF16))
        attn = jnp.where(rxc < s, attn, p)
        cs = cs_next
        s *= 2

    b_last = cs[c - 1:c, :]
    st = st_ref[...]
    o = _dot_nt((q * jnp.exp(cs)).astype(BF16), st.astype(BF16))
    o = o + _dot(attn.astype(BF16), v)
    kd = (k * jnp.exp(b_last - cs)).astype(BF16)
    st_ref[...] = st * jnp.exp(b_last) + _dot_tn(v, kd)

    ms = jnp.mean(o * o, axis=-1, keepdims=True)
    y = o * lax.rsqrt(ms + NORM_EPS) * gn
    g = gr.astype(F32)
    return y * (g * jax.nn.sigmoid(g))


def _gla_kernel(q_ref, k_ref, v_ref, gr_ref, la_ref, gn_ref, o_ref, st_ref):
    @pl.when(pl.program_id(1) == 0)
    def _():
        st_ref[...] = jnp.zeros_like(st_ref)

    gn = gn_ref[...]
    for n in range(GLA_STEP_CHUNKS):
        rs = slice(n * GLA_CHUNK, (n + 1) * GLA_CHUNK)
        for h in range(GLA_HEADS):
            ks = slice(h * GLA_DK, (h + 1) * GLA_DK)
            vs = slice(h * GLA_DV, (h + 1) * GLA_DV)
            out = _gla_head(q_ref[rs, ks].astype(F32), k_ref[rs, ks].astype(F32), v_ref[rs, vs],
                            la_ref[rs, ks], gr_ref[rs, vs], gn, st_ref.at[h])
            o_ref[rs, vs] = out.astype(o_ref.dtype)


def _gla(proj3, la3, gn):
    b, s, _ = proj3.shape
    c = GLA_CHUNK * GLA_STEP_CHUNKS
    return pl.pallas_call(
        _gla_kernel,
        grid=(b, s // c),
        in_specs=[
            pl.BlockSpec((None, c, GLA_QK_W), lambda i, j: (i, j, COL_GQ // GLA_QK_W)),
            pl.BlockSpec((None, c, GLA_QK_W), lambda i, j: (i, j, COL_GK // GLA_QK_W)),
            pl.BlockSpec((None, c, GLA_V_W), lambda i, j: (i, j, COL_GV // GLA_V_W)),
            pl.BlockSpec((None, c, GLA_V_W), lambda i, j: (i, j, COL_GR // GLA_V_W)),
            pl.BlockSpec((None, c, GLA_QK_W), lambda i, j: (i, j, 0)),
            pl.BlockSpec((1, GLA_DV), lambda i, j: (0, 0)),
        ],
        out_specs=pl.BlockSpec((None, c, GLA_V_W), lambda i, j: (i, j, 0)),
        out_shape=jax.ShapeDtypeStruct((b, s, GLA_V_W), BF16),
        scratch_shapes=[pltpu.VMEM((GLA_HEADS, GLA_DV, GLA_DK), F32)],
        compiler_params=pltpu.CompilerParams(
            dimension_semantics=("parallel", "arbitrary"),
            vmem_limit_bytes=_vmem_limit(32 << 20)),
        name="gla",
    )(proj3, proj3, proj3, proj3, la3, gn)


def _split3_bf16(z):
    hi = z.astype(BF16).astype(F32)
    r = z - hi
    mid = r.astype(BF16).astype(F32)
    lo = (r - mid).astype(BF16).astype(F32)
    return hi, mid, lo


def _trunc_bf16(x):
    bits = lax.bitcast_convert_type(x, jnp.uint32) & jnp.uint32(0xFFFF0000)
    return lax.bitcast_convert_type(bits, F32)


def _moba_key_table(slopes, seq):
    nb = seq // MOBA_BLOCK
    pos = jnp.arange(seq)
    z = (slopes * LOG2E)[:, None] * pos.astype(F32)[None, :]
    hi = _trunc_bf16(z)
    mid = _trunc_bf16(z - hi)
    lo = _trunc_bf16(z - hi - mid)
    lane = jnp.arange(MOBA_HD)
    onehot = (lane[None, :] == (pos // MOBA_BLOCK)[:, None]).astype(F32)
    table = (onehot[None] + hi[..., None] * (lane == nb) + mid[..., None] * (lane == nb + 1)
             + lo[..., None] * (lane == nb + 2))
    return table.astype(BF16)


def _moba_kernel(slope_ref, item_q_ref, item_k_ref, q_ref, k_ref, v_ref, kx_ref, o_ref,
                 vt_ref, qt_ref, ka_ref, kmean_ref, kms_ref, m_ref, acc_ref,
                 s_ref, smax_ref, p_ref, al_ref):
    h = pl.program_id(1)
    blk = MOBA_BLOCK
    hd = MOBA_HD
    seq = k_ref.shape[0]
    nb = seq // blk
    slope2 = slope_ref[h] * LOG2E

    def prep(j, carry):
        sl = pl.ds(pl.multiple_of(j * blk, blk), blk)
        kj = k_ref[sl, :]
        vt_ref[0:hd, sl] = v_ref[sl, :].T
        qt_ref[0:hd, sl] = q_ref[sl, :].T
        ka_ref[sl, 0:hd] = kj
        ka_ref[sl, hd:2 * hd] = kx_ref[sl, :]
        kmean_ref[pl.ds(j, 1), :] = jnp.mean(kj.astype(F32), axis=0, keepdims=True)
        return carry
    lax.fori_loop(0, nb, prep, 0)
    qt_ref[hd + nb:2 * hd, :] = jnp.ones((hd - nb, seq), BF16)
    kms_ref[...] = jnp.concatenate(_split3_bf16(kmean_ref[...]), axis=0).astype(BF16)

    chunk = MOBA_SEL_CHUNK
    for c in range(seq // chunk):
        live = min(nb, -(-((c + 1) * (chunk // blk) - 1) // SUBLANES_V7X) * SUBLANES_V7X)
        live = max(live, SUBLANES_V7X)
        cs = slice(c * chunk, (c + 1) * chunk)
        g3 = _dot(kms_ref[...], qt_ref[0:hd, cs])
        gate = g3[0:live] + g3[nb:nb + live] + g3[2 * nb:2 * nb + live]
        bidx = lax.broadcasted_iota(jnp.int32, gate.shape, 0)
        qblk = c * (chunk // blk) + lax.broadcasted_iota(jnp.int32, gate.shape, 1) // blk
        gate = jnp.where(bidx < qblk, gate, -jnp.inf)
        mask = jnp.full(gate.shape, MASK_VALUE, F32)
        for _ in range(MOBA_TOPK):
            m = jnp.max(gate, axis=0, keepdims=True)
            first = jnp.min(jnp.where(gate == m, bidx, nb), axis=0, keepdims=True)
            pick = bidx == jnp.where(m > -jnp.inf, first, -1)
            mask = jnp.where(pick, 0.0, mask)
            gate = jnp.where(pick, -jnp.inf, gate)
        if live < nb:
            mask = jnp.concatenate([mask, jnp.full((nb - live, chunk), MASK_VALUE, F32)], axis=0)
        qt_ref[hd:hd + nb, cs] = mask.astype(BF16)

    m_ref[...] = jnp.full(m_ref.shape, MASK_VALUE, F32)
    ones_row = lax.broadcasted_iota(jnp.int32, (MOBA_SUM_ROWS, seq), 0) == 0
    vt_ref[hd:hd + MOBA_SUM_ROWS, :] = jnp.where(ones_row, 1.0, 0.0).astype(BF16)

    def clear(i, carry):
        acc_ref[i] = jnp.zeros((hd + MOBA_SUM_ROWS, blk), F32)
        return carry
    lax.fori_loop(0, nb, clear, 0)

    kpos = lax.broadcasted_iota(jnp.int32, (blk, blk), 0)
    qpos = lax.broadcasted_iota(jnp.int32, (blk, blk), 1)

    def make_stages(own_block):
        rows = blk if own_block else MOBA_GROUP * blk
        units = (MOBA_OWN_STEP_BLOCKS if own_block else MOBA_STEP_BLOCKS) * blk // rows

        def item(n):
            if own_block:
                i = j = n
            else:
                i = item_q_ref[n]
                j = item_k_ref[n]
            return (i, pl.ds(pl.multiple_of(i * blk, blk), blk),
                    pl.ds(pl.multiple_of(j * blk, blk), rows))

        def scores(n0):
            for u in range(units):
                i, qs, ks = item(n0 + u)
                if own_block:
                    bias = (kpos + i * blk).astype(F32) * slope2
                    s = _dot(k_ref[ks, :], qt_ref[0:hd, qs]) + bias
                    s = jnp.where(kpos <= qpos, s, MASK_VALUE)
                else:
                    s = _dot(ka_ref[ks, :], qt_ref[:, qs])
                s_ref[u * rows:(u + 1) * rows, :] = s
                smax_ref[u] = jnp.max(s, axis=0, keepdims=True)

        def softmax(n0):
            for u in range(units):
                i, _, _ = item(n0 + u)
                row = pl.ds(i, 1)
                m_old = m_ref[row, :]
                m_new = jnp.maximum(m_old, smax_ref[u])
                alpha = jnp.exp2(m_old - m_new)
                p = jnp.exp2(s_ref[u * rows:(u + 1) * rows, :] - m_new)
                m_ref[row, :] = m_new
                p_ref[u * rows:(u + 1) * rows, :] = p.astype(BF16)
                al_ref[u] = alpha

        def values(n0):
            for u in range(units):
                i, _, ks = item(n0 + u)
                acc_ref[i] = (al_ref[u] * acc_ref[i]
                              + _dot(vt_ref[:, ks], p_ref[u * rows:(u + 1) * rows, :]))

        return units, scores, softmax, values

    def run_pipeline(n_items, stages):
        units, scores, softmax, values = stages
        steps = n_items // units
        scores(0)
        softmax(0)
        scores(units)

        def body(t, carry):
            values(t * units)
            softmax((t + 1) * units)
            scores((t + 2) * units)
            return carry
        lax.fori_loop(0, steps - 2, body, 0)
        values((steps - 2) * units)
        softmax((steps - 1) * units)
        values((steps - 1) * units)

    run_pipeline(nb, make_stages(True))
    run_pipeline(item_q_ref.shape[0], make_stages(False))

    def finish(i0, carry):
        for u in range(MOBA_FINISH_UNROLL):
            i = i0 * MOBA_FINISH_UNROLL + u
            qs = pl.ds(pl.multiple_of(i * blk, blk), blk)
            acc = acc_ref[i]
            inv = 1.0 / acc[hd:hd + 1, :]
            o_ref[qs, :] = (acc[0:hd] * inv).T.astype(o_ref.dtype)
        return carry
    lax.fori_loop(0, nb // MOBA_FINISH_UNROLL, finish, 0)


def _moba(qkv, slopes):
    b, _, _, s, _ = qkv.shape
    blk = MOBA_BLOCK
    nb = s // blk
    hd = MOBA_HD
    assert nb + 3 <= hd and s % MOBA_SEL_CHUNK == 0
    groups = [(i, j) for i in range(nb) for j in range(0, i, MOBA_GROUP)]
    own_units = MOBA_OWN_STEP_BLOCKS
    grp_units = MOBA_STEP_BLOCKS // MOBA_GROUP
    assert MOBA_STEP_BLOCKS % MOBA_GROUP == 0 and nb % MOBA_GROUP == 0
    assert MOBA_OWN_STEP_BLOCKS <= MOBA_STEP_BLOCKS and grp_units <= own_units
    assert nb % own_units == 0 and nb // own_units >= 2
    assert len(groups) % grp_units == 0 and len(groups) // grp_units >= 2
    assert nb % MOBA_FINISH_UNROLL == 0
    item_q = jnp.asarray(np.array([g[0] for g in groups], np.int32))
    item_k = jnp.asarray(np.array([g[1] for g in groups], np.int32))
    step_rows = MOBA_STEP_BLOCKS * blk
    vmem = (2 * 5 * s * hd * 2
            + s * (hd + MOBA_SUM_ROWS) * 2 + 2 * 2 * s * hd * 2
            + nb * (hd + MOBA_SUM_ROWS) * blk * 4
            + step_rows * blk * 6
            + 2 * step_rows * blk * 4 + (4 << 20))
    smem = pl.BlockSpec(memory_space=pltpu.SMEM)
    return pl.pallas_call(
        _moba_kernel,
        grid=(b, MOBA_HEADS),
        in_specs=[
            smem, smem, smem,
            pl.BlockSpec((None, None, None, s, hd), lambda bi, h: (bi, 0, h, 0, 0)),
            pl.BlockSpec((None, None, None, s, hd), lambda bi, h: (bi, 1, h, 0, 0)),
            pl.BlockSpec((None, None, None, s, hd), lambda bi, h: (bi, 2, h, 0, 0)),
            pl.BlockSpec((None, s, hd), lambda bi, h: (h, 0, 0)),
        ],
        out_specs=pl.BlockSpec((None, None, s, hd), lambda bi, h: (bi, h, 0, 0)),
        out_shape=jax.ShapeDtypeStruct((b, MOBA_HEADS, s, hd), BF16),
        scratch_shapes=[
            pltpu.VMEM((hd + MOBA_SUM_ROWS, s), BF16),
            pltpu.VMEM((2 * hd, s), BF16),
            pltpu.VMEM((s, 2 * hd), BF16),
            pltpu.VMEM((nb, hd), F32),
            pltpu.VMEM((3 * nb, hd), BF16),
            pltpu.VMEM((nb, blk), F32),
            pltpu.VMEM((nb, hd + MOBA_SUM_ROWS, blk), F32),
            pltpu.VMEM((step_rows, blk), F32),
            pltpu.VMEM((own_units, 1, blk), F32),
            pltpu.VMEM((step_rows, blk), BF16),
            pltpu.VMEM((own_units, 1, blk), F32),
        ],
        compiler_params=pltpu.CompilerParams(
            dimension_semantics=("parallel", "parallel"),
            vmem_limit_bytes=_vmem_limit(vmem)),
        name="moba",
    )(slopes, item_q, item_k, qkv, qkv, qkv, _moba_key_table(slopes, s))


def _merge_kernel(x_ref, oa_ref, ob_ref, ga_ref, gb_ref, wa_ref, wb_ref, wo_ref, o_ref):
    ya = _dot(oa_ref[...], wa_ref[...])
    ob = jnp.concatenate([ob_ref[h] for h in range(MOBA_HEADS)], axis=1)
    yb = _dot(ob, wb_ref[...])
    mix = (jax.nn.sigmoid(ga_ref[...].astype(F32)) * ya
           + jax.nn.sigmoid(gb_ref[...].astype(F32)) * yb)
    o_ref[...] = x_ref[...] + _dot(mix.astype(BF16), wo_ref[...])


def _merge(x, oa, ob, proj, wa, wb, wo):
    t = x.shape[0]
    tm = MERGE_TM
    rpb = ob.shape[2] // tm
    const = dict(pipeline_mode=pl.Buffered(1))
    vmem = (2 * 2 * tm * D_MODEL * 4 + 2 * 2 * tm * GLA_V_W * 2 + 2 * 2 * tm * D_MODEL * 2
            + (2 * GLA_V_W + D_MODEL) * D_MODEL * 2 + 4 * tm * D_MODEL * 4 + (4 << 20))
    return pl.pallas_call(
        _merge_kernel,
        grid=(t // tm,),
        in_specs=[
            pl.BlockSpec((tm, D_MODEL), lambda i: (i, 0)),
            pl.BlockSpec((tm, GLA_V_W), lambda i: (i, 0)),
            pl.BlockSpec((None, MOBA_HEADS, tm, MOBA_HD), lambda i: (i // rpb, 0, i % rpb, 0)),
            pl.BlockSpec((tm, D_MODEL), lambda i: (i, COL_GA // D_MODEL)),
            pl.BlockSpec((tm, D_MODEL), lambda i: (i, COL_GB // D_MODEL)),
            pl.BlockSpec((GLA_V_W, D_MODEL), lambda i: (0, 0), **const),
            pl.BlockSpec((MOBA_W, D_MODEL), lambda i: (0, 0), **const),
            pl.BlockSpec((D_MODEL, D_MODEL), lambda i: (0, 0), **const),
        ],
        out_specs=pl.BlockSpec((tm, D_MODEL), lambda i: (i, 0)),
        out_shape=jax.ShapeDtypeStruct((t, D_MODEL), F32),
        compiler_params=pltpu.CompilerParams(
            dimension_semantics=("parallel",),
            vmem_limit_bytes=_vmem_limit(vmem)),
        name="merge",
    )(x, oa, ob, proj, proj, wa, wb, wo)


def kernel(x, ffn1_norm, ffn1_w_gate, ffn1_w_up, ffn1_w_down, mix_norm, w_in, gla_w_alpha,
           gla_b_alpha, gla_out_norm, w_branch_gla, w_branch_moba, w_out, ffn2_norm,
           ffn2_w_gate, ffn2_w_up, ffn2_w_down, final_norm):
    bsz, seq, d = x.shape
    t = bsz * seq
    depth = ffn1_norm.shape[0]
    assert depth >= 1 and d == D_MODEL and seq % MOBA_BLOCK == 0
    assert seq % (GLA_CHUNK * GLA_STEP_CHUNKS) == 0
    assert t % FFN_TM == 0 and seq % PROJ_TM == 0 and seq % MERGE_TM == 0
    assert OUT_W % PROJ_TN == 0
    assert PROJ_TM % LA_ROWS == 0 and PROJ_TM // LA_ROWS <= PROJ_W // PROJ_TN

    slopes = jnp.exp2(-8.0 * jnp.arange(1, MOBA_HEADS + 1, dtype=F32) / MOBA_HEADS)
    col_scale = jnp.ones((1, PROJ_W), F32)
    col_scale = col_scale.at[:, HEAD_COLS + COL_GQ:HEAD_COLS + COL_GQ + GLA_QK_W].set(GLA_DK ** -0.5)
    col_scale = col_scale.at[:, :MOBA_W].set(MOBA_HD ** -0.5 * LOG2E)
    final_g = final_norm.reshape(1, D_MODEL)

    xf = x.reshape(t, D_MODEL)
    for l in range(depth):
        last = l == depth - 1
        w_l = w_in[l]
        w_b = w_l.astype(BF16)
        w_main = jnp.concatenate([w_b[:, slice(*W_IN_MOBA)], w_b[:, slice(*W_IN_GATES)],
                                  w_b[:, slice(*W_IN_GLA)]], axis=1)
        w_lr = jnp.pad(jnp.tile(w_l[:, slice(*W_IN_LR)], (1, 3)),
                       ((0, 0), (0, LANES_V7X - 3 * GLA_RANK))).astype(BF16)
        wa_hi = _trunc_bf16(gla_w_alpha[l])
        wa_mid = _trunc_bf16(gla_w_alpha[l] - wa_hi)
        w_alpha = jnp.pad(jnp.concatenate([wa_hi, wa_mid, wa_hi], axis=0),
                          ((0, LANES_V7X - 3 * GLA_RANK), (0, 0))).astype(BF16)

        xf = _ffn(xf, ffn1_norm[l].reshape(1, D_MODEL), ffn1_w_gate[l].astype(BF16),
                  ffn1_w_up[l].astype(BF16), ffn1_w_down[l].astype(BF16), final_g, False)
        proj, qkv, log_a = _in_proj(xf, mix_norm[l].reshape(1, D_MODEL), w_main, col_scale,
                                    w_lr, w_alpha, gla_b_alpha[l].reshape(1, GLA_QK_W), bsz)
        proj3 = proj.reshape(bsz, seq, OUT_W)
        oa = _gla(proj3, log_a.reshape(bsz, seq, GLA_QK_W), gla_out_norm[l].reshape(1, GLA_DV))
        ob = _moba(qkv, slopes)
        xf = _merge(xf, oa.reshape(t, GLA_V_W), ob, proj,
                    w_branch_gla[l].astype(BF16), w_branch_moba[l].astype(BF16),
                    w_out[l].astype(BF16))
        xf = _ffn(xf, ffn2_norm[l].reshape(1, D_MODEL), ffn2_w_gate[l].astype(BF16),
                  ffn2_w_up[l].astype(BF16), ffn2_w_down[l].astype(BF16), final_g, last)
    return xf.reshape(bsz, seq, D_MODEL)
```

```python
import functools

import jax
import jax.numpy as jnp
import numpy as np
from jax import lax
from jax.experimental import pallas as pl
from jax.experimental.pallas import tpu as pltpu

F32 = jnp.float32
BF16 = jnp.bfloat16

D_MODEL = 2048
D_FF = 5632
GLA_HEADS = 4
GLA_DK = 128
GLA_DV = 256
GLA_RANK = 16
GLA_TAU = 16.0
MOBA_HEADS = 8
MOBA_HD = 128
MOBA_BLOCK = 256
MOBA_TOPK = 3
NORM_EPS = 1e-6

GLA_QK_W = GLA_HEADS * GLA_DK
GLA_V_W = GLA_HEADS * GLA_DV
MOBA_W = MOBA_HEADS * MOBA_HD

HEAD_COLS = 3 * MOBA_W
COL_GA = 0
COL_GB = COL_GA + D_MODEL
COL_GQ = COL_GB + D_MODEL
COL_GK = COL_GQ + GLA_QK_W
COL_GV = COL_GK + GLA_QK_W
COL_GR = COL_GV + GLA_V_W
OUT_W = COL_GR + GLA_V_W
PROJ_W = HEAD_COLS + OUT_W
W_IN_GLA = (0, 2 * GLA_QK_W + 2 * GLA_V_W)
W_IN_LR = (W_IN_GLA[1], W_IN_GLA[1] + GLA_RANK)
W_IN_MOBA = (W_IN_LR[1], W_IN_LR[1] + 3 * MOBA_W)
W_IN_GATES = (W_IN_MOBA[1], W_IN_MOBA[1] + 2 * D_MODEL)

LANES_V7X = 128
SUBLANES_V7X = 8
VMEM_BYTES_V7X = 64 << 20
VMEM_CAP = VMEM_BYTES_V7X - (6 << 20)

FFN_TM = 1024
FFN_TF = 512
PROJ_TM = 1024
PROJ_TN = MOBA_W
HEAD_TILES = HEAD_COLS // PROJ_TN
LA_ROWS = 128
GLA_CHUNK = 256
GLA_STEP_CHUNKS = 4
GLA_BASE = 2 * SUBLANES_V7X
MERGE_TM = 512
NORM_ROWS = 256
AHEAD_ROWS = 128

MASK_VALUE = -1e30
MOBA_STEP_BLOCKS = 16
MOBA_OWN_STEP_BLOCKS = 16
MOBA_SUM_ROWS = 2 * SUBLANES_V7X
MOBA_GROUP = 4
MOBA_SEL_CHUNK = 1024
MOBA_FINISH_UNROLL = 4
LOG2E = 1.4426950408889634


def _vmem_limit(nbytes):
    return int(min(VMEM_CAP, nbytes))


def _dot(a, b):
    return jnp.dot(a, b, preferred_element_type=F32)


def _dot_nt(a, b, precision=None):
    return lax.dot_general(a, b, (((1,), (1,)), ((), ())),
                           preferred_element_type=F32, precision=precision)


def _dot_tn(a, b):
    return lax.dot_general(a, b, (((0,), (0,)), ((), ())), preferred_element_type=F32)


def _rms_norm_slice(x_ref, g_ref, out_ref, sl):
    x = x_ref[sl, :].astype(F32)
    ms = jnp.mean(x * x, axis=-1, keepdims=True)
    out_ref[sl, :] = (x * lax.rsqrt(ms + NORM_EPS) * g_ref[...]).astype(out_ref.dtype)


def _rms_norm_rows(x_ref, g_ref, out_ref, rows):
    def body(r, carry):
        _rms_norm_slice(x_ref, g_ref, out_ref,
                        pl.ds(pl.multiple_of(r * NORM_ROWS, NORM_ROWS), NORM_ROWS))
        return carry
    lax.fori_loop(0, rows // NORM_ROWS, body, 0)


def _next_row_index(i, j, n_rows):
    return jnp.minimum(i + jnp.minimum(j, 1), n_rows - 1)


def _with_norm_ahead(x_ref, g_ref, xn_even_ref, xn_odd_ref, rows, step):
    i = pl.program_id(0)
    j = pl.program_id(1)

    @pl.when(jnp.logical_and(i == 0, j == 0))
    def _():
        _rms_norm_rows(x_ref, g_ref, xn_even_ref, rows)

    r = jnp.clip(j - 1, 0, rows // AHEAD_ROWS - 1)
    sl = pl.ds(pl.multiple_of(r * AHEAD_ROWS, AHEAD_ROWS), AHEAD_ROWS)
    for parity, cur, nxt in ((0, xn_even_ref, xn_odd_ref), (1, xn_odd_ref, xn_even_ref)):
        @pl.when(i % 2 == parity)
        def _(cur=cur, nxt=nxt):
            step(cur, functools.partial(_rms_norm_slice, x_ref, g_ref, nxt, sl))


def _ffn_kernel(x_ref, g_ref, wg_ref, wu_ref, wd_ref, fg_ref, o_ref, xn_ref, *, final_norm):
    j = pl.program_id(1)

    @pl.when(j == 0)
    def _():
        _rms_norm_rows(x_ref, g_ref, xn_ref, FFN_TM)
        o_ref[...] = x_ref[...]

    xn = xn_ref[...]
    hg = _dot(xn, wg_ref[...])
    hu = _dot(xn, wu_ref[...])
    a = (0.5 * hg * jax.nn.sigmoid(hg)) * hu
    o_ref[...] += _dot(a.astype(BF16), wd_ref[...])

    if final_norm:
        @pl.when(j == pl.num_programs(1) - 1)
        def _():
            _rms_norm_rows(o_ref, fg_ref, o_ref, FFN_TM)


def _ffn(x, norm_g, wg, wu, wd, final_g, final_norm):
    t = x.shape[0]
    vmem = (2 * 2 * FFN_TM * D_MODEL * 4
            + FFN_TM * D_MODEL * 2
            + 2 * 3 * D_MODEL * FFN_TF * 2
            + 4 * FFN_TM * FFN_TF * 4
            + (4 << 20))
    return pl.pallas_call(
        functools.partial(_ffn_kernel, final_norm=final_norm),
        grid=(t // FFN_TM, D_FF // FFN_TF),
        in_specs=[
            pl.BlockSpec((FFN_TM, D_MODEL), lambda i, j: (i, 0)),
            pl.BlockSpec((1, D_MODEL), lambda i, j: (0, 0)),
            pl.BlockSpec((D_MODEL, FFN_TF), lambda i, j: (0, j)),
            pl.BlockSpec((D_MODEL, FFN_TF), lambda i, j: (0, j)),
            pl.BlockSpec((FFN_TF, D_MODEL), lambda i, j: (j, 0)),
            pl.BlockSpec((1, D_MODEL), lambda i, j: (0, 0)),
        ],
        out_specs=pl.BlockSpec((FFN_TM, D_MODEL), lambda i, j: (i, 0)),
        out_shape=jax.ShapeDtypeStruct((t, D_MODEL), F32),
        scratch_shapes=[pltpu.VMEM((FFN_TM, D_MODEL), BF16)],
        compiler_params=pltpu.CompilerParams(
            dimension_semantics=("parallel", "arbitrary"),
            vmem_limit_bytes=_vmem_limit(vmem)),
        name="ffn_final" if final_norm else "ffn",
    )(x, norm_g, wg, wu, wd, final_g)


def _log_sigmoid(z):
    return jnp.minimum(z, 0.0) - jnp.log1p(jnp.exp(-jnp.abs(z)))


def _in_proj_kernel(x_ref, g_ref, w_ref, cs_ref, wlr_ref, wa_ref, ba_ref,
                    proj_ref, heads_ref, la_ref, xn_even_ref, xn_odd_ref, lr_ref):
    j = pl.program_id(1)

    def step(xn_ref, norm_next):
        @pl.when(j == 0)
        def _():
            lr_ref[...] = _dot(xn_ref[...], wlr_ref[...])

        norm_next()

        rows = pl.ds(pl.multiple_of(jnp.minimum(j, PROJ_TM // LA_ROWS - 1) * LA_ROWS, LA_ROWS),
                     LA_ROWS)
        lr = lr_ref[rows, :]
        lr_hi = lr.astype(BF16).astype(F32)
        lane = lax.broadcasted_iota(jnp.int32, lr.shape, 1)
        packed = jnp.where(lane < 2 * GLA_RANK, lr_hi, lr - lr_hi).astype(BF16)
        z = _dot(packed, wa_ref[...]) + ba_ref[...]
        la_ref[rows, :] = _log_sigmoid(z) * (1.0 / GLA_TAU)

        res = (_dot(xn_ref[...], w_ref[...]) * cs_ref[...]).astype(BF16)
        proj_ref[...] = res
        for h in range(MOBA_HEADS):
            heads_ref[h] = res[:, h * MOBA_HD:(h + 1) * MOBA_HD]

    _with_norm_ahead(x_ref, g_ref, xn_even_ref, xn_odd_ref, PROJ_TM, step)


def _in_proj(x, norm_g, w_main, col_scale, w_lr, w_alpha, b_alpha, bsz):
    t = x.shape[0]
    rpb = t // bsz // PROJ_TM
    assert PROJ_W // PROJ_TN > PROJ_TM // AHEAD_ROWS and 3 * GLA_RANK <= LANES_V7X
    vmem = (2 * PROJ_TM * D_MODEL * 4 + 2 * PROJ_TM * D_MODEL * 2
            + 2 * D_MODEL * PROJ_TN * 2 + 2 * 2 * PROJ_TM * PROJ_TN * 2
            + 2 * PROJ_TM * GLA_QK_W * 4 + 2 * PROJ_TM * PROJ_TN * 4
            + 2 * D_MODEL * LANES_V7X * 2 + PROJ_TM * LANES_V7X * 4 + (4 << 20))
    return pl.pallas_call(
        _in_proj_kernel,
        grid=(t // PROJ_TM, PROJ_W // PROJ_TN),
        in_specs=[
            pl.BlockSpec((PROJ_TM, D_MODEL), lambda i, j: (_next_row_index(i, j, t // PROJ_TM), 0)),
            pl.BlockSpec((1, D_MODEL), lambda i, j: (0, 0)),
            pl.BlockSpec((D_MODEL, PROJ_TN), lambda i, j: (0, j)),
            pl.BlockSpec((1, PROJ_TN), lambda i, j: (0, j)),
            pl.BlockSpec((D_MODEL, LANES_V7X), lambda i, j: (0, 0)),
            pl.BlockSpec((LANES_V7X, GLA_QK_W), lambda i, j: (0, 0)),
            pl.BlockSpec((1, GLA_QK_W), lambda i, j: (0, 0)),
        ],
        out_specs=[
            pl.BlockSpec((PROJ_TM, PROJ_TN), lambda i, j: (i, jnp.maximum(j - HEAD_TILES, 0))),
            pl.BlockSpec((None, None, MOBA_HEADS, PROJ_TM, MOBA_HD),
                         lambda i, j: (i // rpb, jnp.minimum(j, HEAD_TILES), 0, i % rpb, 0)),
            pl.BlockSpec((PROJ_TM, GLA_QK_W), lambda i, j: (i, 0)),
        ],
        out_shape=[
            jax.ShapeDtypeStruct((t, OUT_W), BF16),
            jax.ShapeDtypeStruct((bsz, HEAD_TILES + 1, MOBA_HEADS, t // bsz, MOBA_HD), BF16),
            jax.ShapeDtypeStruct((t, GLA_QK_W), F32),
        ],
        scratch_shapes=[pltpu.VMEM((PROJ_TM, D_MODEL), BF16),
                        pltpu.VMEM((PROJ_TM, D_MODEL), BF16),
                        pltpu.VMEM((PROJ_TM, LANES_V7X), F32)],
        compiler_params=pltpu.CompilerParams(
            dimension_semantics=("arbitrary", "arbitrary"),
            vmem_limit_bytes=_vmem_limit(vmem)),
        name="in_proj",
    )(x, norm_g, w_main, col_scale, w_lr, w_alpha, b_alpha)


def _bcast_rows(v, s, c):
    return jnp.broadcast_to(v, (c // s, s, GLA_DK)).reshape(c, GLA_DK)


def _gla_head(q, k, v, la, gr, gn, st_ref):
    c = q.shape[0]
    row = lax.broadcasted_iota(jnp.int32, (c, 1), 0)
    ri = lax.broadcasted_iota(jnp.int32, (c, c), 0)
    ci = lax.broadcasted_iota(jnp.int32, (c, c), 1)
    rxc = ri ^ ci

    def next_level(cs, s):
        ends = cs.reshape(c // s, s, GLA_DK)[:, s - 1:s, :]
        prev = jnp.concatenate([jnp.zeros_like(ends[:1]), ends[:-1]], axis=0)
        odd = ((row // s) & 1) == 1
        return ends, odd, cs + jnp.where(odd, _bcast_rows(prev, s, c), 0.0)

    cs = la.reshape(c // SUBLANES_V7X, SUBLANES_V7X, GLA_DK)
    sub = lax.broadcasted_iota(jnp.int32, cs.shape, 1)
    sh = 1
    while sh < SUBLANES_V7X:
        cs = cs + jnp.where(sub >= sh, pltpu.roll(cs, sh, axis=1), 0.0)
        sh *= 2
    cs = cs.reshape(c, GLA_DK)
    s = SUBLANES_V7X
    while s < GLA_BASE:
        _, _, cs = next_level(cs, s)
        s *= 2

    p = _dot_nt((q * jnp.exp(cs)).astype(BF16), (k * jnp.exp(-cs)).astype(BF16))
    attn = jnp.where(ci <= ri, p, 0.0)
    while s < c:
        ends, odd, cs_next = next_level(cs, s)
        qe = jnp.where(odd, jnp.exp(cs), 0.0)
        ke = jnp.where(odd, 0.0, jnp.exp(_bcast_rows(ends, s, c) - cs))
        p = _dot_nt((q * qe).astype(BF16), (k * ke).astype(BF16))
        attn = jnp.where(rxc < s, attn, p)
        cs = cs_next
        s *= 2

    b_last = cs[c - 1:c, :]
    st = st_ref[...]
    o = _dot_nt((q * jnp.exp(cs)).astype(BF16), st.astype(BF16))
    o = o + _dot(attn.astype(BF16), v)
    kd = (k * jnp.exp(b_last - cs)).astype(BF16)
    st_ref[...] = st * jnp.exp(b_last) + _dot_tn(v, kd)

    ms = jnp.mean(o * o, axis=-1, keepdims=True)
    y = o * lax.rsqrt(ms + NORM_EPS) * gn
    g = gr.astype(F32)
    return y * (g * jax.nn.sigmoid(g))


def _gla_kernel(q_ref, k_ref, v_ref, gr_ref, la_ref, gn_ref, o_ref, st_ref):
    @pl.when(pl.program_id(1) == 0)
    def _():
        st_ref[...] = jnp.zeros_like(st_ref)

    gn = gn_ref[...]
    for n in range(GLA_STEP_CHUNKS):
        rs = slice(n * GLA_CHUNK, (n + 1) * GLA_CHUNK)
        for h in range(GLA_HEADS):
            ks = slice(h * GLA_DK, (h + 1) * GLA_DK)
            vs = slice(h * GLA_DV, (h + 1) * GLA_DV)
            out = _gla_head(q_ref[rs, ks].astype(F32), k_ref[rs, ks].astype(F32), v_ref[rs, vs],
                            la_ref[rs, ks], gr_ref[rs, vs], gn, st_ref.at[h])
            o_ref[rs, vs] = out.astype(o_ref.dtype)


def _gla(proj3, la3, gn):
    b, s, _ = proj3.shape
    c = GLA_CHUNK * GLA_STEP_CHUNKS
    vmem = (2 * c * (2 * GLA_QK_W + 3 * GLA_V_W) * 2 + 2 * c * GLA_QK_W * 4
            + GLA_HEADS * GLA_DV * GLA_DK * 4
            + 16 * GLA_CHUNK * GLA_CHUNK * 4 + (4 << 20))
    return pl.pallas_call(
        _gla_kernel,
        grid=(b, s // c),
        in_specs=[
            pl.BlockSpec((None, c, GLA_QK_W), lambda i, j: (i, j, COL_GQ // GLA_QK_W)),
            pl.BlockSpec((None, c, GLA_QK_W), lambda i, j: (i, j, COL_GK // GLA_QK_W)),
            pl.BlockSpec((None, c, GLA_V_W), lambda i, j: (i, j, COL_GV // GLA_V_W)),
            pl.BlockSpec((None, c, GLA_V_W), lambda i, j: (i, j, COL_GR // GLA_V_W)),
            pl.BlockSpec((None, c, GLA_QK_W), lambda i, j: (i, j, 0)),
            pl.BlockSpec((1, GLA_DV), lambda i, j: (0, 0)),
        ],
        out_specs=pl.BlockSpec((None, c, GLA_V_W), lambda i, j: (i, j, 0)),
        out_shape=jax.ShapeDtypeStruct((b, s, GLA_V_W), BF16),
        scratch_shapes=[pltpu.VMEM((GLA_HEADS, GLA_DV, GLA_DK), F32)],
        compiler_params=pltpu.CompilerParams(
            dimension_semantics=("parallel", "arbitrary"),
            vmem_limit_bytes=_vmem_limit(vmem)),
        name="gla",
    )(proj3, proj3, proj3, proj3, la3, gn)


def _split3_bf16(z):
    hi = z.astype(BF16).astype(F32)
    r = z - hi
    mid = r.astype(BF16).astype(F32)
    lo = (r - mid).astype(BF16).astype(F32)
    return hi, mid, lo


def _trunc_bf16(x):
    bits = lax.bitcast_convert_type(x, jnp.uint32) & jnp.uint32(0xFFFF0000)
    return lax.bitcast_convert_type(bits, F32)


def _moba_key_table(slopes, seq):
    nb = seq // MOBA_BLOCK
    pos = jnp.arange(seq)
    z = (slopes * LOG2E)[:, None] * pos.astype(F32)[None, :]
    hi = _trunc_bf16(z)
    mid = _trunc_bf16(z - hi)
    lo = _trunc_bf16(z - hi - mid)
    lane = jnp.arange(MOBA_HD)
    onehot = (lane[None, :] == (pos // MOBA_BLOCK)[:, None]).astype(F32)
    table = (onehot[None] + hi[..., None] * (lane == nb) + mid[..., None] * (lane == nb + 1)
             + lo[..., None] * (lane == nb + 2))
    return table.astype(BF16)


def _moba_kernel(item_q_ref, item_k_ref, q_ref, k_ref, v_ref, kx_ref, o_ref,
                 vt_ref, qt_ref, ka_ref, kmean_ref, kms_ref, m_ref, acc_ref,
                 s_ref, smax_ref, p_ref, al_ref):
    blk = MOBA_BLOCK
    hd = MOBA_HD
    seq = k_ref.shape[0]
    nb = seq // blk

    def prep(j, carry):
        sl = pl.ds(pl.multiple_of(j * blk, blk), blk)
        kj = k_ref[sl, :]
        vt_ref[0:hd, sl] = v_ref[sl, :].T
        qt_ref[0:hd, sl] = q_ref[sl, :].T
        ka_ref[sl, 0:hd] = kj
        ka_ref[sl, hd:2 * hd] = kx_ref[sl, :]
        kmean_ref[pl.ds(j, 1), :] = jnp.mean(kj.astype(F32), axis=0, keepdims=True)
        return carry
    lax.fori_loop(0, nb, prep, 0)
    qt_ref[hd + nb:2 * hd, :] = jnp.ones((hd - nb, seq), BF16)
    kms_ref[...] = jnp.concatenate(_split3_bf16(kmean_ref[...]), axis=0).astype(BF16)

    chunk = MOBA_SEL_CHUNK
    for c in range(seq // chunk):
        live = min(nb, -(-((c + 1) * (chunk // blk) - 1) // SUBLANES_V7X) * SUBLANES_V7X)
        live = max(live, SUBLANES_V7X)
        cs = slice(c * chunk, (c + 1) * chunk)
        g3 = _dot(kms_ref[...], qt_ref[0:hd, cs])
        gate = g3[0:live] + g3[nb:nb + live] + g3[2 * nb:2 * nb + live]
        bidx = lax.broadcasted_iota(jnp.int32, gate.shape, 0)
        qblk = c * (chunk // blk) + lax.broadcasted_iota(jnp.int32, gate.shape, 1) // blk
        gate = jnp.where(bidx < qblk, gate, -jnp.inf)
        mask = jnp.full(gate.shape, MASK_VALUE, F32)
        for _ in range(MOBA_TOPK):
            m = jnp.max(gate, axis=0, keepdims=True)
            first = jnp.min(jnp.where(gate == m, bidx, nb), axis=0, keepdims=True)
            pick = bidx == jnp.where(m > -jnp.inf, first, -1)
            mask = jnp.where(pick, 0.0, mask)
            gate = jnp.where(pick, -jnp.inf, gate)
        if live < nb:
            mask = jnp.concatenate([mask, jnp.full((nb - live, chunk), MASK_VALUE, F32)], axis=0)
        qt_ref[hd:hd + nb, cs] = mask.astype(BF16)

    m_ref[...] = jnp.full(m_ref.shape, MASK_VALUE, F32)
    ones_row = lax.broadcasted_iota(jnp.int32, (MOBA_SUM_ROWS, seq), 0) == 0
    vt_ref[hd:hd + MOBA_SUM_ROWS, :] = jnp.where(ones_row, 1.0, 0.0).astype(BF16)

    def clear(i, carry):
        acc_ref[i] = jnp.zeros((hd + MOBA_SUM_ROWS, blk), F32)
        return carry
    lax.fori_loop(0, nb, clear, 0)

    causal = (lax.broadcasted_iota(jnp.int32, (blk, blk), 0)
              <= lax.broadcasted_iota(jnp.int32, (blk, blk), 1))
    tail_row = lax.broadcasted_iota(jnp.int32, (hd, blk), 0)
    own_tail = jnp.where(tail_row < nb, 0.0, 1.0).astype(BF16)

    def make_stages(own_block):
        rows = blk if own_block else MOBA_GROUP * blk
        units = (MOBA_OWN_STEP_BLOCKS if own_block else MOBA_STEP_BLOCKS) * blk // rows

        def item(n):
            if own_block:
                i = j = n
            else:
                i = item_q_ref[n]
                j = item_k_ref[n]
            return (i, pl.ds(pl.multiple_of(i * blk, blk), blk),
                    pl.ds(pl.multiple_of(j * blk, blk), rows))

        def scores(n0):
            for u in range(units):
                i, qs, ks = item(n0 + u)
                if own_block:
                    rhs = jnp.concatenate([qt_ref[0:hd, qs], own_tail], axis=0)
                    s = jnp.where(causal, _dot(ka_ref[ks, :], rhs), MASK_VALUE)
                else:
                    s = _dot(ka_ref[ks, :], qt_ref[:, qs])
                s_ref[u * rows:(u + 1) * rows, :] = s
                smax_ref[u] = jnp.max(s, axis=0, keepdims=True)

        def softmax(n0):
            for u in range(units):
                i, _, _ = item(n0 + u)
                row = pl.ds(i, 1)
                m_old = m_ref[row, :]
                m_new = jnp.maximum(m_old, smax_ref[u])
                alpha = jnp.exp2(m_old - m_new)
                p = jnp.exp2(s_ref[u * rows:(u + 1) * rows, :] - m_new)
                m_ref[row, :] = m_new
                p_ref[u * rows:(u + 1) * rows, :] = p.astype(BF16)
                al_ref[u] = alpha

        def values(n0):
            for u in range(units):
                i, _, ks = item(n0 + u)
                acc_ref[i] = (al_ref[u] * acc_ref[i]
                              + _dot(vt_ref[:, ks], p_ref[u * rows:(u + 1) * rows, :]))

        return units, scores, softmax, values

    def run_pipeline(n_items, stages):
        units, scores, softmax, values = stages
        steps = n_items // units
        scores(0)
        softmax(0)
        scores(units)

        def body(t, carry):
            values(t * units)
            softmax((t + 1) * units)
            scores((t + 2) * units)
            return carry
        lax.fori_loop(0, steps - 2, body, 0)
        values((steps - 2) * units)
        softmax((steps - 1) * units)
        values((steps - 1) * units)

    run_pipeline(nb, make_stages(True))
    run_pipeline(item_q_ref.shape[0], make_stages(False))

    def finish(i0, carry):
        for u in range(MOBA_FINISH_UNROLL):
            i = i0 * MOBA_FINISH_UNROLL + u
            qs = pl.ds(pl.multiple_of(i * blk, blk), blk)
            acc = acc_ref[i]
            inv = 1.0 / acc[hd:hd + 1, :]
            o_ref[qs, :] = (acc[0:hd] * inv).T.astype(o_ref.dtype)
        return carry
    lax.fori_loop(0, nb // MOBA_FINISH_UNROLL, finish, 0)


def _moba(qkv, slopes):
    b, _, _, s, _ = qkv.shape
    blk = MOBA_BLOCK
    nb = s // blk
    hd = MOBA_HD
    assert nb + 3 <= hd and s % MOBA_SEL_CHUNK == 0
    groups = [(i, j) for i in range(nb) for j in range(0, i, MOBA_GROUP)]
    own_units = MOBA_OWN_STEP_BLOCKS
    grp_units = MOBA_STEP_BLOCKS // MOBA_GROUP
    assert MOBA_STEP_BLOCKS % MOBA_GROUP == 0 and nb % MOBA_GROUP == 0
    assert MOBA_OWN_STEP_BLOCKS <= MOBA_STEP_BLOCKS and grp_units <= own_units
    assert nb % own_units == 0 and nb // own_units >= 2
    assert len(groups) % grp_units == 0 and len(groups) // grp_units >= 2
    assert nb % MOBA_FINISH_UNROLL == 0
    item_q = jnp.asarray(np.array([g[0] for g in groups], np.int32))
    item_k = jnp.asarray(np.array([g[1] for g in groups], np.int32))
    step_rows = MOBA_STEP_BLOCKS * blk
    vmem = (2 * 5 * s * hd * 2
            + s * (hd + MOBA_SUM_ROWS) * 2 + 2 * 2 * s * hd * 2
            + nb * (hd + MOBA_SUM_ROWS) * blk * 4
            + step_rows * blk * 6
            + 2 * step_rows * blk * 4 + (4 << 20))
    smem = pl.BlockSpec(memory_space=pltpu.SMEM)
    return pl.pallas_call(
        _moba_kernel,
        grid=(b, MOBA_HEADS),
        in_specs=[
            smem, smem,
            pl.BlockSpec((None, None, None, s, hd), lambda bi, h: (bi, 0, h, 0, 0)),
            pl.BlockSpec((None, None, None, s, hd), lambda bi, h: (bi, 1, h, 0, 0)),
            pl.BlockSpec((None, None, None, s, hd), lambda bi, h: (bi, 2, h, 0, 0)),
            pl.BlockSpec((None, s, hd), lambda bi, h: (h, 0, 0)),
        ],
        out_specs=pl.BlockSpec((None, None, s, hd), lambda bi, h: (bi, h, 0, 0)),
        out_shape=jax.ShapeDtypeStruct((b, MOBA_HEADS, s, hd), BF16),
        scratch_shapes=[
            pltpu.VMEM((hd + MOBA_SUM_ROWS, s), BF16),
            pltpu.VMEM((2 * hd, s), BF16),
            pltpu.VMEM((s, 2 * hd), BF16),
            pltpu.VMEM((nb, hd), F32),
            pltpu.VMEM((3 * nb, hd), BF16),
            pltpu.VMEM((nb, blk), F32),
            pltpu.VMEM((nb, hd + MOBA_SUM_ROWS, blk), F32),
            pltpu.VMEM((step_rows, blk), F32),
            pltpu.VMEM((own_units, 1, blk), F32),
            pltpu.VMEM((step_rows, blk), BF16),
            pltpu.VMEM((own_units, 1, blk), F32),
        ],
        compiler_params=pltpu.CompilerParams(
            dimension_semantics=("parallel", "parallel"),
            vmem_limit_bytes=_vmem_limit(vmem)),
        name="moba",
    )(item_q, item_k, qkv, qkv, qkv, _moba_key_table(slopes, s))


def _merge_kernel(x_ref, oa_ref, ob_ref, ga_ref, gb_ref, wa_ref, wb_ref, wo_ref, o_ref):
    ya = _dot(oa_ref[...], wa_ref[...])
    ob = jnp.concatenate([ob_ref[h] for h in range(MOBA_HEADS)], axis=1)
    yb = _dot(ob, wb_ref[...])
    mix = (jax.nn.sigmoid(ga_ref[...].astype(F32)) * ya
           + jax.nn.sigmoid(gb_ref[...].astype(F32)) * yb)
    o_ref[...] = x_ref[...] + _dot(mix.astype(BF16), wo_ref[...])


def _merge(x, oa, ob, proj, wa, wb, wo):
    t = x.shape[0]
    tm = MERGE_TM
    rpb = ob.shape[2] // tm
    const = dict(pipeline_mode=pl.Buffered(1))
    vmem = (2 * 2 * tm * D_MODEL * 4 + 2 * 2 * tm * GLA_V_W * 2 + 2 * 2 * tm * D_MODEL * 2
            + (2 * GLA_V_W + D_MODEL) * D_MODEL * 2 + 4 * tm * D_MODEL * 4 + (4 << 20))
    return pl.pallas_call(
        _merge_kernel,
        grid=(t // tm,),
        in_specs=[
            pl.BlockSpec((tm, D_MODEL), lambda i: (i, 0)),
            pl.BlockSpec((tm, GLA_V_W), lambda i: (i, 0)),
            pl.BlockSpec((None, MOBA_HEADS, tm, MOBA_HD), lambda i: (i // rpb, 0, i % rpb, 0)),
            pl.BlockSpec((tm, D_MODEL), lambda i: (i, COL_GA // D_MODEL)),
            pl.BlockSpec((tm, D_MODEL), lambda i: (i, COL_GB // D_MODEL)),
            pl.BlockSpec((GLA_V_W, D_MODEL), lambda i: (0, 0), **const),
            pl.BlockSpec((MOBA_W, D_MODEL), lambda i: (0, 0), **const),
            pl.BlockSpec((D_MODEL, D_MODEL), lambda i: (0, 0), **const),
        ],
        out_specs=pl.BlockSpec((tm, D_MODEL), lambda i: (i, 0)),
        out_shape=jax.ShapeDtypeStruct((t, D_MODEL), F32),
        compiler_params=pltpu.CompilerParams(
            dimension_semantics=("parallel",),
            vmem_limit_bytes=_vmem_limit(vmem)),
        name="merge",
    )(x, oa, ob, proj, proj, wa, wb, wo)


def kernel(x, ffn1_norm, ffn1_w_gate, ffn1_w_up, ffn1_w_down, mix_norm, w_in, gla_w_alpha,
           gla_b_alpha, gla_out_norm, w_branch_gla, w_branch_moba, w_out, ffn2_norm,
           ffn2_w_gate, ffn2_w_up, ffn2_w_down, final_norm):
    bsz, seq, d = x.shape
    t = bsz * seq
    depth = ffn1_norm.shape[0]
    assert depth >= 1 and d == D_MODEL and seq % MOBA_BLOCK == 0
    assert seq % (GLA_CHUNK * GLA_STEP_CHUNKS) == 0
    assert t % FFN_TM == 0 and seq % PROJ_TM == 0 and seq % MERGE_TM == 0
    assert OUT_W % PROJ_TN == 0
    assert PROJ_TM % LA_ROWS == 0 and PROJ_TM // LA_ROWS <= PROJ_W // PROJ_TN

    slopes = jnp.exp2(-8.0 * jnp.arange(1, MOBA_HEADS + 1, dtype=F32) / MOBA_HEADS)
    col_scale = jnp.ones((1, PROJ_W), F32)
    col_scale = col_scale.at[:, HEAD_COLS + COL_GQ:HEAD_COLS + COL_GQ + GLA_QK_W].set(GLA_DK ** -0.5)
    col_scale = col_scale.at[:, :MOBA_W].set(MOBA_HD ** -0.5 * LOG2E)
    final_g = final_norm.reshape(1, D_MODEL)

    xf = x.reshape(t, D_MODEL)
    for l in range(depth):
        last = l == depth - 1
        w_l = w_in[l]
        w_b = w_l.astype(BF16)
        w_main = jnp.concatenate([w_b[:, slice(*W_IN_MOBA)], w_b[:, slice(*W_IN_GATES)],
                                  w_b[:, slice(*W_IN_GLA)]], axis=1)
        w_lr = jnp.pad(jnp.tile(w_l[:, slice(*W_IN_LR)], (1, 3)),
                       ((0, 0), (0, LANES_V7X - 3 * GLA_RANK))).astype(BF16)
        wa_hi = _trunc_bf16(gla_w_alpha[l])
        wa_mid = _trunc_bf16(gla_w_alpha[l] - wa_hi)
        w_alpha = jnp.pad(jnp.concatenate([wa_hi, wa_mid, wa_hi], axis=0),
                          ((0, LANES_V7X - 3 * GLA_RANK), (0, 0))).astype(BF16)

        xf = _ffn(xf, ffn1_norm[l].reshape(1, D_MODEL), ffn1_w_gate[l].astype(BF16),
                  ffn1_w_up[l].astype(BF16), ffn1_w_down[l].astype(BF16), final_g, False)
        proj, qkv, log_a = _in_proj(xf, mix_norm[l].reshape(1, D_MODEL), w_main, col_scale,
                                    w_lr, w_alpha, gla_b_alpha[l].reshape(1, GLA_QK_W), bsz)
        proj3 = proj.reshape(bsz, seq, OUT_W)
        oa = _gla(proj3, log_a.reshape(bsz, seq, GLA_QK_W), gla_out_norm[l].reshape(1, GLA_DV))
        ob = _moba(qkv, slopes)
        xf = _merge(xf, oa.reshape(t, GLA_V_W), ob, proj,
                    w_branch_gla[l].astype(BF16), w_branch_moba[l].astype(BF16),
                    w_out[l].astype(BF16))
        xf = _ffn(xf, ffn2_norm[l].reshape(1, D_MODEL), ffn2_w_gate[l].astype(BF16),
                  ffn2_w_up[l].astype(BF16), ffn2_w_down[l].astype(BF16), final_g, last)
    return xf.reshape(bsz, seq, D_MODEL)
```

```python
import functools

import jax
import jax.numpy as jnp
import numpy as np
from jax import lax
from jax.experimental import pallas as pl
from jax.experimental.pallas import tpu as pltpu

F32 = jnp.float32
BF16 = jnp.bfloat16

D_MODEL = 2048
D_FF = 5632
GLA_HEADS = 4
GLA_DK = 128
GLA_DV = 256
GLA_RANK = 16
GLA_TAU = 16.0
MOBA_HEADS = 8
MOBA_HD = 128
MOBA_BLOCK = 256
MOBA_TOPK = 3
NORM_EPS = 1e-6

GLA_QK_W = GLA_HEADS * GLA_DK
GLA_V_W = GLA_HEADS * GLA_DV
MOBA_W = MOBA_HEADS * MOBA_HD

HEAD_COLS = 3 * MOBA_W
COL_GA = 0
COL_GB = COL_GA + D_MODEL
COL_GQ = COL_GB + D_MODEL
COL_GK = COL_GQ + GLA_QK_W
COL_GV = COL_GK + GLA_QK_W
COL_GR = COL_GV + GLA_V_W
OUT_W = COL_GR + GLA_V_W
PROJ_W = HEAD_COLS + OUT_W
W_IN_GLA = (0, 2 * GLA_QK_W + 2 * GLA_V_W)
W_IN_LR = (W_IN_GLA[1], W_IN_GLA[1] + GLA_RANK)
W_IN_MOBA = (W_IN_LR[1], W_IN_LR[1] + 3 * MOBA_W)
W_IN_GATES = (W_IN_MOBA[1], W_IN_MOBA[1] + 2 * D_MODEL)

LANES_V7X = 128
SUBLANES_V7X = 8
VMEM_BYTES_V7X = 64 << 20
VMEM_CAP = VMEM_BYTES_V7X - (6 << 20)

FFN_TM = 1024
FFN_TF = 512
PROJ_TM = 1024
PROJ_TN = MOBA_W
HEAD_TILES = HEAD_COLS // PROJ_TN
LA_ROWS = 128
GLA_CHUNK = 256
GLA_STEP_CHUNKS = 2
GLA_BASE = 2 * SUBLANES_V7X
MERGE_TM = 512
NORM_ROWS = 256
AHEAD_ROWS = 128

MASK_VALUE = -1e30
MOBA_STEP_BLOCKS = 16
MOBA_OWN_STEP_BLOCKS = 16
MOBA_SUM_ROWS = 2 * SUBLANES_V7X
MOBA_GROUP = 4
MOBA_SEL_CHUNK = 1024
MOBA_FINISH_UNROLL = 4
LOG2E = 1.4426950408889634


def _vmem_limit(nbytes):
    return int(min(VMEM_CAP, nbytes))


def _dot(a, b):
    return jnp.dot(a, b, preferred_element_type=F32)


def _dot_nt(a, b):
    return lax.dot_general(a, b, (((1,), (1,)), ((), ())), preferred_element_type=F32)


def _dot_tn(a, b):
    return lax.dot_general(a, b, (((0,), (0,)), ((), ())), preferred_element_type=F32)


def _rms_norm_slice(x_ref, g_ref, out_ref, sl):
    x = x_ref[sl, :].astype(F32)
    ms = jnp.mean(x * x, axis=-1, keepdims=True)
    out_ref[sl, :] = (x * lax.rsqrt(ms + NORM_EPS) * g_ref[...]).astype(out_ref.dtype)


def _rms_norm_rows(x_ref, g_ref, out_ref, rows):
    def body(r, carry):
        _rms_norm_slice(x_ref, g_ref, out_ref,
                        pl.ds(pl.multiple_of(r * NORM_ROWS, NORM_ROWS), NORM_ROWS))
        return carry
    lax.fori_loop(0, rows // NORM_ROWS, body, 0)


def _next_row_index(i, j, n_rows):
    return jnp.minimum(i + jnp.minimum(j, 1), n_rows - 1)


def _with_norm_ahead(x_ref, g_ref, xn_even_ref, xn_odd_ref, rows, step):
    i = pl.program_id(0)
    j = pl.program_id(1)

    @pl.when(jnp.logical_and(i == 0, j == 0))
    def _():
        _rms_norm_rows(x_ref, g_ref, xn_even_ref, rows)

    r = jnp.clip(j - 1, 0, rows // AHEAD_ROWS - 1)
    sl = pl.ds(pl.multiple_of(r * AHEAD_ROWS, AHEAD_ROWS), AHEAD_ROWS)
    for parity, cur, nxt in ((0, xn_even_ref, xn_odd_ref), (1, xn_odd_ref, xn_even_ref)):
        @pl.when(i % 2 == parity)
        def _(cur=cur, nxt=nxt):
            step(cur, functools.partial(_rms_norm_slice, x_ref, g_ref, nxt, sl))


def _ffn_kernel(x_ref, g_ref, wg_ref, wu_ref, wd_ref, fg_ref, o_ref, xn_ref, *, final_norm):
    j = pl.program_id(1)

    @pl.when(j == 0)
    def _():
        _rms_norm_rows(x_ref, g_ref, xn_ref, FFN_TM)
        o_ref[...] = x_ref[...]

    xn = xn_ref[...]
    hg = _dot(xn, wg_ref[...])
    hu = _dot(xn, wu_ref[...])
    a = (0.5 * hg * jax.nn.sigmoid(hg)) * hu
    o_ref[...] += _dot(a.astype(BF16), wd_ref[...])

    if final_norm:
        @pl.when(j == pl.num_programs(1) - 1)
        def _():
            _rms_norm_rows(o_ref, fg_ref, o_ref, FFN_TM)


def _ffn(x, norm_g, wg, wu, wd, final_g, final_norm):
    t = x.shape[0]
    vmem = (2 * 2 * FFN_TM * D_MODEL * 4
            + FFN_TM * D_MODEL * 2
            + 2 * 3 * D_MODEL * FFN_TF * 2
            + 4 * FFN_TM * FFN_TF * 4
            + (4 << 20))
    return pl.pallas_call(
        functools.partial(_ffn_kernel, final_norm=final_norm),
        grid=(t // FFN_TM, D_FF // FFN_TF),
        in_specs=[
            pl.BlockSpec((FFN_TM, D_MODEL), lambda i, j: (i, 0)),
            pl.BlockSpec((1, D_MODEL), lambda i, j: (0, 0)),
            pl.BlockSpec((D_MODEL, FFN_TF), lambda i, j: (0, j)),
            pl.BlockSpec((D_MODEL, FFN_TF), lambda i, j: (0, j)),
            pl.BlockSpec((FFN_TF, D_MODEL), lambda i, j: (j, 0)),
            pl.BlockSpec((1, D_MODEL), lambda i, j: (0, 0)),
        ],
        out_specs=pl.BlockSpec((FFN_TM, D_MODEL), lambda i, j: (i, 0)),
        out_shape=jax.ShapeDtypeStruct((t, D_MODEL), F32),
        scratch_shapes=[pltpu.VMEM((FFN_TM, D_MODEL), BF16)],
        compiler_params=pltpu.CompilerParams(
            dimension_semantics=("parallel", "arbitrary"),
            vmem_limit_bytes=_vmem_limit(vmem)),
        name="ffn_final" if final_norm else "ffn",
    )(x, norm_g, wg, wu, wd, final_g)


def _regroup_kernel(w_ref, o_ref):
    o_ref[...] = w_ref[0].astype(BF16)


def _regroup(w_t, layer):
    d = w_t.shape[2]
    n_head_gate = (W_IN_GATES[1] - W_IN_MOBA[0]) // PROJ_TN

    def src_row(r):
        row = jnp.where(r < n_head_gate, W_IN_MOBA[0] + r * PROJ_TN,
                        W_IN_GLA[0] + (r - n_head_gate) * PROJ_TN)
        return pl.multiple_of(row, SUBLANES_V7X)

    return pl.pallas_call(
        _regroup_kernel,
        grid=(PROJ_W // PROJ_TN,),
        in_specs=[pl.BlockSpec((pl.Element(1), pl.Element(PROJ_TN), pl.Element(d)),
                               lambda r: (layer, src_row(r), 0))],
        out_specs=pl.BlockSpec((PROJ_TN, d), lambda r: (r, 0)),
        out_shape=jax.ShapeDtypeStruct((PROJ_W, d), BF16),
        compiler_params=pltpu.CompilerParams(
            dimension_semantics=("parallel",),
            vmem_limit_bytes=_vmem_limit(2 * PROJ_TN * d * 6 + PROJ_TN * d * 4 + (4 << 20))),
        name="regroup",
    )(w_t)


def _log_sigmoid(z):
    return jnp.minimum(z, 0.0) - jnp.log1p(jnp.exp(-jnp.abs(z)))


def _in_proj_kernel(x_ref, g_ref, w_ref, cs_ref, wlr_ref, wa_ref, ba_ref,
                    proj_ref, heads_ref, la_ref, xn_even_ref, xn_odd_ref, lr_ref):
    j = pl.program_id(1)

    def step(xn_ref, norm_next):
        @pl.when(j == 0)
        def _():
            lr_ref[...] = _dot_nt(xn_ref[...], wlr_ref[...])

        norm_next()

        rows = pl.ds(pl.multiple_of(jnp.minimum(j, PROJ_TM // LA_ROWS - 1) * LA_ROWS, LA_ROWS),
                     LA_ROWS)
        lr = lr_ref[rows, :]
        lr_hi = lr.astype(BF16).astype(F32)
        lane = lax.broadcasted_iota(jnp.int32, lr.shape, 1)
        packed = jnp.where(lane < 2 * GLA_RANK, lr_hi, lr - lr_hi).astype(BF16)
        z = _dot(packed, wa_ref[...]) + ba_ref[...]
        la_ref[rows, :] = _log_sigmoid(z) * (1.0 / GLA_TAU)

        res = (_dot_nt(xn_ref[...], w_ref[...]) * cs_ref[...]).astype(BF16)
        proj_ref[...] = res
        for h in range(MOBA_HEADS):
            heads_ref[h] = res[:, h * MOBA_HD:(h + 1) * MOBA_HD]

    _with_norm_ahead(x_ref, g_ref, xn_even_ref, xn_odd_ref, PROJ_TM, step)


def _in_proj(x, norm_g, w_main, col_scale, w_lr, w_alpha, b_alpha, bsz):
    t = x.shape[0]
    rpb = t // bsz // PROJ_TM
    assert PROJ_W // PROJ_TN > PROJ_TM // AHEAD_ROWS and 3 * GLA_RANK <= LANES_V7X
    vmem = (2 * PROJ_TM * D_MODEL * 4 + 2 * PROJ_TM * D_MODEL * 2
            + 2 * D_MODEL * PROJ_TN * 2 + 2 * 2 * PROJ_TM * PROJ_TN * 2
            + 2 * PROJ_TM * GLA_QK_W * 4 + 2 * PROJ_TM * PROJ_TN * 4
            + 2 * D_MODEL * LANES_V7X * 2 + PROJ_TM * LANES_V7X * 4 + (4 << 20))
    return pl.pallas_call(
        _in_proj_kernel,
        grid=(t // PROJ_TM, PROJ_W // PROJ_TN),
        in_specs=[
            pl.BlockSpec((PROJ_TM, D_MODEL), lambda i, j: (_next_row_index(i, j, t // PROJ_TM), 0)),
            pl.BlockSpec((1, D_MODEL), lambda i, j: (0, 0)),
            pl.BlockSpec((PROJ_TN, D_MODEL), lambda i, j: (j, 0)),
            pl.BlockSpec((1, PROJ_TN), lambda i, j: (0, j)),
            pl.BlockSpec((LANES_V7X, D_MODEL), lambda i, j: (0, 0)),
            pl.BlockSpec((LANES_V7X, GLA_QK_W), lambda i, j: (0, 0)),
            pl.BlockSpec((1, GLA_QK_W), lambda i, j: (0, 0)),
        ],
        out_specs=[
            pl.BlockSpec((PROJ_TM, PROJ_TN), lambda i, j: (i, jnp.maximum(j - HEAD_TILES, 0))),
            pl.BlockSpec((None, None, MOBA_HEADS, PROJ_TM, MOBA_HD),
                         lambda i, j: (i // rpb, jnp.minimum(j, HEAD_TILES), 0, i % rpb, 0)),
            pl.BlockSpec((PROJ_TM, GLA_QK_W), lambda i, j: (i, 0)),
        ],
        out_shape=[
            jax.ShapeDtypeStruct((t, OUT_W), BF16),
            jax.ShapeDtypeStruct((bsz, HEAD_TILES + 1, MOBA_HEADS, t // bsz, MOBA_HD), BF16),
            jax.ShapeDtypeStruct((t, GLA_QK_W), F32),
        ],
        scratch_shapes=[pltpu.VMEM((PROJ_TM, D_MODEL), BF16),
                        pltpu.VMEM((PROJ_TM, D_MODEL), BF16),
                        pltpu.VMEM((PROJ_TM, LANES_V7X), F32)],
        compiler_params=pltpu.CompilerParams(
            dimension_semantics=("arbitrary", "arbitrary"),
            vmem_limit_bytes=_vmem_limit(vmem)),
        name="in_proj",
    )(x, norm_g, w_main, col_scale, w_lr, w_alpha, b_alpha)


def _bcast_rows(v, s, c):
    return jnp.broadcast_to(v, (c // s, s, GLA_DK)).reshape(c, GLA_DK)


def _gla_head(q, k, v, la, gr, gn, st_ref):
    c = q.shape[0]
    row = lax.broadcasted_iota(jnp.int32, (c, 1), 0)
    ri = lax.broadcasted_iota(jnp.int32, (c, c), 0)
    ci = lax.broadcasted_iota(jnp.int32, (c, c), 1)
    rxc = ri ^ ci

    def next_level(cs, s):
        ends = cs.reshape(c // s, s, GLA_DK)[:, s - 1:s, :]
        prev = jnp.concatenate([jnp.zeros_like(ends[:1]), ends[:-1]], axis=0)
        odd = ((row // s) & 1) == 1
        return ends, odd, cs + jnp.where(odd, _bcast_rows(prev, s, c), 0.0)

    cs = la.reshape(c // SUBLANES_V7X, SUBLANES_V7X, GLA_DK)
    sub = lax.broadcasted_iota(jnp.int32, cs.shape, 1)
    sh = 1
    while sh < SUBLANES_V7X:
        cs = cs + jnp.where(sub >= sh, pltpu.roll(cs, sh, axis=1), 0.0)
        sh *= 2
    cs = cs.reshape(c, GLA_DK)
    s = SUBLANES_V7X
    while s < GLA_BASE:
        _, _, cs = next_level(cs, s)
        s *= 2

    p = _dot_nt((q * jnp.exp(cs)).astype(BF16), (k * jnp.exp(-cs)).astype(BF16))
    attn = jnp.where(ci <= ri, p, 0.0)
    while s < c:
        ends, odd, cs_next = next_level(cs, s)
        qe = jnp.where(odd, jnp.exp(cs), 0.0)
        ke = jnp.where(odd, 0.0, jnp.exp(_bcast_rows(ends, s, c) - cs))
        p = _dot_nt((q * qe).astype(BF16), (k * ke).astype(BF16))
        attn = jnp.where(rxc < s, attn, p)
        cs = cs_next
        s *= 2

    b_last = cs[c - 1:c, :]
    st = st_ref[...]
    o = _dot_nt((q * jnp.exp(cs)).astype(BF16), st.astype(BF16))
    o = o + _dot(attn.astype(BF16), v)
    kd = (k * jnp.exp(b_last - cs)).astype(BF16)
    st_ref[...] = st * jnp.exp(b_last) + _dot_tn(v, kd)

    ms = jnp.mean(o * o, axis=-1, keepdims=True)
    y = o * lax.rsqrt(ms + NORM_EPS) * gn
    g = gr.astype(F32)
    return y * (g * jax.nn.sigmoid(g))


def _gla_kernel(q_ref, k_ref, v_ref, gr_ref, la_ref, gn_ref, o_ref, st_ref):
    @pl.when(pl.program_id(1) == 0)
    def _():
        st_ref[...] = jnp.zeros_like(st_ref)

    gn = gn_ref[...]
    for n in range(GLA_STEP_CHUNKS):
        rs = slice(n * GLA_CHUNK, (n + 1) * GLA_CHUNK)
        for h in range(GLA_HEADS):
            ks = slice(h * GLA_DK, (h + 1) * GLA_DK)
            vs = slice(h * GLA_DV, (h + 1) * GLA_DV)
            out = _gla_head(q_ref[rs, ks].astype(F32), k_ref[rs, ks].astype(F32), v_ref[rs, vs],
                            la_ref[rs, ks], gr_ref[rs, vs], gn, st_ref.at[h])
            o_ref[rs, vs] = out.astype(o_ref.dtype)


def _gla(proj3, la3, gn):
    b, s, _ = proj3.shape
    c = GLA_CHUNK * GLA_STEP_CHUNKS
    vmem = (2 * c * (2 * GLA_QK_W + 3 * GLA_V_W) * 2 + 2 * c * GLA_QK_W * 4
            + GLA_HEADS * GLA_DV * GLA_DK * 4
            + 16 * GLA_CHUNK * GLA_CHUNK * 4 + (4 << 20))
    return pl.pallas_call(
        _gla_kernel,
        grid=(b, s // c),
        in_specs=[
            pl.BlockSpec((None, c, GLA_QK_W), lambda i, j: (i, j, COL_GQ // GLA_QK_W)),
            pl.BlockSpec((None, c, GLA_QK_W), lambda i, j: (i, j, COL_GK // GLA_QK_W)),
            pl.BlockSpec((None, c, GLA_V_W), lambda i, j: (i, j, COL_GV // GLA_V_W)),
            pl.BlockSpec((None, c, GLA_V_W), lambda i, j: (i, j, COL_GR // GLA_V_W)),
            pl.BlockSpec((None, c, GLA_QK_W), lambda i, j: (i, j, 0)),
            pl.BlockSpec((1, GLA_DV), lambda i, j: (0, 0)),
        ],
        out_specs=pl.BlockSpec((None, c, GLA_V_W), lambda i, j: (i, j, 0)),
        out_shape=jax.ShapeDtypeStruct((b, s, GLA_V_W), BF16),
        scratch_shapes=[pltpu.VMEM((GLA_HEADS, GLA_DV, GLA_DK), F32)],
        compiler_params=pltpu.CompilerParams(
            dimension_semantics=("parallel", "arbitrary"),
            vmem_limit_bytes=_vmem_limit(vmem)),
        name="gla",
    )(proj3, proj3, proj3, proj3, la3, gn)


def _split3_bf16(z):
    hi = z.astype(BF16).astype(F32)
    r = z - hi
    mid = r.astype(BF16).astype(F32)
    lo = (r - mid).astype(BF16).astype(F32)
    return hi, mid, lo


def _trunc_bf16(x):
    bits = lax.bitcast_convert_type(x, jnp.uint32) & jnp.uint32(0xFFFF0000)
    return lax.bitcast_convert_type(bits, F32)


def _moba_key_table(slopes, seq):
    nb = seq // MOBA_BLOCK
    pos = jnp.arange(seq)
    z = (slopes * LOG2E)[:, None] * pos.astype(F32)[None, :]
    hi = _trunc_bf16(z)
    mid = _trunc_bf16(z - hi)
    lo = _trunc_bf16(z - hi - mid)
    lane = jnp.arange(MOBA_HD)
    onehot = (lane[None, :] == (pos // MOBA_BLOCK)[:, None]).astype(F32)
    table = (onehot[None] + hi[..., None] * (lane == nb) + mid[..., None] * (lane == nb + 1)
             + lo[..., None] * (lane == nb + 2))
    return table.astype(BF16)


def _moba_kernel(item_q_ref, item_k_ref, q_ref, k_ref, v_ref, kx_ref, o_ref,
                 vt_ref, qt_ref, ka_ref, kmean_ref, kms_ref, m_ref, acc_ref,
                 s_ref, smax_ref, p_ref, al_ref):
    blk = MOBA_BLOCK
    hd = MOBA_HD
    seq = k_ref.shape[0]
    nb = seq // blk

    def prep(j, carry):
        sl = pl.ds(pl.multiple_of(j * blk, blk), blk)
        kj = k_ref[sl, :]
        vt_ref[0:hd, sl] = v_ref[sl, :].T
        qt_ref[0:hd, sl] = q_ref[sl, :].T
        ka_ref[sl, 0:hd] = kj
        ka_ref[sl, hd:2 * hd] = kx_ref[sl, :]
        kmean_ref[pl.ds(j, 1), :] = jnp.mean(kj.astype(F32), axis=0, keepdims=True)
        return carry
    lax.fori_loop(0, nb, prep, 0)
    qt_ref[hd + nb:2 * hd, :] = jnp.ones((hd - nb, seq), BF16)
    kms_ref[...] = jnp.concatenate(_split3_bf16(kmean_ref[...]), axis=0).astype(BF16)

    chunk = MOBA_SEL_CHUNK
    for c in range(seq // chunk):
        live = min(nb, -(-((c + 1) * (chunk // blk) - 1) // SUBLANES_V7X) * SUBLANES_V7X)
        live = max(live, SUBLANES_V7X)
        cs = slice(c * chunk, (c + 1) * chunk)
        g3 = _dot(kms_ref[...], qt_ref[0:hd, cs])
        gate = g3[0:live] + g3[nb:nb + live] + g3[2 * nb:2 * nb + live]
        bidx = lax.broadcasted_iota(jnp.int32, gate.shape, 0)
        qblk = c * (chunk // blk) + lax.broadcasted_iota(jnp.int32, gate.shape, 1) // blk
        gate = jnp.where(bidx < qblk, gate, -jnp.inf)
        mask = jnp.full(gate.shape, MASK_VALUE, F32)
        for _ in range(MOBA_TOPK):
            m = jnp.max(gate, axis=0, keepdims=True)
            first = jnp.min(jnp.where(gate == m, bidx, nb), axis=0, keepdims=True)
            pick = bidx == jnp.where(m > -jnp.inf, first, -1)
            mask = jnp.where(pick, 0.0, mask)
            gate = jnp.where(pick, -jnp.inf, gate)
        if live < nb:
            mask = jnp.concatenate([mask, jnp.full((nb - live, chunk), MASK_VALUE, F32)], axis=0)
        qt_ref[hd:hd + nb, cs] = mask.astype(BF16)

    m_ref[...] = jnp.full(m_ref.shape, MASK_VALUE, F32)
    ones_row = lax.broadcasted_iota(jnp.int32, (MOBA_SUM_ROWS, seq), 0) == 0
    vt_ref[hd:hd + MOBA_SUM_ROWS, :] = jnp.where(ones_row, 1.0, 0.0).astype(BF16)

    def clear(i, carry):
        acc_ref[i] = jnp.zeros((hd + MOBA_SUM_ROWS, blk), F32)
        return carry
    lax.fori_loop(0, nb, clear, 0)

    causal = (lax.broadcasted_iota(jnp.int32, (blk, blk), 0)
              <= lax.broadcasted_iota(jnp.int32, (blk, blk), 1))
    tail_row = lax.broadcasted_iota(jnp.int32, (hd, blk), 0)
    own_tail = jnp.where(tail_row < nb, 0.0, 1.0).astype(BF16)

    def make_stages(own_block):
        rows = blk if own_block else MOBA_GROUP * blk
        units = (MOBA_OWN_STEP_BLOCKS if own_block else MOBA_STEP_BLOCKS) * blk // rows

        def item(n):
            if own_block:
                i = j = n
            else:
                i = item_q_ref[n]
                j = item_k_ref[n]
            return (i, pl.ds(pl.multiple_of(i * blk, blk), blk),
                    pl.ds(pl.multiple_of(j * blk, blk), rows))

        def scores(n0):
            for u in range(units):
                i, qs, ks = item(n0 + u)
                if own_block:
                    rhs = jnp.concatenate([qt_ref[0:hd, qs], own_tail], axis=0)
                    s = jnp.where(causal, _dot(ka_ref[ks, :], rhs), MASK_VALUE)
                else:
                    s = _dot(ka_ref[ks, :], qt_ref[:, qs])
                s_ref[u * rows:(u + 1) * rows, :] = s
                smax_ref[u] = jnp.max(s, axis=0, keepdims=True)

        def softmax(n0):
            for u in range(units):
                i, _, _ = item(n0 + u)
                row = pl.ds(i, 1)
                m_old = m_ref[row, :]
                m_new = jnp.maximum(m_old, smax_ref[u])
                alpha = jnp.exp2(m_old - m_new)
                p = jnp.exp2(s_ref[u * rows:(u + 1) * rows, :] - m_new)
                m_ref[row, :] = m_new
                p_ref[u * rows:(u + 1) * rows, :] = p.astype(BF16)
                al_ref[u] = alpha

        def values(n0):
            for u in range(units):
                i, _, ks = item(n0 + u)
                acc_ref[i] = (al_ref[u] * acc_ref[i]
                              + _dot(vt_ref[:, ks], p_ref[u * rows:(u + 1) * rows, :]))

        return units, scores, softmax, values

    def run_pipeline(n_items, stages):
        units, scores, softmax, values = stages
        steps = n_items // units
        scores(0)
        softmax(0)
        scores(units)

        def body(t, carry):
            values(t * units)
            softmax((t + 1) * units)
            scores((t + 2) * units)
            return carry
        lax.fori_loop(0, steps - 2, body, 0)
        values((steps - 2) * units)
        softmax((steps - 1) * units)
        values((steps - 1) * units)

    run_pipeline(nb, make_stages(True))
    run_pipeline(item_q_ref.shape[0], make_stages(False))

    def finish(i0, carry):
        for u in range(MOBA_FINISH_UNROLL):
            i = i0 * MOBA_FINISH_UNROLL + u
            qs = pl.ds(pl.multiple_of(i * blk, blk), blk)
            acc = acc_ref[i]
            inv = 1.0 / acc[hd:hd + 1, :]
            o_ref[qs, :] = (acc[0:hd] * inv).T.astype(o_ref.dtype)
        return carry
    lax.fori_loop(0, nb // MOBA_FINISH_UNROLL, finish, 0)


def _moba(qkv, slopes):
    b, _, _, s, _ = qkv.shape
    blk = MOBA_BLOCK
    nb = s // blk
    hd = MOBA_HD
    assert nb + 3 <= hd and s % MOBA_SEL_CHUNK == 0
    groups = [(i, j) for i in range(nb) for j in range(0, i, MOBA_GROUP)]
    own_units = MOBA_OWN_STEP_BLOCKS
    grp_units = MOBA_STEP_BLOCKS // MOBA_GROUP
    assert MOBA_STEP_BLOCKS % MOBA_GROUP == 0 and nb % MOBA_GROUP == 0
    assert MOBA_OWN_STEP_BLOCKS <= MOBA_STEP_BLOCKS and grp_units <= own_units
    assert nb % own_units == 0 and nb // own_units >= 2
    assert len(groups) % grp_units == 0 and len(groups) // grp_units >= 2
    assert nb % MOBA_FINISH_UNROLL == 0
    item_q = jnp.asarray(np.array([g[0] for g in groups], np.int32))
    item_k = jnp.asarray(np.array([g[1] for g in groups], np.int32))
    step_rows = MOBA_STEP_BLOCKS * blk
    vmem = (2 * 5 * s * hd * 2
            + s * (hd + MOBA_SUM_ROWS) * 2 + 2 * 2 * s * hd * 2
            + nb * (hd + MOBA_SUM_ROWS) * blk * 4
            + step_rows * blk * 6
            + 2 * step_rows * blk * 4 + (4 << 20))
    smem = pl.BlockSpec(memory_space=pltpu.SMEM)
    return pl.pallas_call(
        _moba_kernel,
        grid=(b, MOBA_HEADS),
        in_specs=[
            smem, smem,
            pl.BlockSpec((None, None, None, s, hd), lambda bi, h: (bi, 0, h, 0, 0)),
            pl.BlockSpec((None, None, None, s, hd), lambda bi, h: (bi, 1, h, 0, 0)),
            pl.BlockSpec((None, None, None, s, hd), lambda bi, h: (bi, 2, h, 0, 0)),
            pl.BlockSpec((None, s, hd), lambda bi, h: (h, 0, 0)),
        ],
        out_specs=pl.BlockSpec((None, None, s, hd), lambda bi, h: (bi, h, 0, 0)),
        out_shape=jax.ShapeDtypeStruct((b, MOBA_HEADS, s, hd), BF16),
        scratch_shapes=[
            pltpu.VMEM((hd + MOBA_SUM_ROWS, s), BF16),
            pltpu.VMEM((2 * hd, s), BF16),
            pltpu.VMEM((s, 2 * hd), BF16),
            pltpu.VMEM((nb, hd), F32),
            pltpu.VMEM((3 * nb, hd), BF16),
            pltpu.VMEM((nb, blk), F32),
            pltpu.VMEM((nb, hd + MOBA_SUM_ROWS, blk), F32),
            pltpu.VMEM((step_rows, blk), F32),
            pltpu.VMEM((own_units, 1, blk), F32),
            pltpu.VMEM((step_rows, blk), BF16),
            pltpu.VMEM((own_units, 1, blk), F32),
        ],
        compiler_params=pltpu.CompilerParams(
            dimension_semantics=("parallel", "parallel"),
            vmem_limit_bytes=_vmem_limit(vmem)),
        name="moba",
    )(item_q, item_k, qkv, qkv, qkv, _moba_key_table(slopes, s))


def _merge_kernel(x_ref, oa_ref, ob_ref, ga_ref, gb_ref, wa_ref, wb_ref, wo_ref, o_ref):
    ya = _dot(oa_ref[...], wa_ref[...])
    ob = jnp.concatenate([ob_ref[h] for h in range(MOBA_HEADS)], axis=1)
    yb = _dot(ob, wb_ref[...])
    mix = (jax.nn.sigmoid(ga_ref[...].astype(F32)) * ya
           + jax.nn.sigmoid(gb_ref[...].astype(F32)) * yb)
    o_ref[...] = x_ref[...] + _dot(mix.astype(BF16), wo_ref[...])


def _merge(x, oa, ob, proj, wa, wb, wo):
    t = x.shape[0]
    tm = MERGE_TM
    rpb = ob.shape[2] // tm
    const = dict(pipeline_mode=pl.Buffered(1))
    vmem = (2 * 2 * tm * D_MODEL * 4 + 2 * 2 * tm * GLA_V_W * 2 + 2 * 2 * tm * D_MODEL * 2
            + (2 * GLA_V_W + D_MODEL) * D_MODEL * 2 + 4 * tm * D_MODEL * 4 + (4 << 20))
    return pl.pallas_call(
        _merge_kernel,
        grid=(t // tm,),
        in_specs=[
            pl.BlockSpec((tm, D_MODEL), lambda i: (i, 0)),
            pl.BlockSpec((tm, GLA_V_W), lambda i: (i, 0)),
            pl.BlockSpec((None, MOBA_HEADS, tm, MOBA_HD), lambda i: (i // rpb, 0, i % rpb, 0)),
            pl.BlockSpec((tm, D_MODEL), lambda i: (i, COL_GA // D_MODEL)),
            pl.BlockSpec((tm, D_MODEL), lambda i: (i, COL_GB // D_MODEL)),
            pl.BlockSpec((GLA_V_W, D_MODEL), lambda i: (0, 0), **const),
            pl.BlockSpec((MOBA_W, D_MODEL), lambda i: (0, 0), **const),
            pl.BlockSpec((D_MODEL, D_MODEL), lambda i: (0, 0), **const),
        ],
        out_specs=pl.BlockSpec((tm, D_MODEL), lambda i: (i, 0)),
        out_shape=jax.ShapeDtypeStruct((t, D_MODEL), F32),
        compiler_params=pltpu.CompilerParams(
            dimension_semantics=("parallel",),
            vmem_limit_bytes=_vmem_limit(vmem)),
        name="merge",
    )(x, oa, ob, proj, proj, wa, wb, wo)


def kernel(x, ffn1_norm, ffn1_w_gate, ffn1_w_up, ffn1_w_down, mix_norm, w_in, gla_w_alpha,
           gla_b_alpha, gla_out_norm, w_branch_gla, w_branch_moba, w_out, ffn2_norm,
           ffn2_w_gate, ffn2_w_up, ffn2_w_down, final_norm):
    bsz, seq, d = x.shape
    t = bsz * seq
    depth = ffn1_norm.shape[0]
    assert depth >= 1 and d == D_MODEL and seq % MOBA_BLOCK == 0
    assert seq % (GLA_CHUNK * GLA_STEP_CHUNKS) == 0
    assert t % FFN_TM == 0 and seq % PROJ_TM == 0 and seq % MERGE_TM == 0
    assert OUT_W % PROJ_TN == 0
    assert PROJ_TM % LA_ROWS == 0 and PROJ_TM // LA_ROWS <= PROJ_W // PROJ_TN

    slopes = jnp.exp2(-8.0 * jnp.arange(1, MOBA_HEADS + 1, dtype=F32) / MOBA_HEADS)
    col_scale = jnp.ones((1, PROJ_W), F32)
    col_scale = col_scale.at[:, HEAD_COLS + COL_GQ:HEAD_COLS + COL_GQ + GLA_QK_W].set(GLA_DK ** -0.5)
    col_scale = col_scale.at[:, :MOBA_W].set(MOBA_HD ** -0.5 * LOG2E)
    final_g = final_norm.reshape(1, D_MODEL)

    xf = x.reshape(t, D_MODEL)
    for l in range(depth):
        last = l == depth - 1
        w_in_t = jnp.swapaxes(w_in, 1, 2)
        w_main = _regroup(w_in_t, l)
        w_lr = jnp.pad(jnp.tile(w_in_t[l, slice(*W_IN_LR), :], (3, 1)),
                       ((0, LANES_V7X - 3 * GLA_RANK), (0, 0))).astype(BF16)
        wa_hi = _trunc_bf16(gla_w_alpha[l])
        wa_mid = _trunc_bf16(gla_w_alpha[l] - wa_hi)
        w_alpha = jnp.pad(jnp.concatenate([wa_hi, wa_mid, wa_hi], axis=0),
                          ((0, LANES_V7X - 3 * GLA_RANK), (0, 0))).astype(BF16)

        xf = _ffn(xf, ffn1_norm[l].reshape(1, D_MODEL), ffn1_w_gate[l].astype(BF16),
                  ffn1_w_up[l].astype(BF16), ffn1_w_down[l].astype(BF16), final_g, False)
        proj, qkv, log_a = _in_proj(xf, mix_norm[l].reshape(1, D_MODEL), w_main, col_scale,
                                    w_lr, w_alpha, gla_b_alpha[l].reshape(1, GLA_QK_W), bsz)
        proj3 = proj.reshape(bsz, seq, OUT_W)
        oa = _gla(proj3, log_a.reshape(bsz, seq, GLA_QK_W), gla_out_norm[l].reshape(1, GLA_DV))
        ob = _moba(qkv, slopes)
        xf = _merge(xf, oa.reshape(t, GLA_V_W), ob, proj,
                    w_branch_gla[l].astype(BF16), w_branch_moba[l].astype(BF16),
                    w_out[l].astype(BF16))
        xf = _ffn(xf, ffn2_norm[l].reshape(1, D_MODEL), ffn2_w_gate[l].astype(BF16),
                  ffn2_w_up[l].astype(BF16), ffn2_w_down[l].astype(BF16), final_g, last)
    return xf.reshape(bsz, seq, D_MODEL)
```

```python
import functools

import jax
import jax.numpy as jnp
import numpy as np
from jax import lax
from jax.experimental import pallas as pl
from jax.experimental.pallas import tpu as pltpu

F32 = jnp.float32
BF16 = jnp.bfloat16

D_MODEL = 2048
D_FF = 5632
GLA_HEADS = 4
GLA_DK = 128
GLA_DV = 256
GLA_RANK = 16
GLA_TAU = 16.0
MOBA_HEADS = 8
MOBA_HD = 128
MOBA_BLOCK = 256
MOBA_TOPK = 3
NORM_EPS = 1e-6

GLA_QK_W = GLA_HEADS * GLA_DK
GLA_V_W = GLA_HEADS * GLA_DV
MOBA_W = MOBA_HEADS * MOBA_HD

HEAD_COLS = 3 * MOBA_W
COL_GA = 0
COL_GB = COL_GA + D_MODEL
COL_GQ = COL_GB + D_MODEL
COL_GK = COL_GQ + GLA_QK_W
COL_GV = COL_GK + GLA_QK_W
COL_GR = COL_GV + GLA_V_W
OUT_W = COL_GR + GLA_V_W
PROJ_W = HEAD_COLS + OUT_W
W_IN_GLA = (0, 2 * GLA_QK_W + 2 * GLA_V_W)
W_IN_LR = (W_IN_GLA[1], W_IN_GLA[1] + GLA_RANK)
W_IN_MOBA = (W_IN_LR[1], W_IN_LR[1] + 3 * MOBA_W)
W_IN_GATES = (W_IN_MOBA[1], W_IN_MOBA[1] + 2 * D_MODEL)

LANES_V7X = 128
SUBLANES_V7X = 8
VMEM_BYTES_V7X = 64 << 20
VMEM_CAP = VMEM_BYTES_V7X - (6 << 20)

FFN_TM = 1024
FFN_TF = 512
PROJ_TM = 1024
PROJ_TN = MOBA_W
HEAD_TILES = HEAD_COLS // PROJ_TN
LA_ROWS = 128
GLA_CHUNK = 256
GLA_STEP_CHUNKS = 2
GLA_BASE = 2 * SUBLANES_V7X
MERGE_TM = 512
NORM_ROWS = 256
AHEAD_ROWS = 128

MASK_VALUE = -1e30
MOBA_STEP_BLOCKS = 32
MOBA_OWN_STEP_BLOCKS = 16
MOBA_SUM_ROWS = 2 * SUBLANES_V7X
MOBA_GROUP = 4
MOBA_SEL_CHUNK = 1024
MOBA_FINISH_UNROLL = 4
LOG2E = 1.4426950408889634


def _vmem_limit(nbytes):
    return int(min(VMEM_CAP, nbytes))


def _dot(a, b):
    return jnp.dot(a, b, preferred_element_type=F32)


def _dot_nt(a, b):
    return lax.dot_general(a, b, (((1,), (1,)), ((), ())), preferred_element_type=F32)


def _dot_tn(a, b):
    return lax.dot_general(a, b, (((0,), (0,)), ((), ())), preferred_element_type=F32)


def _rms_norm_slice(x_ref, g_ref, out_ref, sl):
    x = x_ref[sl, :].astype(F32)
    ms = jnp.mean(x * x, axis=-1, keepdims=True)
    out_ref[sl, :] = (x * lax.rsqrt(ms + NORM_EPS) * g_ref[...]).astype(out_ref.dtype)


def _rms_norm_rows(x_ref, g_ref, out_ref, rows):
    def body(r, carry):
        _rms_norm_slice(x_ref, g_ref, out_ref,
                        pl.ds(pl.multiple_of(r * NORM_ROWS, NORM_ROWS), NORM_ROWS))
        return carry
    lax.fori_loop(0, rows // NORM_ROWS, body, 0)


def _next_row_index(i, j, n_rows):
    return jnp.minimum(i + jnp.minimum(j, 1), n_rows - 1)


def _with_norm_ahead(x_ref, g_ref, xn_even_ref, xn_odd_ref, rows, step):
    i = pl.program_id(0)
    j = pl.program_id(1)

    @pl.when(jnp.logical_and(i == 0, j == 0))
    def _():
        _rms_norm_rows(x_ref, g_ref, xn_even_ref, rows)

    r = jnp.clip(j - 1, 0, rows // AHEAD_ROWS - 1)
    sl = pl.ds(pl.multiple_of(r * AHEAD_ROWS, AHEAD_ROWS), AHEAD_ROWS)
    for parity, cur, nxt in ((0, xn_even_ref, xn_odd_ref), (1, xn_odd_ref, xn_even_ref)):
        @pl.when(i % 2 == parity)
        def _(cur=cur, nxt=nxt):
            step(cur, functools.partial(_rms_norm_slice, x_ref, g_ref, nxt, sl))


def _ffn_kernel(x_ref, g_ref, wg_ref, wu_ref, wd_ref, fg_ref, o_ref, xn_ref, *, final_norm):
    j = pl.program_id(1)

    @pl.when(j == 0)
    def _():
        _rms_norm_rows(x_ref, g_ref, xn_ref, FFN_TM)
        o_ref[...] = x_ref[...]

    xn = xn_ref[...]
    hg = _dot(xn, wg_ref[...])
    hu = _dot(xn, wu_ref[...])
    a = (0.5 * hg * jax.nn.sigmoid(hg)) * hu
    o_ref[...] += _dot(a.astype(BF16), wd_ref[...])

    if final_norm:
        @pl.when(j == pl.num_programs(1) - 1)
        def _():
            _rms_norm_rows(o_ref, fg_ref, o_ref, FFN_TM)


def _ffn(x, norm_g, wg, wu, wd, final_g, final_norm):
    t = x.shape[0]
    vmem = (2 * 2 * FFN_TM * D_MODEL * 4
            + FFN_TM * D_MODEL * 2
            + 2 * 3 * D_MODEL * FFN_TF * 2
            + 4 * FFN_TM * FFN_TF * 4
            + (4 << 20))
    return pl.pallas_call(
        functools.partial(_ffn_kernel, final_norm=final_norm),
        grid=(t // FFN_TM, D_FF // FFN_TF),
        in_specs=[
            pl.BlockSpec((FFN_TM, D_MODEL), lambda i, j: (i, 0)),
            pl.BlockSpec((1, D_MODEL), lambda i, j: (0, 0)),
            pl.BlockSpec((D_MODEL, FFN_TF), lambda i, j: (0, j)),
            pl.BlockSpec((D_MODEL, FFN_TF), lambda i, j: (0, j)),
            pl.BlockSpec((FFN_TF, D_MODEL), lambda i, j: (j, 0)),
            pl.BlockSpec((1, D_MODEL), lambda i, j: (0, 0)),
        ],
        out_specs=pl.BlockSpec((FFN_TM, D_MODEL), lambda i, j: (i, 0)),
        out_shape=jax.ShapeDtypeStruct((t, D_MODEL), F32),
        scratch_shapes=[pltpu.VMEM((FFN_TM, D_MODEL), BF16)],
        compiler_params=pltpu.CompilerParams(
            dimension_semantics=("parallel", "arbitrary"),
            vmem_limit_bytes=_vmem_limit(vmem)),
        name="ffn_final" if final_norm else "ffn",
    )(x, norm_g, wg, wu, wd, final_g)


def _regroup_kernel(w_ref, o_ref):
    o_ref[...] = w_ref[0].astype(BF16)


def _regroup(w_t, layer):
    d = w_t.shape[2]
    n_head_gate = (W_IN_GATES[1] - W_IN_MOBA[0]) // PROJ_TN

    def src_row(r):
        row = jnp.where(r < n_head_gate, W_IN_MOBA[0] + r * PROJ_TN,
                        W_IN_GLA[0] + (r - n_head_gate) * PROJ_TN)
        return pl.multiple_of(row, SUBLANES_V7X)

    return pl.pallas_call(
        _regroup_kernel,
        grid=(PROJ_W // PROJ_TN,),
        in_specs=[pl.BlockSpec((pl.Element(1), pl.Element(PROJ_TN), pl.Element(d)),
                               lambda r: (layer, src_row(r), 0))],
        out_specs=pl.BlockSpec((PROJ_TN, d), lambda r: (r, 0)),
        out_shape=jax.ShapeDtypeStruct((PROJ_W, d), BF16),
        compiler_params=pltpu.CompilerParams(
            dimension_semantics=("parallel",),
            vmem_limit_bytes=_vmem_limit(2 * PROJ_TN * d * 6 + PROJ_TN * d * 4 + (4 << 20))),
        name="regroup",
    )(w_t)


def _log_sigmoid(z):
    return jnp.minimum(z, 0.0) - jnp.log1p(jnp.exp(-jnp.abs(z)))


def _in_proj_kernel(x_ref, g_ref, w_ref, cs_ref, wlr_ref, wa_ref, ba_ref,
                    proj_ref, heads_ref, la_ref, xn_even_ref, xn_odd_ref, lr_ref):
    j = pl.program_id(1)

    def step(xn_ref, norm_next):
        @pl.when(j == 0)
        def _():
            lr_ref[...] = _dot_nt(xn_ref[...], wlr_ref[...])

        norm_next()

        rows = pl.ds(pl.multiple_of(jnp.minimum(j, PROJ_TM // LA_ROWS - 1) * LA_ROWS, LA_ROWS),
                     LA_ROWS)
        lr = lr_ref[rows, :]
        lr_hi = lr.astype(BF16).astype(F32)
        lane = lax.broadcasted_iota(jnp.int32, lr.shape, 1)
        packed = jnp.where(lane < 2 * GLA_RANK, lr_hi, lr - lr_hi).astype(BF16)
        z = _dot(packed, wa_ref[...]) + ba_ref[...]
        la_ref[rows, :] = _log_sigmoid(z) * (1.0 / GLA_TAU)

        res = (_dot_nt(xn_ref[...], w_ref[...]) * cs_ref[...]).astype(BF16)
        proj_ref[...] = res
        for h in range(MOBA_HEADS):
            heads_ref[h] = res[:, h * MOBA_HD:(h + 1) * MOBA_HD]

    _with_norm_ahead(x_ref, g_ref, xn_even_ref, xn_odd_ref, PROJ_TM, step)


def _in_proj(x, norm_g, w_main, col_scale, w_lr, w_alpha, b_alpha, bsz):
    t = x.shape[0]
    rpb = t // bsz // PROJ_TM
    assert PROJ_W // PROJ_TN > PROJ_TM // AHEAD_ROWS and 3 * GLA_RANK <= LANES_V7X
    vmem = (2 * PROJ_TM * D_MODEL * 4 + 2 * PROJ_TM * D_MODEL * 2
            + 2 * D_MODEL * PROJ_TN * 2 + 2 * 2 * PROJ_TM * PROJ_TN * 2
            + 2 * PROJ_TM * GLA_QK_W * 4 + 2 * PROJ_TM * PROJ_TN * 4
            + 2 * D_MODEL * LANES_V7X * 2 + PROJ_TM * LANES_V7X * 4 + (4 << 20))
    return pl.pallas_call(
        _in_proj_kernel,
        grid=(t // PROJ_TM, PROJ_W // PROJ_TN),
        in_specs=[
            pl.BlockSpec((PROJ_TM, D_MODEL), lambda i, j: (_next_row_index(i, j, t // PROJ_TM), 0)),
            pl.BlockSpec((1, D_MODEL), lambda i, j: (0, 0)),
            pl.BlockSpec((PROJ_TN, D_MODEL), lambda i, j: (j, 0)),
            pl.BlockSpec((1, PROJ_TN), lambda i, j: (0, j)),
            pl.BlockSpec((LANES_V7X, D_MODEL), lambda i, j: (0, 0)),
            pl.BlockSpec((LANES_V7X, GLA_QK_W), lambda i, j: (0, 0)),
            pl.BlockSpec((1, GLA_QK_W), lambda i, j: (0, 0)),
        ],
        out_specs=[
            pl.BlockSpec((PROJ_TM, PROJ_TN), lambda i, j: (i, jnp.maximum(j - HEAD_TILES, 0))),
            pl.BlockSpec((None, None, MOBA_HEADS, PROJ_TM, MOBA_HD),
                         lambda i, j: (i // rpb, jnp.minimum(j, HEAD_TILES), 0, i % rpb, 0)),
            pl.BlockSpec((PROJ_TM, GLA_QK_W), lambda i, j: (i, 0)),
        ],
        out_shape=[
            jax.ShapeDtypeStruct((t, OUT_W), BF16),
            jax.ShapeDtypeStruct((bsz, HEAD_TILES + 1, MOBA_HEADS, t // bsz, MOBA_HD), BF16),
            jax.ShapeDtypeStruct((t, GLA_QK_W), F32),
        ],
        scratch_shapes=[pltpu.VMEM((PROJ_TM, D_MODEL), BF16),
                        pltpu.VMEM((PROJ_TM, D_MODEL), BF16),
                        pltpu.VMEM((PROJ_TM, LANES_V7X), F32)],
        compiler_params=pltpu.CompilerParams(
            dimension_semantics=("arbitrary", "arbitrary"),
            vmem_limit_bytes=_vmem_limit(vmem)),
        name="in_proj",
    )(x, norm_g, w_main, col_scale, w_lr, w_alpha, b_alpha)


def _bcast_rows(v, s, c):
    return jnp.broadcast_to(v, (c // s, s, GLA_DK)).reshape(c, GLA_DK)


def _gla_head(q, k, v, la, gr, gn, st_ref):
    c = q.shape[0]
    row = lax.broadcasted_iota(jnp.int32, (c, 1), 0)
    ri = lax.broadcasted_iota(jnp.int32, (c, c), 0)
    ci = lax.broadcasted_iota(jnp.int32, (c, c), 1)
    rxc = ri ^ ci

    def next_level(cs, s):
        ends = cs.reshape(c // s, s, GLA_DK)[:, s - 1:s, :]
        prev = jnp.concatenate([jnp.zeros_like(ends[:1]), ends[:-1]], axis=0)
        odd = ((row // s) & 1) == 1
        return ends, odd, cs + jnp.where(odd, _bcast_rows(prev, s, c), 0.0)

    cs = la.reshape(c // SUBLANES_V7X, SUBLANES_V7X, GLA_DK)
    sub = lax.broadcasted_iota(jnp.int32, cs.shape, 1)
    sh = 1
    while sh < SUBLANES_V7X:
        cs = cs + jnp.where(sub >= sh, pltpu.roll(cs, sh, axis=1), 0.0)
        sh *= 2
    cs = cs.reshape(c, GLA_DK)
    s = SUBLANES_V7X
    while s < GLA_BASE:
        _, _, cs = next_level(cs, s)
        s *= 2

    p = _dot_nt((q * jnp.exp(cs)).astype(BF16), (k * jnp.exp(-cs)).astype(BF16))
    attn = jnp.where(ci <= ri, p, 0.0)
    while s < c:
        ends, odd, cs_next = next_level(cs, s)
        e = jnp.exp(jnp.where(odd, cs, _bcast_rows(ends, s, c) - cs))
        qe = jnp.where(odd, e, 0.0)
        ke = jnp.where(odd, 0.0, e)
        p = _dot_nt((q * qe).astype(BF16), (k * ke).astype(BF16))
        attn = jnp.where(rxc < s, attn, p)
        cs = cs_next
        s *= 2

    b_last = cs[c - 1:c, :]
    st = st_ref[...]
    o = _dot_nt((q * jnp.exp(cs)).astype(BF16), st.astype(BF16))
    o = o + _dot(attn.astype(BF16), v)
    kd = (k * jnp.exp(b_last - cs)).astype(BF16)
    st_ref[...] = st * jnp.exp(b_last) + _dot_tn(v, kd)

    ms = jnp.mean(o * o, axis=-1, keepdims=True)
    y = o * lax.rsqrt(ms + NORM_EPS) * gn
    g = gr.astype(F32)
    return y * (g * jax.nn.sigmoid(g))


def _gla_kernel(q_ref, k_ref, v_ref, gr_ref, la_ref, gn_ref, o_ref, st_ref):
    @pl.when(pl.program_id(1) == 0)
    def _():
        st_ref[...] = jnp.zeros_like(st_ref)

    gn = gn_ref[...]
    for n in range(GLA_STEP_CHUNKS):
        rs = slice(n * GLA_CHUNK, (n + 1) * GLA_CHUNK)
        for h in range(GLA_HEADS):
            ks = slice(h * GLA_DK, (h + 1) * GLA_DK)
            vs = slice(h * GLA_DV, (h + 1) * GLA_DV)
            out = _gla_head(q_ref[rs, ks].astype(F32), k_ref[rs, ks].astype(F32), v_ref[rs, vs],
                            la_ref[rs, ks], gr_ref[rs, vs], gn, st_ref.at[h])
            o_ref[rs, vs] = out.astype(o_ref.dtype)


def _gla(proj3, la3, gn):
    b, s, _ = proj3.shape
    c = GLA_CHUNK * GLA_STEP_CHUNKS
    vmem = (2 * c * (2 * GLA_QK_W + 3 * GLA_V_W) * 2 + 2 * c * GLA_QK_W * 4
            + GLA_HEADS * GLA_DV * GLA_DK * 4
            + 16 * GLA_CHUNK * GLA_CHUNK * 4 + (4 << 20))
    return pl.pallas_call(
        _gla_kernel,
        grid=(b, s // c),
        in_specs=[
            pl.BlockSpec((None, c, GLA_QK_W), lambda i, j: (i, j, COL_GQ // GLA_QK_W)),
            pl.BlockSpec((None, c, GLA_QK_W), lambda i, j: (i, j, COL_GK // GLA_QK_W)),
            pl.BlockSpec((None, c, GLA_V_W), lambda i, j: (i, j, COL_GV // GLA_V_W)),
            pl.BlockSpec((None, c, GLA_V_W), lambda i, j: (i, j, COL_GR // GLA_V_W)),
            pl.BlockSpec((None, c, GLA_QK_W), lambda i, j: (i, j, 0)),
            pl.BlockSpec((1, GLA_DV), lambda i, j: (0, 0)),
        ],
        out_specs=pl.BlockSpec((None, c, GLA_V_W), lambda i, j: (i, j, 0)),
        out_shape=jax.ShapeDtypeStruct((b, s, GLA_V_W), BF16),
        scratch_shapes=[pltpu.VMEM((GLA_HEADS, GLA_DV, GLA_DK), F32)],
        compiler_params=pltpu.CompilerParams(
            dimension_semantics=("parallel", "arbitrary"),
            vmem_limit_bytes=_vmem_limit(vmem)),
        name="gla",
    )(proj3, proj3, proj3, proj3, la3, gn)


def _split3_bf16(z):
    hi = z.astype(BF16).astype(F32)
    r = z - hi
    mid = r.astype(BF16).astype(F32)
    lo = (r - mid).astype(BF16).astype(F32)
    return hi, mid, lo


def _trunc_bf16(x):
    bits = lax.bitcast_convert_type(x, jnp.uint32) & jnp.uint32(0xFFFF0000)
    return lax.bitcast_convert_type(bits, F32)


def _moba_key_table(slopes, seq):
    nb = seq // MOBA_BLOCK
    pos = jnp.arange(seq)
    z = (slopes * LOG2E)[:, None] * pos.astype(F32)[None, :]
    hi = _trunc_bf16(z)
    mid = _trunc_bf16(z - hi)
    lo = _trunc_bf16(z - hi - mid)
    lane = jnp.arange(MOBA_HD)
    onehot = (lane[None, :] == (pos // MOBA_BLOCK)[:, None]).astype(F32)
    table = (onehot[None] + hi[..., None] * (lane == nb) + mid[..., None] * (lane == nb + 1)
             + lo[..., None] * (lane == nb + 2))
    return table.astype(BF16)


def _moba_kernel(item_q_ref, item_k_ref, q_ref, k_ref, v_ref, kx_ref, o_ref,
                 vt_ref, qt_ref, ka_ref, kmean_ref, kms_ref, m_ref, acc_ref,
                 s_ref, smax_ref, p_ref, al_ref):
    blk = MOBA_BLOCK
    hd = MOBA_HD
    seq = k_ref.shape[0]
    nb = seq // blk

    def prep(j, carry):
        sl = pl.ds(pl.multiple_of(j * blk, blk), blk)
        kj = k_ref[sl, :]
        vt_ref[0:hd, sl] = v_ref[sl, :].T
        qt_ref[0:hd, sl] = q_ref[sl, :].T
        ka_ref[sl, 0:hd] = kj
        ka_ref[sl, hd:2 * hd] = kx_ref[sl, :]
        kmean_ref[pl.ds(j, 1), :] = jnp.mean(kj.astype(F32), axis=0, keepdims=True)
        return carry
    lax.fori_loop(0, nb, prep, 0)
    qt_ref[hd + nb:2 * hd, :] = jnp.ones((hd - nb, seq), BF16)
    kms_ref[...] = jnp.concatenate(_split3_bf16(kmean_ref[...]), axis=0).astype(BF16)

    chunk = MOBA_SEL_CHUNK
    for c in range(seq // chunk):
        live = min(nb, -(-((c + 1) * (chunk // blk) - 1) // SUBLANES_V7X) * SUBLANES_V7X)
        live = max(live, SUBLANES_V7X)
        cs = slice(c * chunk, (c + 1) * chunk)
        g3 = _dot(kms_ref[...], qt_ref[0:hd, cs])
        gate = g3[0:live] + g3[nb:nb + live] + g3[2 * nb:2 * nb + live]
        bidx = lax.broadcasted_iota(jnp.int32, gate.shape, 0)
        qblk = c * (chunk // blk) + lax.broadcasted_iota(jnp.int32, gate.shape, 1) // blk
        gate = jnp.where(bidx < qblk, gate, -jnp.inf)
        mask = jnp.full(gate.shape, MASK_VALUE, F32)
        for _ in range(MOBA_TOPK):
            m = jnp.max(gate, axis=0, keepdims=True)
            first = jnp.min(jnp.where(gate == m, bidx, nb), axis=0, keepdims=True)
            pick = bidx == jnp.where(m > -jnp.inf, first, -1)
            mask = jnp.where(pick, 0.0, mask)
            gate = jnp.where(pick, -jnp.inf, gate)
        if live < nb:
            mask = jnp.concatenate([mask, jnp.full((nb - live, chunk), MASK_VALUE, F32)], axis=0)
        qt_ref[hd:hd + nb, cs] = mask.astype(BF16)

    m_ref[...] = jnp.full(m_ref.shape, MASK_VALUE, F32)
    ones_row = lax.broadcasted_iota(jnp.int32, (MOBA_SUM_ROWS, seq), 0) == 0
    vt_ref[hd:hd + MOBA_SUM_ROWS, :] = jnp.where(ones_row, 1.0, 0.0).astype(BF16)

    def clear(i, carry):
        acc_ref[i] = jnp.zeros((hd + MOBA_SUM_ROWS, blk), F32)
        return carry
    lax.fori_loop(0, nb, clear, 0)

    causal = (lax.broadcasted_iota(jnp.int32, (blk, blk), 0)
              <= lax.broadcasted_iota(jnp.int32, (blk, blk), 1))
    tail_row = lax.broadcasted_iota(jnp.int32, (hd, blk), 0)
    own_tail = jnp.where(tail_row < nb, 0.0, 1.0).astype(BF16)

    def make_stages(own_block):
        rows = blk if own_block else MOBA_GROUP * blk
        units = (MOBA_OWN_STEP_BLOCKS if own_block else MOBA_STEP_BLOCKS) * blk // rows

        def item(n):
            if own_block:
                i = j = n
            else:
                i = item_q_ref[n]
                j = item_k_ref[n]
            return (i, pl.ds(pl.multiple_of(i * blk, blk), blk),
                    pl.ds(pl.multiple_of(j * blk, blk), rows))

        def scores(n0):
            for u in range(units):
                i, qs, ks = item(n0 + u)
                if own_block:
                    rhs = jnp.concatenate([qt_ref[0:hd, qs], own_tail], axis=0)
                    s = jnp.where(causal, _dot(ka_ref[ks, :], rhs), MASK_VALUE)
                else:
                    s = _dot(ka_ref[ks, :], qt_ref[:, qs])
                s_ref[u * rows:(u + 1) * rows, :] = s
                smax_ref[u] = jnp.max(s, axis=0, keepdims=True)

        def softmax(n0):
            for u in range(units):
                i, _, _ = item(n0 + u)
                row = pl.ds(i, 1)
                m_old = m_ref[row, :]
                m_new = jnp.maximum(m_old, smax_ref[u])
                alpha = jnp.exp2(m_old - m_new)
                p = jnp.exp2(s_ref[u * rows:(u + 1) * rows, :] - m_new)
                m_ref[row, :] = m_new
                p_ref[u * rows:(u + 1) * rows, :] = p.astype(BF16)
                al_ref[u] = alpha

        def values(n0):
            for u in range(units):
                i, _, ks = item(n0 + u)
                acc_ref[i] = (al_ref[u] * acc_ref[i]
                              + _dot(vt_ref[:, ks], p_ref[u * rows:(u + 1) * rows, :]))

        return units, scores, softmax, values

    def run_pipeline(n_items, stages):
        units, scores, softmax, values = stages
        steps = n_items // units
        scores(0)
        softmax(0)
        scores(units)

        def body(t, carry):
            values(t * units)
            softmax((t + 1) * units)
            scores((t + 2) * units)
            return carry
        lax.fori_loop(0, steps - 2, body, 0)
        values((steps - 2) * units)
        softmax((steps - 1) * units)
        values((steps - 1) * units)

    run_pipeline(nb, make_stages(True))
    run_pipeline(item_q_ref.shape[0], make_stages(False))

    def finish(i0, carry):
        for u in range(MOBA_FINISH_UNROLL):
            i = i0 * MOBA_FINISH_UNROLL + u
            qs = pl.ds(pl.multiple_of(i * blk, blk), blk)
            acc = acc_ref[i]
            inv = 1.0 / acc[hd:hd + 1, :]
            o_ref[qs, :] = (acc[0:hd] * inv).T.astype(o_ref.dtype)
        return carry
    lax.fori_loop(0, nb // MOBA_FINISH_UNROLL, finish, 0)


def _moba(qkv, slopes):
    b, _, _, s, _ = qkv.shape
    blk = MOBA_BLOCK
    nb = s // blk
    hd = MOBA_HD
    assert nb + 3 <= hd and s % MOBA_SEL_CHUNK == 0
    groups = [(i, j) for i in range(nb) for j in range(0, i, MOBA_GROUP)]
    own_units = MOBA_OWN_STEP_BLOCKS
    grp_units = MOBA_STEP_BLOCKS // MOBA_GROUP
    assert MOBA_STEP_BLOCKS % MOBA_GROUP == 0 and nb % MOBA_GROUP == 0
    assert MOBA_OWN_STEP_BLOCKS <= MOBA_STEP_BLOCKS and grp_units <= own_units
    assert nb % own_units == 0 and nb // own_units >= 2
    assert len(groups) % grp_units == 0 and len(groups) // grp_units >= 2
    assert nb % MOBA_FINISH_UNROLL == 0
    item_q = jnp.asarray(np.array([g[0] for g in groups], np.int32))
    item_k = jnp.asarray(np.array([g[1] for g in groups], np.int32))
    step_rows = MOBA_STEP_BLOCKS * blk
    vmem = (2 * 5 * s * hd * 2
            + s * (hd + MOBA_SUM_ROWS) * 2 + 2 * 2 * s * hd * 2
            + nb * (hd + MOBA_SUM_ROWS) * blk * 4
            + step_rows * blk * 6
            + 2 * step_rows * blk * 4 + (4 << 20))
    smem = pl.BlockSpec(memory_space=pltpu.SMEM)
    return pl.pallas_call(
        _moba_kernel,
        grid=(b, MOBA_HEADS),
        in_specs=[
            smem, smem,
            pl.BlockSpec((None, None, None, s, hd), lambda bi, h: (bi, 0, h, 0, 0)),
            pl.BlockSpec((None, None, None, s, hd), lambda bi, h: (bi, 1, h, 0, 0)),
            pl.BlockSpec((None, None, None, s, hd), lambda bi, h: (bi, 2, h, 0, 0)),
            pl.BlockSpec((None, s, hd), lambda bi, h: (h, 0, 0)),
        ],
        out_specs=pl.BlockSpec((None, None, s, hd), lambda bi, h: (bi, h, 0, 0)),
        out_shape=jax.ShapeDtypeStruct((b, MOBA_HEADS, s, hd), BF16),
        scratch_shapes=[
            pltpu.VMEM((hd + MOBA_SUM_ROWS, s), BF16),
            pltpu.VMEM((2 * hd, s), BF16),
            pltpu.VMEM((s, 2 * hd), BF16),
            pltpu.VMEM((nb, hd), F32),
            pltpu.VMEM((3 * nb, hd), BF16),
            pltpu.VMEM((nb, blk), F32),
            pltpu.VMEM((nb, hd + MOBA_SUM_ROWS, blk), F32),
            pltpu.VMEM((step_rows, blk), F32),
            pltpu.VMEM((own_units, 1, blk), F32),
            pltpu.VMEM((step_rows, blk), BF16),
            pltpu.VMEM((own_units, 1, blk), F32),
        ],
        compiler_params=pltpu.CompilerParams(
            dimension_semantics=("parallel", "parallel"),
            vmem_limit_bytes=_vmem_limit(vmem)),
        name="moba",
    )(item_q, item_k, qkv, qkv, qkv, _moba_key_table(slopes, s))


def _merge_kernel(x_ref, oa_ref, ob_ref, ga_ref, gb_ref, wa_ref, wb_ref, wo_ref, o_ref):
    ya = _dot(oa_ref[...], wa_ref[...])
    ob = jnp.concatenate([ob_ref[h] for h in range(MOBA_HEADS)], axis=1)
    yb = _dot(ob, wb_ref[...])
    mix = (jax.nn.sigmoid(ga_ref[...].astype(F32)) * ya
           + jax.nn.sigmoid(gb_ref[...].astype(F32)) * yb)
    o_ref[...] = x_ref[...] + _dot(mix.astype(BF16), wo_ref[...])


def _merge(x, oa, ob, proj, wa, wb, wo):
    t = x.shape[0]
    tm = MERGE_TM
    rpb = ob.shape[2] // tm
    const = dict(pipeline_mode=pl.Buffered(1))
    vmem = (2 * 2 * tm * D_MODEL * 4 + 2 * 2 * tm * GLA_V_W * 2 + 2 * 2 * tm * D_MODEL * 2
            + (2 * GLA_V_W + D_MODEL) * D_MODEL * 2 + 4 * tm * D_MODEL * 4 + (4 << 20))
    return pl.pallas_call(
        _merge_kernel,
        grid=(t // tm,),
        in_specs=[
            pl.BlockSpec((tm, D_MODEL), lambda i: (i, 0)),
            pl.BlockSpec((tm, GLA_V_W), lambda i: (i, 0)),
            pl.BlockSpec((None, MOBA_HEADS, tm, MOBA_HD), lambda i: (i // rpb, 0, i % rpb, 0)),
            pl.BlockSpec((tm, D_MODEL), lambda i: (i, COL_GA // D_MODEL)),
            pl.BlockSpec((tm, D_MODEL), lambda i: (i, COL_GB // D_MODEL)),
            pl.BlockSpec((GLA_V_W, D_MODEL), lambda i: (0, 0), **const),
            pl.BlockSpec((MOBA_W, D_MODEL), lambda i: (0, 0), **const),
            pl.BlockSpec((D_MODEL, D_MODEL), lambda i: (0, 0), **const),
        ],
        out_specs=pl.BlockSpec((tm, D_MODEL), lambda i: (i, 0)),
        out_shape=jax.ShapeDtypeStruct((t, D_MODEL), F32),
        compiler_params=pltpu.CompilerParams(
            dimension_semantics=("parallel",),
            vmem_limit_bytes=_vmem_limit(vmem)),
        name="merge",
    )(x, oa, ob, proj, proj, wa, wb, wo)


def kernel(x, ffn1_norm, ffn1_w_gate, ffn1_w_up, ffn1_w_down, mix_norm, w_in, gla_w_alpha,
           gla_b_alpha, gla_out_norm, w_branch_gla, w_branch_moba, w_out, ffn2_norm,
           ffn2_w_gate, ffn2_w_up, ffn2_w_down, final_norm):
    bsz, seq, d = x.shape
    t = bsz * seq
    depth = ffn1_norm.shape[0]
    assert depth >= 1 and d == D_MODEL and seq % MOBA_BLOCK == 0
    assert seq % (GLA_CHUNK * GLA_STEP_CHUNKS) == 0
    assert t % FFN_TM == 0 and seq % PROJ_TM == 0 and seq % MERGE_TM == 0
    assert OUT_W % PROJ_TN == 0
    assert PROJ_TM % LA_ROWS == 0 and PROJ_TM // LA_ROWS <= PROJ_W // PROJ_TN

    slopes = jnp.exp2(-8.0 * jnp.arange(1, MOBA_HEADS + 1, dtype=F32) / MOBA_HEADS)
    col_scale = jnp.ones((1, PROJ_W), F32)
    col_scale = col_scale.at[:, HEAD_COLS + COL_GQ:HEAD_COLS + COL_GQ + GLA_QK_W].set(GLA_DK ** -0.5)
    col_scale = col_scale.at[:, :MOBA_W].set(MOBA_HD ** -0.5 * LOG2E)
    final_g = final_norm.reshape(1, D_MODEL)

    xf = x.reshape(t, D_MODEL)
    for l in range(depth):
        last = l == depth - 1
        w_in_t = jnp.swapaxes(w_in, 1, 2)
        w_main = _regroup(w_in_t, l)
        w_lr = jnp.pad(jnp.tile(w_in_t[l, slice(*W_IN_LR), :], (3, 1)),
                       ((0, LANES_V7X - 3 * GLA_RANK), (0, 0))).astype(BF16)
        wa_hi = _trunc_bf16(gla_w_alpha[l])
        wa_mid = _trunc_bf16(gla_w_alpha[l] - wa_hi)
        w_alpha = jnp.pad(jnp.concatenate([wa_hi, wa_mid, wa_hi], axis=0),
                          ((0, LANES_V7X - 3 * GLA_RANK), (0, 0))).astype(BF16)

        xf = _ffn(xf, ffn1_norm[l].reshape(1, D_MODEL), ffn1_w_gate[l].astype(BF16),
                  ffn1_w_up[l].astype(BF16), ffn1_w_down[l].astype(BF16), final_g, False)
        proj, qkv, log_a = _in_proj(xf, mix_norm[l].reshape(1, D_MODEL), w_main, col_scale,
                                    w_lr, w_alpha, gla_b_alpha[l].reshape(1, GLA_QK_W), bsz)
        proj3 = proj.reshape(bsz, seq, OUT_W)
        oa = _gla(proj3, log_a.reshape(bsz, seq, GLA_QK_W), gla_out_norm[l].reshape(1, GLA_DV))
        ob = _moba(qkv, slopes)
        xf = _merge(xf, oa.reshape(t, GLA_V_W), ob, proj,
                    w_branch_gla[l].astype(BF16), w_branch_moba[l].astype(BF16),
                    w_out[l].astype(BF16))
        xf = _ffn(xf, ffn2_norm[l].reshape(1, D_MODEL), ffn2_w_gate[l].astype(BF16),
                  ffn2_w_up[l].astype(BF16), ffn2_w_down[l].astype(BF16), final_g, last)
    return xf.reshape(bsz, seq, D_MODEL)
```

```python
import functools

import jax
import jax.numpy as jnp
import numpy as np
from jax import lax
from jax.experimental import pallas as pl
from jax.experimental.pallas import tpu as pltpu

F32 = jnp.float32
BF16 = jnp.bfloat16

D_MODEL = 2048
D_FF = 5632
GLA_HEADS = 4
GLA_DK = 128
GLA_DV = 256
GLA_RANK = 16
GLA_TAU = 16.0
MOBA_HEADS = 8
MOBA_HD = 128
MOBA_BLOCK = 256
MOBA_TOPK = 3
NORM_EPS = 1e-6

GLA_QK_W = GLA_HEADS * GLA_DK
GLA_V_W = GLA_HEADS * GLA_DV
MOBA_W = MOBA_HEADS * MOBA_HD

HEAD_COLS = 3 * MOBA_W
COL_GA = 0
COL_GB = COL_GA + D_MODEL
COL_GQ = COL_GB + D_MODEL
COL_GK = COL_GQ + GLA_QK_W
COL_GV = COL_GK + GLA_QK_W
COL_GR = COL_GV + GLA_V_W
OUT_W = COL_GR + GLA_V_W
PROJ_W = HEAD_COLS + OUT_W
W_IN_GLA = (0, 2 * GLA_QK_W + 2 * GLA_V_W)
W_IN_LR = (W_IN_GLA[1], W_IN_GLA[1] + GLA_RANK)
W_IN_MOBA = (W_IN_LR[1], W_IN_LR[1] + 3 * MOBA_W)
W_IN_GATES = (W_IN_MOBA[1], W_IN_MOBA[1] + 2 * D_MODEL)

LANES_V7X = 128
SUBLANES_V7X = 8
VMEM_BYTES_V7X = 64 << 20
VMEM_CAP = VMEM_BYTES_V7X - (6 << 20)

FFN_TM = 1024
FFN_TF = 512
PROJ_TM = 1024
PROJ_TN = MOBA_W
HEAD_TILES = HEAD_COLS // PROJ_TN
LA_ROWS = 128
GLA_CHUNK = 256
GLA_STEP_CHUNKS = 2
GLA_BASE = 2 * SUBLANES_V7X
MERGE_TM = 512
NORM_ROWS = 256
AHEAD_ROWS = 128

MASK_VALUE = -1e30
MOBA_STEP_BLOCKS = 32
MOBA_OWN_STEP_BLOCKS = 16
MOBA_SUM_ROWS = 2 * SUBLANES_V7X
MOBA_GROUP = 4
MOBA_SEL_CHUNK = 1024
MOBA_FINISH_UNROLL = 8
MOBA_PREP_UNROLL = 4
LOG2E = 1.4426950408889634


def _vmem_limit(nbytes):
    return int(min(VMEM_CAP, nbytes))


def _dot(a, b):
    return jnp.dot(a, b, preferred_element_type=F32)


def _dot_nt(a, b):
    return lax.dot_general(a, b, (((1,), (1,)), ((), ())), preferred_element_type=F32)


def _dot_tn(a, b):
    return lax.dot_general(a, b, (((0,), (0,)), ((), ())), preferred_element_type=F32)


def _rms_norm_slice(x_ref, g_ref, out_ref, sl):
    x = x_ref[sl, :].astype(F32)
    ms = jnp.mean(x * x, axis=-1, keepdims=True)
    out_ref[sl, :] = (x * lax.rsqrt(ms + NORM_EPS) * g_ref[...]).astype(out_ref.dtype)


def _rms_norm_rows(x_ref, g_ref, out_ref, rows):
    def body(r, carry):
        _rms_norm_slice(x_ref, g_ref, out_ref,
                        pl.ds(pl.multiple_of(r * NORM_ROWS, NORM_ROWS), NORM_ROWS))
        return carry
    lax.fori_loop(0, rows // NORM_ROWS, body, 0)


def _next_row_index(i, j, n_rows):
    return jnp.minimum(i + jnp.minimum(j, 1), n_rows - 1)


def _with_norm_ahead(x_ref, g_ref, xn_even_ref, xn_odd_ref, rows, step):
    i = pl.program_id(0)
    j = pl.program_id(1)

    @pl.when(jnp.logical_and(i == 0, j == 0))
    def _():
        _rms_norm_rows(x_ref, g_ref, xn_even_ref, rows)

    r = jnp.clip(j - 1, 0, rows // AHEAD_ROWS - 1)
    sl = pl.ds(pl.multiple_of(r * AHEAD_ROWS, AHEAD_ROWS), AHEAD_ROWS)
    for parity, cur, nxt in ((0, xn_even_ref, xn_odd_ref), (1, xn_odd_ref, xn_even_ref)):
        @pl.when(i % 2 == parity)
        def _(cur=cur, nxt=nxt):
            step(cur, functools.partial(_rms_norm_slice, x_ref, g_ref, nxt, sl))


def _ffn_kernel(x_ref, g_ref, wg_ref, wu_ref, wd_ref, fg_ref, o_ref, xn_ref, *, final_norm):
    j = pl.program_id(1)

    @pl.when(j == 0)
    def _():
        _rms_norm_rows(x_ref, g_ref, xn_ref, FFN_TM)
        o_ref[...] = x_ref[...]

    xn = xn_ref[...]
    hg = _dot(xn, wg_ref[...])
    hu = _dot(xn, wu_ref[...])
    a = (0.5 * hg * jax.nn.sigmoid(hg)) * hu
    o_ref[...] += _dot(a.astype(BF16), wd_ref[...])

    if final_norm:
        @pl.when(j == pl.num_programs(1) - 1)
        def _():
            _rms_norm_rows(o_ref, fg_ref, o_ref, FFN_TM)


def _ffn(x, norm_g, wg, wu, wd, final_g, final_norm):
    t = x.shape[0]
    vmem = (2 * 2 * FFN_TM * D_MODEL * 4
            + FFN_TM * D_MODEL * 2
            + 2 * 3 * D_MODEL * FFN_TF * 2
            + 4 * FFN_TM * FFN_TF * 4
            + (4 << 20))
    return pl.pallas_call(
        functools.partial(_ffn_kernel, final_norm=final_norm),
        grid=(t // FFN_TM, D_FF // FFN_TF),
        in_specs=[
            pl.BlockSpec((FFN_TM, D_MODEL), lambda i, j: (i, 0)),
            pl.BlockSpec((1, D_MODEL), lambda i, j: (0, 0)),
            pl.BlockSpec((D_MODEL, FFN_TF), lambda i, j: (0, j)),
            pl.BlockSpec((D_MODEL, FFN_TF), lambda i, j: (0, j)),
            pl.BlockSpec((FFN_TF, D_MODEL), lambda i, j: (j, 0)),
            pl.BlockSpec((1, D_MODEL), lambda i, j: (0, 0)),
        ],
        out_specs=pl.BlockSpec((FFN_TM, D_MODEL), lambda i, j: (i, 0)),
        out_shape=jax.ShapeDtypeStruct((t, D_MODEL), F32),
        scratch_shapes=[pltpu.VMEM((FFN_TM, D_MODEL), BF16)],
        compiler_params=pltpu.CompilerParams(
            dimension_semantics=("parallel", "arbitrary"),
            vmem_limit_bytes=_vmem_limit(vmem)),
        name="ffn_final" if final_norm else "ffn",
    )(x, norm_g, wg, wu, wd, final_g)


def _regroup_kernel(w_ref, o_ref):
    o_ref[...] = w_ref[0].astype(BF16)


def _regroup(w_t, layer):
    d = w_t.shape[2]
    n_head_gate = (W_IN_GATES[1] - W_IN_MOBA[0]) // PROJ_TN

    def src_row(r):
        row = jnp.where(r < n_head_gate, W_IN_MOBA[0] + r * PROJ_TN,
                        W_IN_GLA[0] + (r - n_head_gate) * PROJ_TN)
        return pl.multiple_of(row, SUBLANES_V7X)

    return pl.pallas_call(
        _regroup_kernel,
        grid=(PROJ_W // PROJ_TN,),
        in_specs=[pl.BlockSpec((pl.Element(1), pl.Element(PROJ_TN), pl.Element(d)),
                               lambda r: (layer, src_row(r), 0))],
        out_specs=pl.BlockSpec((PROJ_TN, d), lambda r: (r, 0)),
        out_shape=jax.ShapeDtypeStruct((PROJ_W, d), BF16),
        compiler_params=pltpu.CompilerParams(
            dimension_semantics=("parallel",),
            vmem_limit_bytes=_vmem_limit(2 * PROJ_TN * d * 6 + PROJ_TN * d * 4 + (4 << 20))),
        name="regroup",
    )(w_t)


def _log_sigmoid(z):
    return jnp.minimum(z, 0.0) - jnp.log1p(jnp.exp(-jnp.abs(z)))


def _in_proj_kernel(x_ref, g_ref, w_ref, cs_ref, wlr_ref, wa_ref, ba_ref,
                    proj_ref, heads_ref, la_ref, xn_even_ref, xn_odd_ref, lr_ref):
    j = pl.program_id(1)

    def step(xn_ref, norm_next):
        @pl.when(j == 0)
        def _():
            lr_ref[...] = _dot_nt(xn_ref[...], wlr_ref[...])

        norm_next()

        rows = pl.ds(pl.multiple_of(jnp.minimum(j, PROJ_TM // LA_ROWS - 1) * LA_ROWS, LA_ROWS),
                     LA_ROWS)
        lr = lr_ref[rows, :]
        lr_hi = lr.astype(BF16).astype(F32)
        lane = lax.broadcasted_iota(jnp.int32, lr.shape, 1)
        packed = jnp.where(lane < 2 * GLA_RANK, lr_hi, lr - lr_hi).astype(BF16)
        z = _dot(packed, wa_ref[...]) + ba_ref[...]
        la_ref[rows, :] = _log_sigmoid(z) * (1.0 / GLA_TAU)

        res = (_dot_nt(xn_ref[...], w_ref[...]) * cs_ref[...]).astype(BF16)
        proj_ref[...] = res
        for h in range(MOBA_HEADS):
            heads_ref[h] = res[:, h * MOBA_HD:(h + 1) * MOBA_HD]

    _with_norm_ahead(x_ref, g_ref, xn_even_ref, xn_odd_ref, PROJ_TM, step)


def _in_proj(x, norm_g, w_main, col_scale, w_lr, w_alpha, b_alpha, bsz):
    t = x.shape[0]
    rpb = t // bsz // PROJ_TM
    assert PROJ_W // PROJ_TN > PROJ_TM // AHEAD_ROWS and 3 * GLA_RANK <= LANES_V7X
    vmem = (2 * PROJ_TM * D_MODEL * 4 + 2 * PROJ_TM * D_MODEL * 2
            + 2 * D_MODEL * PROJ_TN * 2 + 2 * 2 * PROJ_TM * PROJ_TN * 2
            + 2 * PROJ_TM * GLA_QK_W * 4 + 2 * PROJ_TM * PROJ_TN * 4
            + 2 * D_MODEL * LANES_V7X * 2 + PROJ_TM * LANES_V7X * 4 + (4 << 20))
    return pl.pallas_call(
        _in_proj_kernel,
        grid=(t // PROJ_TM, PROJ_W // PROJ_TN),
        in_specs=[
            pl.BlockSpec((PROJ_TM, D_MODEL), lambda i, j: (_next_row_index(i, j, t // PROJ_TM), 0)),
            pl.BlockSpec((1, D_MODEL), lambda i, j: (0, 0)),
            pl.BlockSpec((PROJ_TN, D_MODEL), lambda i, j: (j, 0)),
            pl.BlockSpec((1, PROJ_TN), lambda i, j: (0, j)),
            pl.BlockSpec((LANES_V7X, D_MODEL), lambda i, j: (0, 0)),
            pl.BlockSpec((LANES_V7X, GLA_QK_W), lambda i, j: (0, 0)),
            pl.BlockSpec((1, GLA_QK_W), lambda i, j: (0, 0)),
        ],
        out_specs=[
            pl.BlockSpec((PROJ_TM, PROJ_TN), lambda i, j: (i, jnp.maximum(j - HEAD_TILES, 0))),
            pl.BlockSpec((None, None, MOBA_HEADS, PROJ_TM, MOBA_HD),
                         lambda i, j: (i // rpb, jnp.minimum(j, HEAD_TILES), 0, i % rpb, 0)),
            pl.BlockSpec((PROJ_TM, GLA_QK_W), lambda i, j: (i, 0)),
        ],
        out_shape=[
            jax.ShapeDtypeStruct((t, OUT_W), BF16),
            jax.ShapeDtypeStruct((bsz, HEAD_TILES + 1, MOBA_HEADS, t // bsz, MOBA_HD), BF16),
            jax.ShapeDtypeStruct((t, GLA_QK_W), F32),
        ],
        scratch_shapes=[pltpu.VMEM((PROJ_TM, D_MODEL), BF16),
                        pltpu.VMEM((PROJ_TM, D_MODEL), BF16),
                        pltpu.VMEM((PROJ_TM, LANES_V7X), F32)],
        compiler_params=pltpu.CompilerParams(
            dimension_semantics=("arbitrary", "arbitrary"),
            vmem_limit_bytes=_vmem_limit(vmem)),
        name="in_proj",
    )(x, norm_g, w_main, col_scale, w_lr, w_alpha, b_alpha)


def _bcast_rows(v, s, c):
    return jnp.broadcast_to(v, (c // s, s, GLA_DK)).reshape(c, GLA_DK)


def _gla_head(q, k, v, la, gr, gn, st_ref):
    c = q.shape[0]
    row = lax.broadcasted_iota(jnp.int32, (c, 1), 0)
    ri = lax.broadcasted_iota(jnp.int32, (c, c), 0)
    ci = lax.broadcasted_iota(jnp.int32, (c, c), 1)
    rxc = ri ^ ci

    def next_level(cs, s):
        ends = cs.reshape(c // s, s, GLA_DK)[:, s - 1:s, :]
        prev = jnp.concatenate([jnp.zeros_like(ends[:1]), ends[:-1]], axis=0)
        odd = ((row // s) & 1) == 1
        return ends, odd, cs + jnp.where(odd, _bcast_rows(prev, s, c), 0.0)

    cs = la.reshape(c // SUBLANES_V7X, SUBLANES_V7X, GLA_DK)
    sub = lax.broadcasted_iota(jnp.int32, cs.shape, 1)
    sh = 1
    while sh < SUBLANES_V7X:
        cs = cs + jnp.where(sub >= sh, pltpu.roll(cs, sh, axis=1), 0.0)
        sh *= 2
    cs = cs.reshape(c, GLA_DK)
    s = SUBLANES_V7X
    while s < GLA_BASE:
        _, _, cs = next_level(cs, s)
        s *= 2

    p = _dot_nt((q * jnp.exp(cs)).astype(BF16), (k * jnp.exp(-cs)).astype(BF16))
    attn = jnp.where(ci <= ri, p, 0.0)
    while s < c:
        ends, odd, cs_next = next_level(cs, s)
        e = jnp.exp(jnp.where(odd, cs, _bcast_rows(ends, s, c) - cs))
        qe = jnp.where(odd, e, 0.0)
        ke = jnp.where(odd, 0.0, e)
        p = _dot_nt((q * qe).astype(BF16), (k * ke).astype(BF16))
        attn = jnp.where(rxc < s, attn, p)
        cs = cs_next
        s *= 2

    b_last = cs[c - 1:c, :]
    st = st_ref[...]
    o = _dot_nt((q * jnp.exp(cs)).astype(BF16), st.astype(BF16))
    o = o + _dot(attn.astype(BF16), v)
    kd = (k * jnp.exp(b_last - cs)).astype(BF16)
    st_ref[...] = st * jnp.exp(b_last) + _dot_tn(v, kd)

    ms = jnp.mean(o * o, axis=-1, keepdims=True)
    y = o * lax.rsqrt(ms + NORM_EPS) * gn
    g = gr.astype(F32)
    return y * (g * jax.nn.sigmoid(g))


def _gla_kernel(q_ref, k_ref, v_ref, gr_ref, la_ref, gn_ref, o_ref, st_ref):
    @pl.when(pl.program_id(1) == 0)
    def _():
        st_ref[...] = jnp.zeros_like(st_ref)

    gn = gn_ref[...]
    for n in range(GLA_STEP_CHUNKS):
        rs = slice(n * GLA_CHUNK, (n + 1) * GLA_CHUNK)
        for h in range(GLA_HEADS):
            ks = slice(h * GLA_DK, (h + 1) * GLA_DK)
            vs = slice(h * GLA_DV, (h + 1) * GLA_DV)
            out = _gla_head(q_ref[rs, ks].astype(F32), k_ref[rs, ks].astype(F32), v_ref[rs, vs],
                            la_ref[rs, ks], gr_ref[rs, vs], gn, st_ref.at[h])
            o_ref[rs, vs] = out.astype(o_ref.dtype)


def _gla(proj3, la3, gn):
    b, s, _ = proj3.shape
    c = GLA_CHUNK * GLA_STEP_CHUNKS
    vmem = (2 * c * (2 * GLA_QK_W + 3 * GLA_V_W) * 2 + 2 * c * GLA_QK_W * 4
            + GLA_HEADS * GLA_DV * GLA_DK * 4
            + 16 * GLA_CHUNK * GLA_CHUNK * 4 + (4 << 20))
    return pl.pallas_call(
        _gla_kernel,
        grid=(b, s // c),
        in_specs=[
            pl.BlockSpec((None, c, GLA_QK_W), lambda i, j: (i, j, COL_GQ // GLA_QK_W)),
            pl.BlockSpec((None, c, GLA_QK_W), lambda i, j: (i, j, COL_GK // GLA_QK_W)),
            pl.BlockSpec((None, c, GLA_V_W), lambda i, j: (i, j, COL_GV // GLA_V_W)),
            pl.BlockSpec((None, c, GLA_V_W), lambda i, j: (i, j, COL_GR // GLA_V_W)),
            pl.BlockSpec((None, c, GLA_QK_W), lambda i, j: (i, j, 0)),
            pl.BlockSpec((1, GLA_DV), lambda i, j: (0, 0)),
        ],
        out_specs=pl.BlockSpec((None, c, GLA_V_W), lambda i, j: (i, j, 0)),
        out_shape=jax.ShapeDtypeStruct((b, s, GLA_V_W), BF16),
        scratch_shapes=[pltpu.VMEM((GLA_HEADS, GLA_DV, GLA_DK), F32)],
        compiler_params=pltpu.CompilerParams(
            dimension_semantics=("parallel", "arbitrary"),
            vmem_limit_bytes=_vmem_limit(vmem)),
        name="gla",
    )(proj3, proj3, proj3, proj3, la3, gn)


def _split3_bf16(z):
    hi = z.astype(BF16).astype(F32)
    r = z - hi
    mid = r.astype(BF16).astype(F32)
    lo = (r - mid).astype(BF16).astype(F32)
    return hi, mid, lo


def _trunc_bf16(x):
    bits = lax.bitcast_convert_type(x, jnp.uint32) & jnp.uint32(0xFFFF0000)
    return lax.bitcast_convert_type(bits, F32)


def _moba_key_table(slopes, seq):
    nb = seq // MOBA_BLOCK
    pos = jnp.arange(seq)
    z = (slopes * LOG2E)[:, None] * pos.astype(F32)[None, :]
    hi = _trunc_bf16(z)
    mid = _trunc_bf16(z - hi)
    lo = _trunc_bf16(z - hi - mid)
    lane = jnp.arange(MOBA_HD)
    onehot = (lane[None, :] == (pos // MOBA_BLOCK)[:, None]).astype(F32)
    table = (onehot[None] + hi[..., None] * (lane == nb) + mid[..., None] * (lane == nb + 1)
             + lo[..., None] * (lane == nb + 2))
    return table.astype(BF16)


def _moba_kernel(item_q_ref, item_k_ref, q_ref, k_ref, v_ref, kx_ref, o_ref,
                 vt_ref, qt_ref, ka_ref, kmean_ref, kms_ref, m_ref, acc_ref,
                 s_ref, smax_ref, p_ref, al_ref):
    blk = MOBA_BLOCK
    hd = MOBA_HD
    seq = k_ref.shape[0]
    nb = seq // blk

    def prep(j0, carry):
        for u in range(MOBA_PREP_UNROLL):
            j = j0 * MOBA_PREP_UNROLL + u
            sl = pl.ds(pl.multiple_of(j * blk, blk), blk)
            kj = k_ref[sl, :]
            vt_ref[0:hd, sl] = v_ref[sl, :].T
            qt_ref[0:hd, sl] = q_ref[sl, :].T
            ka_ref[sl, 0:hd] = kj
            ka_ref[sl, hd:2 * hd] = kx_ref[sl, :]
            kmean_ref[pl.ds(j, 1), :] = jnp.mean(kj.astype(F32), axis=0, keepdims=True)
        return carry
    lax.fori_loop(0, nb // MOBA_PREP_UNROLL, prep, 0)
    qt_ref[hd + nb:2 * hd, :] = jnp.ones((hd - nb, seq), BF16)
    kms_ref[...] = jnp.concatenate(_split3_bf16(kmean_ref[...]), axis=0).astype(BF16)

    chunk = MOBA_SEL_CHUNK
    for c in range(seq // chunk):
        live = min(nb, -(-((c + 1) * (chunk // blk) - 1) // SUBLANES_V7X) * SUBLANES_V7X)
        live = max(live, SUBLANES_V7X)
        cs = slice(c * chunk, (c + 1) * chunk)
        g3 = _dot(kms_ref[...], qt_ref[0:hd, cs])
        gate = g3[0:live] + g3[nb:nb + live] + g3[2 * nb:2 * nb + live]
        bidx = lax.broadcasted_iota(jnp.int32, gate.shape, 0)
        qblk = c * (chunk // blk) + lax.broadcasted_iota(jnp.int32, gate.shape, 1) // blk
        gate = jnp.where(bidx < qblk, gate, -jnp.inf)
        mask = jnp.full(gate.shape, MASK_VALUE, F32)
        for _ in range(MOBA_TOPK):
            m = jnp.max(gate, axis=0, keepdims=True)
            first = jnp.min(jnp.where(gate == m, bidx, nb), axis=0, keepdims=True)
            pick = bidx == jnp.where(m > -jnp.inf, first, -1)
            mask = jnp.where(pick, 0.0, mask)
            gate = jnp.where(pick, -jnp.inf, gate)
        if live < nb:
            mask = jnp.concatenate([mask, jnp.full((nb - live, chunk), MASK_VALUE, F32)], axis=0)
        qt_ref[hd:hd + nb, cs] = mask.astype(BF16)

    m_ref[...] = jnp.full(m_ref.shape, MASK_VALUE, F32)
    ones_row = lax.broadcasted_iota(jnp.int32, (MOBA_SUM_ROWS, seq), 0) == 0
    vt_ref[hd:hd + MOBA_SUM_ROWS, :] = jnp.where(ones_row, 1.0, 0.0).astype(BF16)

    def clear(i, carry):
        acc_ref[i] = jnp.zeros((hd + MOBA_SUM_ROWS, blk), F32)
        return carry
    lax.fori_loop(0, nb, clear, 0)

    causal = (lax.broadcasted_iota(jnp.int32, (blk, blk), 0)
              <= lax.broadcasted_iota(jnp.int32, (blk, blk), 1))
    tail_row = lax.broadcasted_iota(jnp.int32, (hd, blk), 0)
    own_tail = jnp.where(tail_row < nb, 0.0, 1.0).astype(BF16)

    def make_stages(own_block):
        rows = blk if own_block else MOBA_GROUP * blk
        units = (MOBA_OWN_STEP_BLOCKS if own_block else MOBA_STEP_BLOCKS) * blk // rows

        def item(n):
            if own_block:
                i = j = n
            else:
                i = item_q_ref[n]
                j = item_k_ref[n]
            return (i, pl.ds(pl.multiple_of(i * blk, blk), blk),
                    pl.ds(pl.multiple_of(j * blk, blk), rows))

        def scores(n0):
            for u in range(units):
                i, qs, ks = item(n0 + u)
                if own_block:
                    rhs = jnp.concatenate([qt_ref[0:hd, qs], own_tail], axis=0)
                    s = jnp.where(causal, _dot(ka_ref[ks, :], rhs), MASK_VALUE)
                else:
                    s = _dot(ka_ref[ks, :], qt_ref[:, qs])
                s_ref[u * rows:(u + 1) * rows, :] = s
                smax_ref[u] = jnp.max(s, axis=0, keepdims=True)

        def softmax(n0):
            for u in range(units):
                i, _, _ = item(n0 + u)
                row = pl.ds(i, 1)
                m_old = m_ref[row, :]
                m_new = jnp.maximum(m_old, smax_ref[u])
                alpha = jnp.exp2(m_old - m_new)
                p = jnp.exp2(s_ref[u * rows:(u + 1) * rows, :] - m_new)
                m_ref[row, :] = m_new
                p_ref[u * rows:(u + 1) * rows, :] = p.astype(BF16)
                al_ref[u] = alpha

        def values(n0):
            for u in range(units):
                i, _, ks = item(n0 + u)
                acc_ref[i] = (al_ref[u] * acc_ref[i]
                              + _dot(vt_ref[:, ks], p_ref[u * rows:(u + 1) * rows, :]))

        return units, scores, softmax, values

    def run_pipeline(n_items, stages):
        units, scores, softmax, values = stages
        steps = n_items // units
        scores(0)
        softmax(0)
        scores(units)

        def body(t, carry):
            values(t * units)
            softmax((t + 1) * units)
            scores((t + 2) * units)
            return carry
        lax.fori_loop(0, steps - 2, body, 0)
        values((steps - 2) * units)
        softmax((steps - 1) * units)
        values((steps - 1) * units)

    run_pipeline(nb, make_stages(True))
    run_pipeline(item_q_ref.shape[0], make_stages(False))

    def finish(i0, carry):
        for u in range(MOBA_FINISH_UNROLL):
            i = i0 * MOBA_FINISH_UNROLL + u
            qs = pl.ds(pl.multiple_of(i * blk, blk), blk)
            acc = acc_ref[i]
            inv = 1.0 / acc[hd:hd + 1, :]
            o_ref[qs, :] = (acc[0:hd] * inv).T.astype(o_ref.dtype)
        return carry
    lax.fori_loop(0, nb // MOBA_FINISH_UNROLL, finish, 0)


def _moba(qkv, slopes):
    b, _, _, s, _ = qkv.shape
    blk = MOBA_BLOCK
    nb = s // blk
    hd = MOBA_HD
    assert nb + 3 <= hd and s % MOBA_SEL_CHUNK == 0
    groups = [(i, j) for i in range(nb) for j in range(0, i, MOBA_GROUP)]
    own_units = MOBA_OWN_STEP_BLOCKS
    grp_units = MOBA_STEP_BLOCKS // MOBA_GROUP
    assert MOBA_STEP_BLOCKS % MOBA_GROUP == 0 and nb % MOBA_GROUP == 0
    assert MOBA_OWN_STEP_BLOCKS <= MOBA_STEP_BLOCKS and grp_units <= own_units
    assert nb % own_units == 0 and nb // own_units >= 2
    assert len(groups) % grp_units == 0 and len(groups) // grp_units >= 2
    assert nb % MOBA_FINISH_UNROLL == 0
    item_q = jnp.asarray(np.array([g[0] for g in groups], np.int32))
    item_k = jnp.asarray(np.array([g[1] for g in groups], np.int32))
    step_rows = MOBA_STEP_BLOCKS * blk
    vmem = (2 * 5 * s * hd * 2
            + s * (hd + MOBA_SUM_ROWS) * 2 + 2 * 2 * s * hd * 2
            + nb * (hd + MOBA_SUM_ROWS) * blk * 4
            + step_rows * blk * 6
            + 2 * step_rows * blk * 4 + (4 << 20))
    smem = pl.BlockSpec(memory_space=pltpu.SMEM)
    return pl.pallas_call(
        _moba_kernel,
        grid=(b, MOBA_HEADS),
        in_specs=[
            smem, smem,
            pl.BlockSpec((None, None, None, s, hd), lambda bi, h: (bi, 0, h, 0, 0)),
            pl.BlockSpec((None, None, None, s, hd), lambda bi, h: (bi, 1, h, 0, 0)),
            pl.BlockSpec((None, None, None, s, hd), lambda bi, h: (bi, 2, h, 0, 0)),
            pl.BlockSpec((None, s, hd), lambda bi, h: (h, 0, 0)),
        ],
        out_specs=pl.BlockSpec((None, None, s, hd), lambda bi, h: (bi, h, 0, 0)),
        out_shape=jax.ShapeDtypeStruct((b, MOBA_HEADS, s, hd), BF16),
        scratch_shapes=[
            pltpu.VMEM((hd + MOBA_SUM_ROWS, s), BF16),
            pltpu.VMEM((2 * hd, s), BF16),
            pltpu.VMEM((s, 2 * hd), BF16),
            pltpu.VMEM((nb, hd), F32),
            pltpu.VMEM((3 * nb, hd), BF16),
            pltpu.VMEM((nb, blk), F32),
            pltpu.VMEM((nb, hd + MOBA_SUM_ROWS, blk), F32),
            pltpu.VMEM((step_rows, blk), F32),
            pltpu.VMEM((own_units, 1, blk), F32),
            pltpu.VMEM((step_rows, blk), BF16),
            pltpu.VMEM((own_units, 1, blk), F32),
        ],
        compiler_params=pltpu.CompilerParams(
            dimension_semantics=("parallel", "parallel"),
            vmem_limit_bytes=_vmem_limit(vmem)),
        name="moba",
    )(item_q, item_k, qkv, qkv, qkv, _moba_key_table(slopes, s))


def _merge_kernel(x_ref, oa_ref, ob_ref, ga_ref, gb_ref, wa_ref, wb_ref, wo_ref, o_ref):
    ya = _dot(oa_ref[...], wa_ref[...])
    ob = jnp.concatenate([ob_ref[h] for h in range(MOBA_HEADS)], axis=1)
    yb = _dot(ob, wb_ref[...])
    mix = (jax.nn.sigmoid(ga_ref[...].astype(F32)) * ya
           + jax.nn.sigmoid(gb_ref[...].astype(F32)) * yb)
    o_ref[...] = x_ref[...] + _dot(mix.astype(BF16), wo_ref[...])


def _merge(x, oa, ob, proj, wa, wb, wo):
    t = x.shape[0]
    tm = MERGE_TM
    rpb = ob.shape[2] // tm
    const = dict(pipeline_mode=pl.Buffered(1))
    vmem = (2 * 2 * tm * D_MODEL * 4 + 2 * 2 * tm * GLA_V_W * 2 + 2 * 2 * tm * D_MODEL * 2
            + (2 * GLA_V_W + D_MODEL) * D_MODEL * 2 + 4 * tm * D_MODEL * 4 + (4 << 20))
    return pl.pallas_call(
        _merge_kernel,
        grid=(t // tm,),
        in_specs=[
            pl.BlockSpec((tm, D_MODEL), lambda i: (i, 0)),
            pl.BlockSpec((tm, GLA_V_W), lambda i: (i, 0)),
            pl.BlockSpec((None, MOBA_HEADS, tm, MOBA_HD), lambda i: (i // rpb, 0, i % rpb, 0)),
            pl.BlockSpec((tm, D_MODEL), lambda i: (i, COL_GA // D_MODEL)),
            pl.BlockSpec((tm, D_MODEL), lambda i: (i, COL_GB // D_MODEL)),
            pl.BlockSpec((GLA_V_W, D_MODEL), lambda i: (0, 0), **const),
            pl.BlockSpec((MOBA_W, D_MODEL), lambda i: (0, 0), **const),
            pl.BlockSpec((D_MODEL, D_MODEL), lambda i: (0, 0), **const),
        ],
        out_specs=pl.BlockSpec((tm, D_MODEL), lambda i: (i, 0)),
        out_shape=jax.ShapeDtypeStruct((t, D_MODEL), F32),
        compiler_params=pltpu.CompilerParams(
            dimension_semantics=("parallel",),
            vmem_limit_bytes=_vmem_limit(vmem)),
        name="merge",
    )(x, oa, ob, proj, proj, wa, wb, wo)


def kernel(x, ffn1_norm, ffn1_w_gate, ffn1_w_up, ffn1_w_down, mix_norm, w_in, gla_w_alpha,
           gla_b_alpha, gla_out_norm, w_branch_gla, w_branch_moba, w_out, ffn2_norm,
           ffn2_w_gate, ffn2_w_up, ffn2_w_down, final_norm):
    bsz, seq, d = x.shape
    t = bsz * seq
    depth = ffn1_norm.shape[0]
    assert depth >= 1 and d == D_MODEL and seq % MOBA_BLOCK == 0
    assert seq % (GLA_CHUNK * GLA_STEP_CHUNKS) == 0
    assert t % FFN_TM == 0 and seq % PROJ_TM == 0 and seq % MERGE_TM == 0
    assert OUT_W % PROJ_TN == 0
    assert PROJ_TM % LA_ROWS == 0 and PROJ_TM // LA_ROWS <= PROJ_W // PROJ_TN

    slopes = jnp.exp2(-8.0 * jnp.arange(1, MOBA_HEADS + 1, dtype=F32) / MOBA_HEADS)
    col_scale = jnp.ones((1, PROJ_W), F32)
    col_scale = col_scale.at[:, HEAD_COLS + COL_GQ:HEAD_COLS + COL_GQ + GLA_QK_W].set(GLA_DK ** -0.5)
    col_scale = col_scale.at[:, :MOBA_W].set(MOBA_HD ** -0.5 * LOG2E)
    final_g = final_norm.reshape(1, D_MODEL)

    xf = x.reshape(t, D_MODEL)
    for l in range(depth):
        last = l == depth - 1
        w_in_t = jnp.swapaxes(w_in, 1, 2)
        w_main = _regroup(w_in_t, l)
        w_lr = jnp.pad(jnp.tile(w_in_t[l, slice(*W_IN_LR), :], (3, 1)),
                       ((0, LANES_V7X - 3 * GLA_RANK), (0, 0))).astype(BF16)
        wa_hi = _trunc_bf16(gla_w_alpha[l])
        wa_mid = _trunc_bf16(gla_w_alpha[l] - wa_hi)
        w_alpha = jnp.pad(jnp.concatenate([wa_hi, wa_mid, wa_hi], axis=0),
                          ((0, LANES_V7X - 3 * GLA_RANK), (0, 0))).astype(BF16)

        xf = _ffn(xf, ffn1_norm[l].reshape(1, D_MODEL), ffn1_w_gate[l].astype(BF16),
                  ffn1_w_up[l].astype(BF16), ffn1_w_down[l].astype(BF16), final_g, False)
        proj, qkv, log_a = _in_proj(xf, mix_norm[l].reshape(1, D_MODEL), w_main, col_scale,
                                    w_lr, w_alpha, gla_b_alpha[l].reshape(1, GLA_QK_W), bsz)
        proj3 = proj.reshape(bsz, seq, OUT_W)
        oa = _gla(proj3, log_a.reshape(bsz, seq, GLA_QK_W), gla_out_norm[l].reshape(1, GLA_DV))
        ob = _moba(qkv, slopes)
        xf = _merge(xf, oa.reshape(t, GLA_V_W), ob, proj,
                    w_branch_gla[l].astype(BF16), w_branch_moba[l].astype(BF16),
                    w_out[l].astype(BF16))
        xf = _ffn(xf, ffn2_norm[l].reshape(1, D_MODEL), ffn2_w_gate[l].astype(BF16),
                  ffn2_w_up[l].astype(BF16), ffn2_w_down[l].astype(BF16), final_g, last)
    return xf.reshape(bsz, seq, D_MODEL)
```
